```python
import math
import jax
import jax.numpy as jnp
from jax import lax
import numpy as np

D_MODEL = 1024
BATCH = 32
SEQ = 256
DEPTH = 2
DEC_BATCH = 8
DEC_SEQ = 4096
PAST_LEN = 512

GRID_W = 64
HEAD_DIM = 64
ATTN_HEADS = 6
ATTN_KV_HEADS = 2
ROPE_THETA = 10000.0
Q_BLOCK = 128
SSD_HEADS = 6
SSD_HEAD_DIM = 64
SSD_GROUPS = 2
SSD_STATE = 64
SSD_CONV = 5
SSD_CHUNK = 128
RWKV_HEADS = 4
RWKV_HEAD_DIM = 64
DECAY_LORA = 64
AAA_LORA = 64
GATE_LORA = 128
N_EXPERTS = 32
TOP_K = 4
D_EXPERT = 1024
SWIGLU_LIMIT = 7.0
SWIGLU_ALPHA = 1.702
MOE_BLOCK = 128
RMS_EPS = 1e-6
GN_EPS = 64e-5

ATTN_WIDTH = ATTN_HEADS * HEAD_DIM
KV_WIDTH = ATTN_KV_HEADS * HEAD_DIM
SSD_WIDTH = SSD_HEADS * SSD_HEAD_DIM
BC_WIDTH = SSD_GROUPS * SSD_STATE
XBC_WIDTH = SSD_WIDTH + 2 * BC_WIDTH
RWKV_WIDTH = RWKV_HEADS * RWKV_HEAD_DIM
MIX_WIDTH = ATTN_WIDTH + SSD_WIDTH + RWKV_WIDTH
IN_SPLITS = (ATTN_WIDTH, KV_WIDTH, KV_WIDTH, SSD_WIDTH, XBC_WIDTH, 2 * SSD_HEADS,
             RWKV_WIDTH, RWKV_WIDTH, RWKV_WIDTH, DECAY_LORA, AAA_LORA, GATE_LORA)
IN_WIDTH = sum(IN_SPLITS)

kernel_name = 'hybrid_diffusion_prefix_step'

F32 = jnp.float32


def rms_norm(x, w):
    xf = x.astype(F32)
    y = xf * lax.rsqrt(jnp.mean(xf * xf, axis=-1, keepdims=True) + RMS_EPS)
    return (y * w.astype(F32)).astype(x.dtype)


def split_cols(x, sizes):
    out, start = [], 0
    for s in sizes:
        out.append(x[..., start:start + s])
        start += s
    return out


def rope_2d(x):
    T = x.shape[1]
    rows = T // GRID_W
    row = jnp.repeat(jnp.arange(rows), GRID_W).astype(F32)
    col = jnp.tile(jnp.arange(GRID_W), rows).astype(F32)
    half = HEAD_DIM // 2
    quarter = half // 2
    inv_freq = jnp.power(ROPE_THETA, -jnp.arange(0, half, 2, dtype=F32) / half)

    def rot(xh, pos):
        ang = pos[:, None] * inv_freq[None, :]
        cos = jnp.cos(ang)[None, :, None, :]
        sin = jnp.sin(ang)[None, :, None, :]
        x1, x2 = xh[..., :quarter], xh[..., quarter:]
        return jnp.concatenate([x1 * cos - x2 * sin, x2 * cos + x1 * sin], axis=-1)

    xf = x.astype(F32)
    out = jnp.concatenate([rot(xf[..., :half], row), rot(xf[..., half:], col)], axis=-1)
    return out.astype(x.dtype)


def blocked_attention(q, k, v):
    b, T, H, _ = q.shape
    groups = H // ATTN_KV_HEADS
    scale = HEAD_DIM ** -0.5
    qb = q.reshape(b, T // Q_BLOCK, Q_BLOCK, ATTN_KV_HEADS, groups, HEAD_DIM).transpose(1, 0, 2, 3, 4, 5)

    def one_block(q_blk):
        s = jnp.einsum('bqkgd,bskd->bkgqs', q_blk, k).astype(F32) * scale
        p = jax.nn.softmax(s, axis=-1).astype(v.dtype)
        return jnp.einsum('bkgqs,bskd->bqkgd', p, v)

    out = lax.map(one_block, qb)
    return out.transpose(1, 0, 2, 3, 4, 5).reshape(b, T, H * HEAD_DIM)


def centred_dwconv(x, w, bias):
    y = lax.conv_general_dilated(x, w[:, None, :], window_strides=(1,),
                                 padding=[(SSD_CONV // 2, SSD_CONV // 2)],
                                 dimension_numbers=('NWC', 'WIO', 'NWC'),
                                 feature_group_count=x.shape[-1])
    return y + bias


def segsum(x):
    n = x.shape[-1]
    cs = jnp.cumsum(x, axis=-1)
    diff = cs[..., :, None] - cs[..., None, :]
    return jnp.where(jnp.tril(jnp.ones((n, n), dtype=bool)), diff, -jnp.inf)


def ssd_chunked_scan(x, dt, a, bmat, cmat, h0):
    b, T, H, P = x.shape
    nc = T // SSD_CHUNK
    xd = (x.astype(F32) * dt[..., None]).reshape(b, nc, SSD_CHUNK, H, P)
    bm = bmat.astype(F32).reshape(b, nc, SSD_CHUNK, H, SSD_STATE)
    cm = cmat.astype(F32).reshape(b, nc, SSD_CHUNK, H, SSD_STATE)
    la = (dt * a).reshape(b, nc, SSD_CHUNK, H).transpose(0, 3, 1, 2)
    la_cs = jnp.cumsum(la, axis=-1)
    decay_in = jnp.exp(segsum(la))
    scores = jnp.einsum('bclhn,bcshn->bhcls', cm, bm) * decay_in
    y_diag = jnp.einsum('bhcls,bcshp->bclhp', scores, xd)
    decay_to_end = jnp.exp(la_cs[..., -1:] - la_cs)
    chunk_states = jnp.einsum('bclhn,bhcl,bclhp->bchpn', bm, decay_to_end, xd)
    states = jnp.concatenate([h0.astype(F32)[:, None], chunk_states], axis=1)
    chunk_decay = jnp.exp(segsum(jnp.pad(la_cs[..., -1], ((0, 0), (0, 0), (1, 0)))))
    states = jnp.einsum('bhzc,bchpn->bzhpn', chunk_decay, states)
    y_off = jnp.einsum('bclhn,bchpn,bhcl->bclhp', cm, states[:, :-1], jnp.exp(la_cs))
    return (y_diag + y_off).reshape(b, T, H, P), states[:, -1]


def ssd_mixer(xbc, z, dt_raw, lp, init):
    b, T, _ = xbc.shape
    xbc = jax.nn.silu(centred_dwconv(xbc, lp['ssd_conv_w'], lp['ssd_conv_b']))
    xs, bm, cm = split_cols(xbc, (SSD_WIDTH, BC_WIDTH, BC_WIDTH))
    xs = xs.reshape(b, T, SSD_HEADS, SSD_HEAD_DIM)
    rep = SSD_HEADS // SSD_GROUPS
    bm = jnp.repeat(bm.reshape(b, T, SSD_GROUPS, SSD_STATE), rep, axis=2)
    cm = jnp.repeat(cm.reshape(b, T, SSD_GROUPS, SSD_STATE), rep, axis=2)
    dt = jax.nn.softplus(dt_raw.reshape(b, T, 2, SSD_HEADS).astype(F32) + lp['ssd_dt_bias'].astype(F32))
    a = -jnp.exp(lp['ssd_a_log'].astype(F32))
    y_f, s_f = ssd_chunked_scan(xs, dt[:, :, 0], a[0], bm, cm, init[:, 0])
    y_b, s_b = ssd_chunked_scan(xs[:, ::-1], dt[:, ::-1, 1], a[1], bm[:, ::-1], cm[:, ::-1], init[:, 1])
    y = y_f + y_b[:, ::-1] + lp['ssd_d'].astype(F32)[:, None] * xs.astype(F32)
    y = y.reshape(b, T, SSD_WIDTH).astype(z.dtype) * jax.nn.silu(z)
    y = rms_norm(y, lp['ssd_norm_w'])
    return y, jnp.stack([s_f, s_b], axis=1).astype(init.dtype)


def rwkv_scan(r, decay, k, v, kk, a, s0, reverse):
    def step(S, inp):
        r_t, w_t, k_t, v_t, kk_t, a_t = inp
        s_kk = jnp.einsum('bhij,bhj->bhi', S, kk_t)
        S = (S * w_t[:, :, None, :] - s_kk[..., None] * (kk_t * a_t)[:, :, None, :]
             + v_t[..., None] * k_t[:, :, None, :])
        return S, jnp.einsum('bhij,bhj->bhi', S, r_t)

    xs = tuple(jnp.swapaxes(t, 0, 1) for t in (r, decay, k, v, kk, a))
    s_final, ys = lax.scan(step, s0.astype(F32), xs, reverse=reverse)
    return jnp.swapaxes(ys, 0, 1), s_final


def rwkv_mixer(r, k, v, w_lo, a_lo, g_lo, lp, init):
    b, T, _ = r.shape

    def to_heads(t):
        return t.astype(F32).reshape(b, T, RWKV_HEADS, RWKV_HEAD_DIM)

    g = jax.nn.sigmoid(g_lo) @ lp['rwkv_g_up']
    kk = to_heads(k.astype(F32) * lp['rwkv_k_k'].astype(F32))
    kk = kk * lax.rsqrt(jnp.sum(kk * kk, axis=-1, keepdims=True) + 1e-12)
    rh, vh = to_heads(r), to_heads(v)
    ys, finals = [], []
    for d in range(2):
        zw = (lp['rwkv_w0'][d] + jnp.tanh(w_lo) @ lp['rwkv_w_up'][d]).astype(F32)
        decay = jnp.exp(-math.exp(-0.5) * jax.nn.sigmoid(zw))
        a = jax.nn.sigmoid((lp['rwkv_a0'][d] + a_lo @ lp['rwkv_a_up'][d]).astype(F32))
        k_d = k.astype(F32) * (1 + (a - 1) * lp['rwkv_k_a'].astype(F32))
        y_d, s_d = rwkv_scan(rh, to_heads(decay), to_heads(k_d), vh, kk, to_heads(a), init[:, d], d == 1)
        ys.append(y_d)
        finals.append(s_d)
    y = ys[0] + ys[1]
    mu = jnp.mean(y, axis=-1, keepdims=True)
    var = jnp.mean(jnp.square(y - mu), axis=-1, keepdims=True)
    y = ((y - mu) * lax.rsqrt(var + GN_EPS)).reshape(b, T, RWKV_WIDTH)
    y = y * lp['rwkv_ln_w'].astype(F32) + lp['rwkv_ln_b'].astype(F32)
    bonus = jnp.sum(rh * to_heads(k) * lp['rwkv_r_k'].astype(F32), axis=-1, keepdims=True) * vh
    y = (y + bonus.reshape(b, T, RWKV_WIDTH)) * g.astype(F32)
    return y.astype(r.dtype), jnp.stack(finals, axis=1).astype(init.dtype)


def moe_ffn(u, lp):
    b, T, D = u.shape
    n_tok = b * T
    n_assign = n_tok * TOP_K
    n_blocks = n_assign // MOE_BLOCK + N_EXPERTS
    xf = u.reshape(n_tok, D)
    logits = (xf @ lp['router_w'] + lp['router_b']).astype(F32)
    top_logit, top_idx = lax.top_k(logits, TOP_K)
    gates = jax.nn.softmax(top_logit, axis=-1)
    e_flat = top_idx.reshape(-1)
    tok_flat = jnp.repeat(jnp.arange(n_tok, dtype=jnp.int32), TOP_K)
    order = jnp.argsort(e_flat)
    e_sorted = e_flat[order]
    tok_sorted = tok_flat[order]
    gate_sorted = gates.reshape(-1)[order]
    counts = jnp.zeros((N_EXPERTS,), jnp.int32).at[e_flat].add(1)
    padded = (counts + MOE_BLOCK - 1) // MOE_BLOCK * MOE_BLOCK
    start = jnp.cumsum(counts) - counts
    padded_end = jnp.cumsum(padded)
    padded_start = padded_end - padded
    dest = padded_start[e_sorted] + jnp.arange(n_assign, dtype=jnp.int32) - start[e_sorted]
    row_tok = jnp.full((n_blocks * MOE_BLOCK,), n_tok, jnp.int32).at[dest].set(tok_sorted)
    row_gate = jnp.zeros((n_blocks * MOE_BLOCK,), F32).at[dest].set(gate_sorted)
    block_start = jnp.arange(n_blocks, dtype=jnp.int32) * MOE_BLOCK
    block_expert = jnp.minimum(jnp.searchsorted(padded_end, block_start, side='right'), N_EXPERTS - 1)
    x_rows = jnp.concatenate([xf, jnp.zeros((1, D), xf.dtype)], axis=0)[row_tok]
    x_rows = x_rows.reshape(n_blocks, MOE_BLOCK, D)
    w_gu, b_gu = lp['moe_w_gate_up'], lp['moe_b_gate_up']
    w_dn, b_dn = lp['moe_w_down'], lp['moe_b_down']

    def expert_block(args):
        xb, e = args
        gu = xb @ w_gu[e] + b_gu[e]
        gate = jnp.minimum(gu[:, :D_EXPERT], SWIGLU_LIMIT)
        up = jnp.clip(gu[:, D_EXPERT:], -SWIGLU_LIMIT, SWIGLU_LIMIT)
        act = (up + 1) * gate * jax.nn.sigmoid(SWIGLU_ALPHA * gate)
        return act @ w_dn[e] + b_dn[e]

    y_rows = lax.map(expert_block, (x_rows, block_expert)).reshape(n_blocks * MOE_BLOCK, D)
    y = jax.ops.segment_sum(y_rows * row_gate[:, None].astype(y_rows.dtype), row_tok,
                            num_segments=n_tok + 1)
    return y[:n_tok].reshape(b, T, D)


def trunk_layer(h, cond, lp, ctx):
    b, T, _ = h.shape
    mod = jax.nn.silu(cond) @ lp['ada_w'] + lp['ada_b']
    sh1, sc1, g1, sh2, sc2, g2 = jnp.split(mod[:, None, :], 6, axis=-1)
    u = rms_norm(h, lp['norm1_w']) * (1 + sc1) + sh1
    q, k, v, z, xbc, dt_raw, r, kr, vr, w_lo, a_lo, g_lo = split_cols(u @ lp['w_in'], IN_SPLITS)
    q = rms_norm(q.reshape(b, T, ATTN_HEADS, HEAD_DIM), lp['q_norm_w'])
    k = rms_norm(k.reshape(b, T, ATTN_KV_HEADS, HEAD_DIM), lp['k_norm_w'])
    v = v.reshape(b, T, ATTN_KV_HEADS, HEAD_DIM)
    if ctx is None:
        attn = blocked_attention(q, k, v)
        ssd_init = jnp.zeros((b, 2, SSD_HEADS, SSD_HEAD_DIM, SSD_STATE), h.dtype)
        rwkv_init = jnp.zeros((b, 2, RWKV_HEADS, RWKV_HEAD_DIM, RWKV_HEAD_DIM), h.dtype)
    else:
        k_ctx, v_ctx, ssd_init, rwkv_init = ctx
        attn = blocked_attention(rope_2d(q),
                                 jnp.concatenate([k_ctx, rope_2d(k)], axis=1),
                                 jnp.concatenate([v_ctx, v], axis=1))
    ssd_out, ssd_final = ssd_mixer(xbc, z, dt_raw, lp, ssd_init)
    rwkv_out, rwkv_final = rwkv_mixer(r, kr, vr, w_lo, a_lo, g_lo, lp, rwkv_init)
    mixed = jnp.concatenate([attn, ssd_out, rwkv_out], axis=-1) @ lp['w_out']
    h = h + g1 * mixed
    u = rms_norm(h, lp['norm2_w']) * (1 + sc2) + sh2
    h = h + g2 * moe_ffn(u, lp)
    new_ctx = (k, v, ssd_final, rwkv_final) if ctx is None else None
    return h, new_ctx


def setup_inputs(seed: int = 0) -> dict:
    key = jax.random.key(seed)
    keys = jax.random.split(key, 48)
    counter = iter(range(48))

    def nrm(shape, scale):
        return jax.random.normal(keys[next(counter)], shape, F32) * scale

    def gain(shape):
        return 1.0 + nrm(shape, 0.02)

    dt0 = jnp.exp(jax.random.uniform(keys[next(counter)], (DEPTH, 2, SSD_HEADS), F32,
                                     math.log(1e-3), math.log(1e-1)))
    a_init = jax.random.uniform(keys[next(counter)], (DEPTH, 2, SSD_HEADS), F32, 1.0, 16.0)
    return {
        'x_prompt': nrm((BATCH, SEQ, D_MODEL), 1.0),
        'x_sample': nrm((DEC_BATCH, DEC_SEQ, D_MODEL), 1.0),
        'cache_attn_k': nrm((DEC_BATCH, DEPTH, PAST_LEN, ATTN_KV_HEADS, HEAD_DIM), 1.0),
        'cache_attn_v': nrm((DEC_BATCH, DEPTH, PAST_LEN, ATTN_KV_HEADS, HEAD_DIM), 1.0),
        'state_ssd': nrm((DEC_BATCH, DEPTH, 2, SSD_HEADS, SSD_HEAD_DIM, SSD_STATE), 0.3),
        'state_rwkv': nrm((DEC_BATCH, DEPTH, 2, RWKV_HEADS, RWKV_HEAD_DIM, RWKV_HEAD_DIM), 0.3),
        'c': nrm((DEC_BATCH, D_MODEL), 1.0),
        'c_ctx': nrm((D_MODEL,), 1.0),
        'norm1_w': gain((DEPTH, D_MODEL)),
        'norm2_w': gain((DEPTH, D_MODEL)),
        'ada_w': nrm((DEPTH, D_MODEL, 6 * D_MODEL), 0.5 * D_MODEL ** -0.5),
        'ada_b': nrm((DEPTH, 6 * D_MODEL), 0.02),
        'w_in': nrm((DEPTH, D_MODEL, IN_WIDTH), D_MODEL ** -0.5),
        'q_norm_w': gain((DEPTH, HEAD_DIM)),
        'k_norm_w': gain((DEPTH, HEAD_DIM)),
        'ssd_conv_w': nrm((DEPTH, SSD_CONV, XBC_WIDTH), SSD_CONV ** -0.5),
        'ssd_conv_b': nrm((DEPTH, XBC_WIDTH), 0.02),
        'ssd_dt_bias': dt0 + jnp.log(-jnp.expm1(-dt0)),
        'ssd_a_log': jnp.log(a_init),
        'ssd_d': 1.0 + nrm((DEPTH, SSD_HEADS), 0.1),
        'ssd_norm_w': gain((DEPTH, SSD_WIDTH)),
        'rwkv_w0': nrm((DEPTH, 2, RWKV_WIDTH), 0.5),
        'rwkv_w_up': nrm((DEPTH, 2, DECAY_LORA, RWKV_WIDTH), 0.5 * DECAY_LORA ** -0.5),
        'rwkv_a0': nrm((DEPTH, 2, RWKV_WIDTH), 0.1),
        'rwkv_a_up': nrm((DEPTH, 2, AAA_LORA, RWKV_WIDTH), 0.5 * AAA_LORA ** -0.5),
        'rwkv_g_up': nrm((DEPTH, GATE_LORA, RWKV_WIDTH), GATE_LORA ** -0.5),
        'rwkv_k_k': 0.85 + nrm((DEPTH, RWKV_WIDTH), 0.02),
        'rwkv_k_a': gain((DEPTH, RWKV_WIDTH)),
        'rwkv_r_k': nrm((DEPTH, RWKV_HEADS, RWKV_HEAD_DIM), 0.1),
        'rwkv_ln_w': gain((DEPTH, RWKV_WIDTH)),
        'rwkv_ln_b': nrm((DEPTH, RWKV_WIDTH), 0.02),
        'w_out': nrm((DEPTH, MIX_WIDTH, D_MODEL), MIX_WIDTH ** -0.5),
        'router_w': nrm((DEPTH, D_MODEL, N_EXPERTS), D_MODEL ** -0.5),
        'router_b': nrm((DEPTH, N_EXPERTS), 0.01),
        'moe_w_gate_up': nrm((DEPTH, N_EXPERTS, D_MODEL, 2 * D_EXPERT), D_MODEL ** -0.5),
        'moe_b_gate_up': nrm((DEPTH, N_EXPERTS, 2 * D_EXPERT), 0.01),
        'moe_w_down': nrm((DEPTH, N_EXPERTS, D_EXPERT, D_MODEL), D_EXPERT ** -0.5),
        'moe_b_down': nrm((DEPTH, N_EXPERTS, D_MODEL), 0.01),
        'final_norm_w': gain((D_MODEL,)),
    }


def reference(x_prompt, x_sample, cache_attn_k, cache_attn_v, state_ssd, state_rwkv, c, c_ctx,
              norm1_w, norm2_w, ada_w, ada_b, w_in, q_norm_w, k_norm_w,
              ssd_conv_w, ssd_conv_b, ssd_dt_bias, ssd_a_log, ssd_d, ssd_norm_w,
              rwkv_w0, rwkv_w_up, rwkv_a0, rwkv_a_up, rwkv_g_up, rwkv_k_k, rwkv_k_a, rwkv_r_k,
              rwkv_ln_w, rwkv_ln_b, w_out, router_w, router_b,
              moe_w_gate_up, moe_b_gate_up, moe_w_down, moe_b_down, final_norm_w):
    layers = [dict(norm1_w=norm1_w[l], norm2_w=norm2_w[l], ada_w=ada_w[l], ada_b=ada_b[l],
                   w_in=w_in[l], q_norm_w=q_norm_w[l], k_norm_w=k_norm_w[l],
                   ssd_conv_w=ssd_conv_w[l], ssd_conv_b=ssd_conv_b[l], ssd_dt_bias=ssd_dt_bias[l],
                   ssd_a_log=ssd_a_log[l], ssd_d=ssd_d[l], ssd_norm_w=ssd_norm_w[l],
                   rwkv_w0=rwkv_w0[l], rwkv_w_up=rwkv_w_up[l], rwkv_a0=rwkv_a0[l],
                   rwkv_a_up=rwkv_a_up[l], rwkv_g_up=rwkv_g_up[l], rwkv_k_k=rwkv_k_k[l],
                   rwkv_k_a=rwkv_k_a[l], rwkv_r_k=rwkv_r_k[l], rwkv_ln_w=rwkv_ln_w[l],
                   rwkv_ln_b=rwkv_ln_b[l], w_out=w_out[l], router_w=router_w[l], router_b=router_b[l],
                   moe_w_gate_up=moe_w_gate_up[l], moe_b_gate_up=moe_b_gate_up[l],
                   moe_w_down=moe_w_down[l], moe_b_down=moe_b_down[l])
              for l in range(DEPTH)]

    h = x_prompt
    ks, vs, ssd_states, rwkv_states = [], [], [], []
    for l in range(DEPTH):
        h, (k_l, v_l, s_l, w_l) = trunk_layer(h, c_ctx[None, :], layers[l], None)
        ks.append(k_l)
        vs.append(v_l)
        ssd_states.append(s_l)
        rwkv_states.append(w_l)
    y_prompt = rms_norm(h, final_norm_w)
    new_attn_k = jnp.stack(ks, axis=1)
    new_attn_v = jnp.stack(vs, axis=1)
    new_state_ssd = jnp.stack(ssd_states, axis=1)
    new_state_rwkv = jnp.stack(rwkv_states, axis=1)

    h = x_sample
    for l in range(DEPTH):
        h, _ = trunk_layer(h, c, layers[l],
                           (cache_attn_k[:, l], cache_attn_v[:, l], state_ssd[:, l], state_rwkv[:, l]))
    y_sample = rms_norm(h, final_norm_w)
    return (y_prompt, y_sample, new_attn_k, new_attn_v, new_state_ssd, new_state_rwkv)
```

```python
import functools
import math

import numpy as np
import jax
import jax.numpy as jnp
from jax import lax
from jax.experimental import pallas as pl
from jax.experimental.pallas import tpu as pltpu

F32 = jnp.float32
BF16 = jnp.bfloat16
HI = lax.Precision.HIGHEST

D_MODEL = 1024
GRID_W = 64
HEAD_DIM = 64
ATTN_HEADS = 6
ATTN_KV_HEADS = 2
ROPE_THETA = 10000.0
SSD_HEADS = 6
SSD_GROUPS = 2
SSD_STATE = 64
SSD_CONV = 5
SSD_CHUNK = 128
RWKV_HEADS = 4
RWKV_CHUNK = 64
N_EXPERTS = 32
TOP_K = 4
D_EXPERT = 1024
SWIGLU_LIMIT = 7.0
SWIGLU_ALPHA = 1.702
RMS_EPS = 1e-6
GN_EPS = 64e-5

ATTN_WIDTH = ATTN_HEADS * HEAD_DIM
KV_WIDTH = ATTN_KV_HEADS * HEAD_DIM
SSD_WIDTH = SSD_HEADS * HEAD_DIM
XBC_WIDTH = SSD_WIDTH + 2 * SSD_GROUPS * SSD_STATE
RWKV_WIDTH = RWKV_HEADS * HEAD_DIM
LANES = 128

COL_Q, COL_Z, COL_R, COL_KR = 0, 384, 768, 1024
COL_K, COL_V, COL_DT, COL_WALO, COL_GLO = 1280, 1408, 1536, 1664, 1792
COL_XBC, COL_VR = 1920, 2560
PROJ_WIDTH = 2816

TOKEN_TILE = 256
MOE_ROWS = 512
GATHER_ROWS = 512
COMBINE_TILE = 128
VMEM_LIMIT = 56 * 1024 * 1024


def _cparams(*sem):
    return pltpu.CompilerParams(dimension_semantics=sem, vmem_limit_bytes=VMEM_LIMIT)


def _nt(a, b, precision=None):
    return lax.dot_general(a, b, (((1,), (1,)), ((), ())), precision=precision,
                           preferred_element_type=F32)


def _tn(a, b, precision=None):
    return lax.dot_general(a, b, (((0,), (0,)), ((), ())), precision=precision,
                           preferred_element_type=F32)


def _mm(a, b, precision=None):
    return jnp.dot(a, b, precision=precision, preferred_element_type=F32)


def _sigmoid(x):
    return 1.0 / (1.0 + jnp.exp(-x))


def _rms(x, w):
    ms = jnp.mean(x * x, axis=-1, keepdims=True)
    return x * lax.rsqrt(ms + RMS_EPS) * w


def _ada_body(c_ref, w_ref, b_ref, o_ref):
    x = c_ref[...]
    x = x * _sigmoid(x)
    o_ref[0] = _mm(x, w_ref[0], HI) + b_ref[0]


def _ada(cond, ada_w, ada_b):
    depth, d, n = ada_w.shape
    rows = cond.shape[0]
    tn = 512
    return pl.pallas_call(
        _ada_body,
        grid=(depth, n // tn),
        in_specs=[pl.BlockSpec((rows, d), lambda l, j: (0, 0)),
                  pl.BlockSpec((1, d, tn), lambda l, j: (l, 0, j)),
                  pl.BlockSpec((1, 1, tn), lambda l, j: (l, 0, j))],
        out_specs=pl.BlockSpec((1, rows, tn), lambda l, j: (l, 0, j)),
        out_shape=jax.ShapeDtypeStruct((depth, rows, n), F32),
        compiler_params=_cparams("parallel", "parallel"),
        name="ada_mod",
    )(cond, ada_w, ada_b.reshape(depth, 1, n))


def _in_proj_body(tmod_ref, h_ref, mod_ref, nw_ref, w_ref, o_ref):
    del tmod_ref
    mod = mod_ref[0]
    u = _rms(h_ref[...], nw_ref[...]) * (1.0 + mod[1:2]) + mod[0:1]
    o_ref[...] = _mm(u.astype(BF16), w_ref[...])


def _in_proj(tile_mod, h, mod, norm_w, w):
    n_tok, d = h.shape
    tm = TOKEN_TILE
    grid_spec = pltpu.PrefetchScalarGridSpec(
        num_scalar_prefetch=1,
        grid=(n_tok // tm,),
        in_specs=[pl.BlockSpec((tm, d), lambda i, t: (i, 0)),
                  pl.BlockSpec((1, 6, d), lambda i, t: (t[i], 0, 0)),
                  pl.BlockSpec((1, d), lambda i, t: (0, 0)),
                  pl.BlockSpec((d, PROJ_WIDTH), lambda i, t: (0, 0))],
        out_specs=pl.BlockSpec((tm, PROJ_WIDTH), lambda i, t: (i, 0)))
    return pl.pallas_call(
        _in_proj_body, grid_spec=grid_spec,
        out_shape=jax.ShapeDtypeStruct((n_tok, PROJ_WIDTH), F32),
        compiler_params=_cparams("parallel"),
        name="in_proj",
    )(tile_mod, h, mod, norm_w.reshape(1, d), w)


def _rope_apply(x, cos, sin_signed):
    width = x.shape[-1]
    quarter = HEAD_DIM // 4
    up = pltpu.roll(x, width - quarter, 1)
    down = pltpu.roll(x, quarter, 1)
    lane = lax.broadcasted_iota(jnp.int32, x.shape, 1)
    first = (lane % (2 * quarter)) < quarter
    return x * cos + jnp.where(first, up, down) * sin_signed


def _prep_body(*refs, rope):
    if rope:
        (q_ref, k_ref, v_ref, qnw_ref, knw_ref, bd_ref, cq_ref, sq_ref, ck_ref, sk_ref,
         qo_ref, ko_ref, vo_ref) = refs
    else:
        q_ref, k_ref, v_ref, qnw_ref, knw_ref, bd_ref, qo_ref, ko_ref, vo_ref, kn_ref = refs
    bd = bd_ref[...]
    q = q_ref[...]
    k = k_ref[...]
    qn = q * lax.rsqrt(_mm(q * q, bd, HI) + RMS_EPS) * qnw_ref[...]
    kn = k * lax.rsqrt(_mm(k * k, bd[:KV_WIDTH, :KV_WIDTH], HI) + RMS_EPS) * knw_ref[...]
    if rope:
        qn = _rope_apply(qn, cq_ref[...], sq_ref[...])
        kr = _rope_apply(kn, ck_ref[...], sk_ref[...])
    else:
        kn_ref[...] = kn
        kr = kn
    qo_ref[...] = (qn * (HEAD_DIM ** -0.5)).astype(BF16)
    ko_ref[...] = kr.astype(BF16)
    vo_ref[...] = v_ref[...].astype(BF16)


def _attn_prep(proj, tile_off, n_tok, q_norm_w, k_norm_w, rope_tabs, tiles_per_seq):
    tm = TOKEN_TILE
    rope = rope_tabs is not None
    head_of = np.arange(ATTN_WIDTH) // HEAD_DIM
    bd = jnp.asarray((head_of[:, None] == head_of[None, :]).astype(np.float32) / HEAD_DIM)
    qnw = jnp.tile(q_norm_w, ATTN_HEADS).reshape(1, ATTN_WIDTH)
    knw = jnp.tile(k_norm_w, ATTN_KV_HEADS).reshape(1, KV_WIDTH)
    in_specs = [pl.BlockSpec((tm, ATTN_WIDTH), lambda i: (i + tile_off, COL_Q // ATTN_WIDTH)),
                pl.BlockSpec((tm, KV_WIDTH), lambda i: (i + tile_off, COL_K // KV_WIDTH)),
                pl.BlockSpec((tm, KV_WIDTH), lambda i: (i + tile_off, COL_V // KV_WIDTH)),
                pl.BlockSpec((1, ATTN_WIDTH), lambda i: (0, 0)),
                pl.BlockSpec((1, KV_WIDTH), lambda i: (0, 0)),
                pl.BlockSpec((ATTN_WIDTH, ATTN_WIDTH), lambda i: (0, 0))]
    args = [proj, proj, proj, qnw, knw, bd]
    out_specs = [pl.BlockSpec((tm, ATTN_WIDTH), lambda i: (i, 0)),
                 pl.BlockSpec((tm, KV_WIDTH), lambda i: (i, 0)),
                 pl.BlockSpec((tm, KV_WIDTH), lambda i: (i, 0))]
    out_shape = [jax.ShapeDtypeStruct((n_tok, ATTN_WIDTH), BF16),
                 jax.ShapeDtypeStruct((n_tok, KV_WIDTH), BF16),
                 jax.ShapeDtypeStruct((n_tok, KV_WIDTH), BF16)]
    if rope:
        cq, sq, ck, sk = rope_tabs
        in_specs += [pl.BlockSpec((tm, ATTN_WIDTH), lambda i: (i % tiles_per_seq, 0)),
                     pl.BlockSpec((tm, ATTN_WIDTH), lambda i: (i % tiles_per_seq, 0)),
                     pl.BlockSpec((tm, KV_WIDTH), lambda i: (i % tiles_per_seq, 0)),
                     pl.BlockSpec((tm, KV_WIDTH), lambda i: (i % tiles_per_seq, 0))]
        args += [cq, sq, ck, sk]
    else:
        out_specs.append(pl.BlockSpec((tm, KV_WIDTH), lambda i: (i, 0)))
        out_shape.append(jax.ShapeDtypeStruct((n_tok, KV_WIDTH), F32))
    return pl.pallas_call(
        functools.partial(_prep_body, rope=rope),
        grid=(n_tok // tm,), in_specs=in_specs, out_specs=out_specs, out_shape=out_shape,
        compiler_params=_cparams("parallel"),
        name="attn_prep_rope" if rope else "attn_prep",
    )(*args)


def _rope_tables(seq_len):
    quarter = HEAD_DIM // 4
    half = HEAD_DIM // 2
    t = np.arange(seq_len)
    pos = np.stack([t // GRID_W, t % GRID_W], axis=1).astype(np.float32)
    inv_freq = jnp.power(ROPE_THETA, -jnp.arange(0, half, 2, dtype=F32) / half)
    d = np.arange(HEAD_DIM)
    which = d // half
    freq = inv_freq[d % quarter]
    ang = jnp.asarray(pos)[:, which] * freq[None, :]
    sign = np.where((d % half) < quarter, -1.0, 1.0).astype(np.float32)
    cos, sin = jnp.cos(ang), jnp.sin(ang) * sign[None, :]
    return (jnp.tile(cos, (1, ATTN_HEADS)), jnp.tile(sin, (1, ATTN_HEADS)),
            jnp.tile(cos, (1, ATTN_KV_HEADS)), jnp.tile(sin, (1, ATTN_KV_HEADS)))


def _attn_body(q_ref, k_ref, v_ref, o_ref, m_ref, l_ref, acc_ref, *, nk):
    ki = pl.program_id(2)

    @pl.when(ki == 0)
    def _():
        m_ref[...] = jnp.full(m_ref.shape, -jnp.inf, F32)
        l_ref[...] = jnp.zeros(l_ref.shape, F32)
        acc_ref[...] = jnp.zeros(acc_ref.shape, F32)

    groups = ATTN_HEADS // ATTN_KV_HEADS
    for h in range(ATTN_HEADS):
        kv = h // groups
        q = q_ref[0, :, h * HEAD_DIM:(h + 1) * HEAD_DIM]
        k = k_ref[0, :, kv * HEAD_DIM:(kv + 1) * HEAD_DIM]
        v = v_ref[0, :, kv * HEAD_DIM:(kv + 1) * HEAD_DIM]
        s = _nt(q, k)
        m_prev = m_ref[h]
        m_new = jnp.maximum(m_prev, jnp.max(s, axis=-1, keepdims=True))
        alpha = jnp.exp(m_prev - m_new)
        p = jnp.exp(s - m_new)
        l_ref[h] = alpha * l_ref[h] + jnp.sum(p, axis=-1, keepdims=True)
        acc_ref[h] = alpha * acc_ref[h] + _mm(p.astype(BF16), v)
        m_ref[h] = m_new

    @pl.when(ki == nk - 1)
    def _():
        out = [acc_ref[h] / l_ref[h] for h in range(ATTN_HEADS)]
        o_ref[0] = jnp.concatenate(out, axis=-1).astype(o_ref.dtype)


def _attention(q, k, v, tq, tk):
    b, t, _ = q.shape
    s = k.shape[1]
    nk = s // tk
    return pl.pallas_call(
        functools.partial(_attn_body, nk=nk),
        grid=(b, t // tq, nk),
        in_specs=[pl.BlockSpec((1, tq, ATTN_WIDTH), lambda bi, qi, ki: (bi, qi, 0)),
                  pl.BlockSpec((1, tk, KV_WIDTH), lambda bi, qi, ki: (bi, ki, 0)),
                  pl.BlockSpec((1, tk, KV_WIDTH), lambda bi, qi, ki: (bi, ki, 0))],
        out_specs=pl.BlockSpec((1, tq, ATTN_WIDTH), lambda bi, qi, ki: (bi, qi, 0)),
        out_shape=jax.ShapeDtypeStruct((b, t, ATTN_WIDTH), BF16),
        scratch_shapes=[pltpu.VMEM((ATTN_HEADS, tq, 1), F32),
                        pltpu.VMEM((ATTN_HEADS, tq, 1), F32),
                        pltpu.VMEM((ATTN_HEADS, tq, HEAD_DIM), F32)],
        compiler_params=_cparams("parallel", "parallel", "arbitrary"),
        name="attention",
    )(q, k, v)


def _conv_body(first_ref, last_ref, cur_ref, prev_ref, next_ref, w_ref, b_ref, o_ref):
    i = pl.program_id(0)
    x = cur_ref[...]
    tm = x.shape[0]
    prev = jnp.where(first_ref[i] == 1, 0.0, prev_ref[...])
    nxt = jnp.where(last_ref[i] == 1, 0.0, next_ref[...])
    row8 = lax.broadcasted_iota(jnp.int32, prev.shape, 0)
    w = w_ref[...]
    half = SSD_CONV // 2
    acc = b_ref[...] + w[half:half + 1] * x
    for s in range(1, half + 1):
        rolled = pltpu.roll(x, s, 0)
        top = jnp.where(row8 < s, pltpu.roll(prev, s, 0), rolled[:8])
        acc = acc + w[half - s:half - s + 1] * jnp.concatenate([top, rolled[8:]], axis=0)
        rolled = pltpu.roll(x, tm - s, 0)
        bot = jnp.where(row8 >= 8 - s, pltpu.roll(nxt, 8 - s, 0), rolled[tm - 8:])
        acc = acc + w[half + s:half + s + 1] * jnp.concatenate([rolled[:tm - 8], bot], axis=0)
    o_ref[...] = acc * _sigmoid(acc)


def _ssd_conv(first, last, proj, conv_w, conv_b):
    n_tok = proj.shape[0]
    tm = TOKEN_TILE
    sub = tm // 8
    col = COL_XBC // XBC_WIDTH
    w8 = jnp.concatenate([conv_w, jnp.zeros((8 - SSD_CONV, XBC_WIDTH), F32)], axis=0)
    grid_spec = pltpu.PrefetchScalarGridSpec(
        num_scalar_prefetch=2,
        grid=(n_tok // tm,),
        in_specs=[pl.BlockSpec((tm, XBC_WIDTH), lambda i, f, l: (i, col)),
                  pl.BlockSpec((8, XBC_WIDTH), lambda i, f, l: (jnp.maximum(i * sub - 1, 0), col)),
                  pl.BlockSpec((8, XBC_WIDTH),
                               lambda i, f, l: (jnp.minimum((i + 1) * sub, n_tok // 8 - 1), col)),
                  pl.BlockSpec((8, XBC_WIDTH), lambda i, f, l: (0, 0)),
                  pl.BlockSpec((1, XBC_WIDTH), lambda i, f, l: (0, 0))],
        out_specs=pl.BlockSpec((tm, XBC_WIDTH), lambda i, f, l: (i, 0)))
    return pl.pallas_call(
        _conv_body, grid_spec=grid_spec,
        out_shape=jax.ShapeDtypeStruct((n_tok, XBC_WIDTH), F32),
        compiler_params=_cparams("parallel"),
        name="ssd_conv",
    )(first, last, proj, proj, proj, w8, conv_b.reshape(1, XBC_WIDTH))


def _softplus(x):
    return jnp.maximum(x, 0.0) + jnp.log1p(jnp.exp(-jnp.abs(x)))


def _ssd_body(*refs, nc, has_init):
    if has_init:
        (xs_ref, bm_ref, cm_ref, dt_ref, z_ref, dtb_ref, an_ref, drow_ref, nw_ref, h0_ref,
         y_ref, fin_ref, y_scr, h_scr) = refs
    else:
        (xs_ref, bm_ref, cm_ref, dt_ref, z_ref, dtb_ref, an_ref, drow_ref, nw_ref,
         y_ref, fin_ref, y_scr, h_scr) = refs
    sweep_id = pl.program_id(1)
    c = pl.program_id(2)
    chunk_len = SSD_CHUNK
    rep = SSD_HEADS // SSD_GROUPS

    @pl.when(c == 0)
    def _():
        if has_init:
            h_scr[...] = h0_ref[0, 0]
        else:
            h_scr[...] = jnp.zeros(h_scr.shape, F32)

    def sweep(direction):
        chunk = c if direction == 0 else nc - 1 - c
        row0 = pl.multiple_of(chunk * chunk_len, chunk_len)
        xs = xs_ref[...]
        bm = bm_ref[...]
        cm = cm_ref[...]
        dt_all = _softplus(dt_ref[...] + dtb_ref[...])
        la_all = dt_all * an_ref[...]
        ti = lax.broadcasted_iota(jnp.int32, (chunk_len, chunk_len), 0)
        si = lax.broadcasted_iota(jnp.int32, (chunk_len, chunk_len), 1)
        before_eq = (si <= ti) if direction == 0 else (si >= ti)
        cs_all = _mm(before_eq.astype(F32), la_all, HI)
        cs_t = cs_all.T
        last = chunk_len - 1 if direction == 0 else 0
        heads = []
        for g in range(SSD_GROUPS):
            b_g = bm[:, g * SSD_STATE:(g + 1) * SSD_STATE].astype(BF16)
            c_g = cm[:, g * SSD_STATE:(g + 1) * SSD_STATE].astype(BF16)
            cb = _nt(c_g, b_g)
            for hh in range(rep):
                h = g * rep + hh
                j = direction * SSD_HEADS + h
                cs_c = cs_all[:, j:j + 1]
                cs_r = cs_t[j:j + 1, :]
                decay = jnp.exp(jnp.where(before_eq, cs_c - cs_r, -jnp.inf))
                xd = xs[:, h * HEAD_DIM:(h + 1) * HEAD_DIM] * dt_all[:, j:j + 1]
                state = h_scr[h]
                y = _mm((cb * decay).astype(BF16), xd.astype(BF16))
                y = y + _nt(c_g, state.astype(BF16)) * jnp.exp(cs_c)
                tot = cs_all[last:last + 1, j:j + 1]
                to_end = jnp.exp(tot - cs_c)
                h_scr[h] = jnp.exp(tot) * state + _tn((xd * to_end).astype(BF16), b_g)
                heads.append(y)
        y_dir = jnp.concatenate(heads, axis=-1)
        if direction == 0:
            y_scr[pl.ds(row0, chunk_len), :] = y_dir + drow_ref[...] * xs
        else:
            y = y_scr[pl.ds(row0, chunk_len), :] + y_dir
            z = z_ref[...]
            y = y * (z * _sigmoid(z))
            y_ref[...] = _rms(y, nw_ref[...]).astype(y_ref.dtype)

    @pl.when(sweep_id == 0)
    def _():
        sweep(0)

    @pl.when(sweep_id == 1)
    def _():
        sweep(1)

    @pl.when(c == nc - 1)
    def _():
        fin_ref[0, 0] = h_scr[...]


def _ssd(proj, xbc, tile_off, batch, seq_len, dt_bias, a_log, d_skip, norm_w, init):
    chunk_len = SSD_CHUNK
    nc = seq_len // chunk_len
    base = tile_off * (TOKEN_TILE // chunk_len)
    has_init = init is not None

    def chunk_of(s, c):
        return c + s * (nc - 1 - 2 * c)

    def row(b, s, c):
        return base + b * nc + chunk_of(s, c)

    def z_row(b, s, c):
        return base + b * nc + s * (nc - 1 - c) + (1 - s) * (nc - 1)

    def y_row(b, s, c):
        return b * nc + s * (nc - 1 - c) + (1 - s) * (nc - 1)

    dtb = jnp.zeros((1, LANES), F32).at[0, :2 * SSD_HEADS].set(dt_bias.reshape(-1))
    a_neg = jnp.zeros((1, LANES), F32).at[0, :2 * SSD_HEADS].set(-jnp.exp(a_log.reshape(-1)))
    d_row = jnp.repeat(d_skip, HEAD_DIM).reshape(1, SSD_WIDTH)
    xs_blk = COL_XBC // XBC_WIDTH * (XBC_WIDTH // LANES)
    in_specs = [
        pl.BlockSpec((chunk_len, SSD_WIDTH), lambda b, s, c: (row(b, s, c), 0)),
        pl.BlockSpec((chunk_len, LANES), lambda b, s, c: (row(b, s, c), SSD_WIDTH // LANES)),
        pl.BlockSpec((chunk_len, LANES), lambda b, s, c: (row(b, s, c), SSD_WIDTH // LANES + 1)),
        pl.BlockSpec((chunk_len, LANES), lambda b, s, c: (row(b, s, c), COL_DT // LANES)),
        pl.BlockSpec((chunk_len, SSD_WIDTH), lambda b, s, c: (z_row(b, s, c), COL_Z // SSD_WIDTH)),
        pl.BlockSpec((1, LANES), lambda b, s, c: (0, 0)),
        pl.BlockSpec((1, LANES), lambda b, s, c: (0, 0)),
        pl.BlockSpec((1, SSD_WIDTH), lambda b, s, c: (0, 0)),
        pl.BlockSpec((1, SSD_WIDTH), lambda b, s, c: (0, 0)),
    ]
    del xs_blk
    args = [xbc, xbc, xbc, proj, proj, dtb, a_neg, d_row, norm_w.reshape(1, SSD_WIDTH)]
    state_block = (1, 1, SSD_HEADS, HEAD_DIM, SSD_STATE)
    if has_init:
        in_specs.append(pl.BlockSpec(state_block, lambda b, s, c: (b, s, 0, 0, 0)))
        args.append(init)
    return pl.pallas_call(
        functools.partial(_ssd_body, nc=nc, has_init=has_init),
        grid=(batch, 2, nc),
        in_specs=in_specs,
        out_specs=[pl.BlockSpec((chunk_len, SSD_WIDTH), lambda b, s, c: (y_row(b, s, c), 0)),
                   pl.BlockSpec(state_block, lambda b, s, c: (b, s, 0, 0, 0))],
        out_shape=[jax.ShapeDtypeStruct((batch * seq_len, SSD_WIDTH), BF16),
                   jax.ShapeDtypeStruct((batch, 2, SSD_HEADS, HEAD_DIM, SSD_STATE), F32)],
        scratch_shapes=[pltpu.VMEM((seq_len, SSD_WIDTH), F32),
                        pltpu.VMEM((SSD_HEADS, HEAD_DIM, SSD_STATE), F32)],
        compiler_params=_cparams("parallel", "arbitrary", "arbitrary"),
        name="ssd",
    )(*args)


def _rwkv_body(*refs, nc, has_init):
    if has_init:
        (r_ref, k_ref, v_ref, walo_ref, glo_ref, w0_ref, wup_ref, a0_ref, aup_ref, gup_ref,
         kk_ref, ka_ref, rk_ref, lnw_ref, lnb_ref, s0_ref, y_ref, fin_ref, y_scr, s_scr) = refs
    else:
        (r_ref, k_ref, v_ref, walo_ref, glo_ref, w0_ref, wup_ref, a0_ref, aup_ref, gup_ref,
         kk_ref, ka_ref, rk_ref, lnw_ref, lnb_ref, y_ref, fin_ref, y_scr, s_scr) = refs
    sweep_id = pl.program_id(1)
    c = pl.program_id(2)
    cl = RWKV_CHUNK
    n = HEAD_DIM
    lora = walo_ref.shape[-1] // 2

    @pl.when(c == 0)
    def _():
        if has_init:
            s_scr[...] = s0_ref[0, 0]
        else:
            s_scr[...] = jnp.zeros(s_scr.shape, F32)

    def sweep(direction):
        chunk = c if direction == 0 else nc - 1 - c
        row0 = pl.multiple_of(chunk * cl, cl)
        r = r_ref[...]
        k = k_ref[...]
        v = v_ref[...]
        walo = walo_ref[...]
        zw = w0_ref[direction:direction + 1] + _mm(jnp.tanh(walo[:, :lora]), wup_ref[direction], HI)
        lw = -math.exp(-0.5) * _sigmoid(zw)
        a = _sigmoid(a0_ref[direction:direction + 1] + _mm(walo[:, lora:], aup_ref[direction], HI))
        kd = k * (1.0 + (a - 1.0) * ka_ref[...])
        kk_raw = k * kk_ref[...]
        ti = lax.broadcasted_iota(jnp.int32, (cl, cl), 0)
        si = lax.broadcasted_iota(jnp.int32, (cl, cl), 1)
        before_eq = (si <= ti) if direction == 0 else (si >= ti)
        before = (si < ti) if direction == 0 else (si > ti)
        lp_all = _mm(before_eq.astype(F32), lw, HI)
        last = cl - 1 if direction == 0 else 0
        outs = []
        for h in range(RWKV_HEADS):
            sl = slice(h * n, (h + 1) * n)
            kk = kk_raw[:, sl]
            kk = kk * lax.rsqrt(jnp.sum(kk * kk, axis=-1, keepdims=True) + 1e-12)
            lp = lp_all[:, sl]
            lw_h = lw[:, sl]
            dec_in = jnp.exp(lp)
            dec_out = jnp.exp(-lp)
            kap = kk * jnp.exp(lp - lw_h)
            rt = r[:, sl] * dec_in
            kt = kd[:, sl] * dec_out
            bt = a[:, sl] * kk * dec_out
            v_h = v[:, sl]
            state = s_scr[h]
            g = _nt(jnp.concatenate([kap, rt], axis=0), jnp.concatenate([kt, bt], axis=0), HI)
            n_mat = jnp.where(before, g[:cl, :cl], 0.0)
            l_mat = jnp.where(before, g[:cl, cl:], 0.0)
            rk_mat = jnp.where(before_eq, g[cl:, :cl], 0.0)
            rb_mat = jnp.where(before_eq, g[cl:, cl:], 0.0)
            from_state = _nt(jnp.concatenate([kap, rt], axis=0), state, HI)
            x = from_state[:cl] + _mm(n_mat, v_h, HI)
            x = x - _mm(l_mat, x, HI)
            pw = l_mat
            for _ in range(int(math.log2(cl)) - 1):
                pw = _mm(pw, pw, HI)
                x = x + _mm(pw, x, HI)
            u = x
            y_h = from_state[cl:] + _mm(rk_mat, v_h, HI) - _mm(rb_mat, u, HI)
            lp_last = lp[last:last + 1]
            to_end = jnp.exp(lp_last - lp)
            upd = _tn(jnp.concatenate([v_h, -u], axis=0),
                      jnp.concatenate([kd[:, sl] * to_end, a[:, sl] * kk * to_end], axis=0), HI)
            s_scr[h] = state * jnp.exp(lp_last) + upd
            outs.append(y_h)
        y_dir = jnp.concatenate(outs, axis=-1)
        if direction == 0:
            y_scr[pl.ds(row0, cl), :] = y_dir
        else:
            y = y_scr[pl.ds(row0, cl), :] + y_dir
            gate = _mm(_sigmoid(glo_ref[...]), gup_ref[...], HI)
            rkw = r * k * rk_ref[...]
            normed = []
            for h in range(RWKV_HEADS):
                sl = slice(h * n, (h + 1) * n)
                y_h = y[:, sl]
                mu = jnp.mean(y_h, axis=-1, keepdims=True)
                var = jnp.mean(jnp.square(y_h - mu), axis=-1, keepdims=True)
                bonus = jnp.sum(rkw[:, sl], axis=-1, keepdims=True) * v[:, sl]
                normed.append(((y_h - mu) * lax.rsqrt(var + GN_EPS), bonus))
            y_n = jnp.concatenate([t[0] for t in normed], axis=-1)
            bonus = jnp.concatenate([t[1] for t in normed], axis=-1)
            out = (y_n * lnw_ref[...] + lnb_ref[...] + bonus) * gate
            y_ref[...] = out.astype(y_ref.dtype)

    @pl.when(sweep_id == 0)
    def _():
        sweep(0)

    @pl.when(sweep_id == 1)
    def _():
        sweep(1)

    @pl.when(c == nc - 1)
    def _():
        fin_ref[0, 0] = s_scr[...]


def _rwkv(proj, tile_off, batch, seq_len, lp, init):
    cl = RWKV_CHUNK
    nc = seq_len // cl
    base = tile_off * (TOKEN_TILE // cl)
    has_init = init is not None
    w = RWKV_WIDTH

    def row(b, s, c):
        return base + b * nc + c + s * (nc - 1 - 2 * c)

    def tail_row(b, s, c):
        return base + b * nc + s * (nc - 1 - c) + (1 - s) * (nc - 1)

    def y_row(b, s, c):
        return b * nc + s * (nc - 1 - c) + (1 - s) * (nc - 1)

    def const(shape):
        return pl.BlockSpec(shape, lambda b, s, c: (0,) * len(shape))

    in_specs = [
        pl.BlockSpec((cl, w), lambda b, s, c: (row(b, s, c), COL_R // w)),
        pl.BlockSpec((cl, w), lambda b, s, c: (row(b, s, c), COL_KR // w)),
        pl.BlockSpec((cl, w), lambda b, s, c: (row(b, s, c), COL_VR // w)),
        pl.BlockSpec((cl, LANES), lambda b, s, c: (row(b, s, c), COL_WALO // LANES)),
        pl.BlockSpec((cl, LANES), lambda b, s, c: (tail_row(b, s, c), COL_GLO // LANES)),
        const((2, w)), const((2, LANES // 2, w)), const((2, w)), const((2, LANES // 2, w)),
        const((LANES, w)), const((1, w)), const((1, w)), const((1, w)), const((1, w)), const((1, w)),
    ]
    args = [proj, proj, proj, proj, proj, lp['rwkv_w0'], lp['rwkv_w_up'], lp['rwkv_a0'],
            lp['rwkv_a_up'], lp['rwkv_g_up'], lp['rwkv_k_k'].reshape(1, w),
            lp['rwkv_k_a'].reshape(1, w), lp['rwkv_r_k'].reshape(1, w),
            lp['rwkv_ln_w'].reshape(1, w), lp['rwkv_ln_b'].reshape(1, w)]
    state_block = (1, 1, RWKV_HEADS, HEAD_DIM, HEAD_DIM)
    if has_init:
        in_specs.append(pl.BlockSpec(state_block, lambda b, s, c: (b, s, 0, 0, 0)))
        args.append(init)
    return pl.pallas_call(
        functools.partial(_rwkv_body, nc=nc, has_init=has_init),
        grid=(batch, 2, nc),
        in_specs=in_specs,
        out_specs=[pl.BlockSpec((cl, w), lambda b, s, c: (y_row(b, s, c), 0)),
                   pl.BlockSpec(state_block, lambda b, s, c: (b, s, 0, 0, 0))],
        out_shape=[jax.ShapeDtypeStruct((batch * seq_len, w), BF16),
                   jax.ShapeDtypeStruct((batch, 2, RWKV_HEADS, HEAD_DIM, HEAD_DIM), F32)],
        scratch_shapes=[pltpu.VMEM((seq_len, w), F32),
                        pltpu.VMEM((RWKV_HEADS, HEAD_DIM, HEAD_DIM), F32)],
        compiler_params=_cparams("parallel", "arbitrary", "arbitrary"),
        name="rwkv",
    )(*args)


def _out_proj_body(tmod_ref, attn_ref, ssd_ref, rwkv_ref, h_ref, mod_ref, nw_ref, w_ref, rw_ref,
                   rb_ref, h_out, u_out, lg_out):
    del tmod_ref
    mixed = (_mm(attn_ref[...], w_ref[:ATTN_WIDTH])
             + _mm(ssd_ref[...], w_ref[ATTN_WIDTH:ATTN_WIDTH + SSD_WIDTH])
             + _mm(rwkv_ref[...], w_ref[ATTN_WIDTH + SSD_WIDTH:]))
    mod = mod_ref[0]
    h = h_ref[...] + mod[2:3] * mixed
    h_out[...] = h
    u = _rms(h, nw_ref[...]) * (1.0 + mod[4:5]) + mod[3:4]
    u_out[...] = u
    lg_out[...] = _mm(u, rw_ref[...], HI) + rb_ref[...]


def _out_proj(tile_mod, attn, ssd, rwkv, h, mod, norm_w, w, router_w, router_b):
    n_tok, d = h.shape
    tm = TOKEN_TILE
    rw = jnp.zeros((d, LANES), F32).at[:, :N_EXPERTS].set(router_w)
    rb = jnp.zeros((1, LANES), F32).at[0, :N_EXPERTS].set(router_b)
    grid_spec = pltpu.PrefetchScalarGridSpec(
        num_scalar_prefetch=1,
        grid=(n_tok // tm,),
        in_specs=[pl.BlockSpec((tm, ATTN_WIDTH), lambda i, t: (i, 0)),
                  pl.BlockSpec((tm, SSD_WIDTH), lambda i, t: (i, 0)),
                  pl.BlockSpec((tm, RWKV_WIDTH), lambda i, t: (i, 0)),
                  pl.BlockSpec((tm, d), lambda i, t: (i, 0)),
                  pl.BlockSpec((1, 6, d), lambda i, t: (t[i], 0, 0)),
                  pl.BlockSpec((1, d), lambda i, t: (0, 0)),
                  pl.BlockSpec((d, d), lambda i, t: (0, 0)),
                  pl.BlockSpec((d, LANES), lambda i, t: (0, 0)),
                  pl.BlockSpec((1, LANES), lambda i, t: (0, 0))],
        out_specs=[pl.BlockSpec((tm, d), lambda i, t: (i, 0)),
                   pl.BlockSpec((tm, d), lambda i, t: (i, 0)),
                   pl.BlockSpec((tm, LANES), lambda i, t: (i, 0))])
    return pl.pallas_call(
        _out_proj_body, grid_spec=grid_spec,
        out_shape=[jax.ShapeDtypeStruct((n_tok, d), F32),
                   jax.ShapeDtypeStruct((n_tok, d), F32),
                   jax.ShapeDtypeStruct((n_tok, LANES), F32)],
        compiler_params=_cparams("parallel"),
        name="out_proj",
    )(tile_mod, attn, ssd, rwkv, h, mod, norm_w.reshape(1, d), w, rw, rb)


def _gather_body(idx_ref, x_hbm, o_hbm, idx_smem, isem, sem):
    i = pl.program_id(0)
    rows = idx_smem.shape[0]
    cp = pltpu.make_async_copy(idx_ref.at[0, 0], idx_smem, isem)
    cp.start()
    cp.wait()
    base = i * rows

    def issue(r, carry):
        pltpu.make_async_copy(x_hbm.at[pl.ds(idx_smem[r], 1)], o_hbm.at[pl.ds(base + r, 1)],
                              sem).start()
        return carry

    lax.fori_loop(0, rows, issue, 0)
    pltpu.make_async_copy(x_hbm.at[pl.ds(0, rows)], o_hbm.at[pl.ds(base, rows)], sem).wait()


def _gather_rows(x, idx):
    n_rows = idx.shape[0]
    rows = GATHER_ROWS
    d = x.shape[1]
    steps = n_rows // rows
    return pl.pallas_call(
        _gather_body,
        grid=(steps,),
        in_specs=[pl.BlockSpec((1, 1, rows), lambda i: (i, 0, 0)),
                  pl.BlockSpec(memory_space=pl.ANY)],
        out_specs=pl.BlockSpec(memory_space=pl.ANY),
        out_shape=jax.ShapeDtypeStruct((n_rows, d), x.dtype),
        scratch_shapes=[pltpu.SMEM((rows,), jnp.int32), pltpu.SemaphoreType.DMA,
                        pltpu.SemaphoreType.DMA],
        compiler_params=_cparams("arbitrary"),
        name="moe_dispatch",
    )(idx.reshape(steps, 1, rows), x)


def _moe_body(be_ref, nv_ref, x_ref, wgu_ref, bgu_ref, wdn_ref, bdn_ref, o_ref):
    del be_ref
    i = pl.program_id(0)

    @pl.when(i < nv_ref[0])
    def _():
        gu = _mm(x_ref[...].astype(BF16), wgu_ref[0]) + bgu_ref[0]
        gate = jnp.minimum(gu[:, :D_EXPERT], SWIGLU_LIMIT)
        up = jnp.clip(gu[:, D_EXPERT:], -SWIGLU_LIMIT, SWIGLU_LIMIT)
        act = (up + 1.0) * gate * _sigmoid(SWIGLU_ALPHA * gate)
        o_ref[...] = _mm(act.astype(BF16), wdn_ref[0]) + bdn_ref[0]

    @pl.when(i >= nv_ref[0])
    def _():
        o_ref[...] = jnp.zeros(o_ref.shape, o_ref.dtype)


def _moe_experts(block_expert, n_valid, x_rows, w_gu, b_gu, w_dn, b_dn):
    n_rows, d = x_rows.shape
    rows = MOE_ROWS
    grid_spec = pltpu.PrefetchScalarGridSpec(
        num_scalar_prefetch=2,
        grid=(n_rows // rows,),
        in_specs=[pl.BlockSpec((rows, d), lambda i, be, nv: (i, 0)),
                  pl.BlockSpec((1, d, 2 * D_EXPERT), lambda i, be, nv: (be[i], 0, 0)),
                  pl.BlockSpec((1, 1, 2 * D_EXPERT), lambda i, be, nv: (be[i], 0, 0)),
                  pl.BlockSpec((1, D_EXPERT, d), lambda i, be, nv: (be[i], 0, 0)),
                  pl.BlockSpec((1, 1, d), lambda i, be, nv: (be[i], 0, 0))],
        out_specs=pl.BlockSpec((rows, d), lambda i, be, nv: (i, 0)))
    return pl.pallas_call(
        _moe_body, grid_spec=grid_spec,
        out_shape=jax.ShapeDtypeStruct((n_rows, d), F32),
        compiler_params=_cparams("arbitrary"),
        name="moe_experts",
    )(block_expert, n_valid, x_rows, w_gu, b_gu.reshape(N_EXPERTS, 1, -1), w_dn,
      b_dn.reshape(N_EXPERTS, 1, -1))


def _combine_body(tmod_ref, pos_ref, gates_ref, h_ref, mod_ref, fw_ref, y_hbm, o_ref, pos_smem,
                  buf, isem, sem, *, final):
    del tmod_ref
    tm = h_ref.shape[0]
    cp = pltpu.make_async_copy(pos_ref.at[0, 0], pos_smem, isem)
    cp.start()
    cp.wait()

    def issue(r, carry):
        pltpu.make_async_copy(y_hbm.at[pl.ds(pos_smem[r], 1)], buf.at[pl.ds(r, 1)], sem).start()
        return carry

    lax.fori_loop(0, TOP_K * tm, issue, 0)
    pltpu.make_async_copy(y_hbm.at[pl.ds(0, TOP_K * tm)], buf, sem).wait()
    gates = gates_ref[...]
    y = gates[:, 0:1] * buf[0:tm]
    for kk in range(1, TOP_K):
        y = y + gates[:, kk:kk + 1] * buf[kk * tm:(kk + 1) * tm]
    h = h_ref[...] + mod_ref[0][5:6] * y
    if final:
        o_ref[...] = _rms(h, fw_ref[...])
    else:
        o_ref[...] = h


def _moe_combine(tile_mod, pos, gates, h, mod, final_w, y_rows, final):
    n_tok, d = h.shape
    tm = COMBINE_TILE
    steps = n_tok // tm
    ratio = TOKEN_TILE // tm
    pos_t = pos.reshape(steps, tm, TOP_K).transpose(0, 2, 1).reshape(steps, 1, TOP_K * tm)
    grid_spec = pltpu.PrefetchScalarGridSpec(
        num_scalar_prefetch=1,
        grid=(steps,),
        in_specs=[pl.BlockSpec((1, 1, TOP_K * tm), lambda i, t: (i, 0, 0)),
                  pl.BlockSpec((tm, TOP_K), lambda i, t: (i, 0)),
                  pl.BlockSpec((tm, d), lambda i, t: (i, 0)),
                  pl.BlockSpec((1, 6, d), lambda i, t: (t[i // ratio], 0, 0)),
                  pl.BlockSpec((1, d), lambda i, t: (0, 0)),
                  pl.BlockSpec(memory_space=pl.ANY)],
        out_specs=pl.BlockSpec((tm, d), lambda i, t: (i, 0)),
        scratch_shapes=[pltpu.SMEM((TOP_K * tm,), jnp.int32),
                        pltpu.VMEM((TOP_K * tm, d), F32),
                        pltpu.SemaphoreType.DMA, pltpu.SemaphoreType.DMA])
    return pl.pallas_call(
        functools.partial(_combine_body, final=final), grid_spec=grid_spec,
        out_shape=jax.ShapeDtypeStruct((n_tok, d), F32),
        compiler_params=_cparams("arbitrary"),
        name="moe_combine",
    )(tile_mod, pos_t, gates, h, mod, final_w.reshape(1, d), y_rows)


def _route(logits):
    n_tok = logits.shape[0]
    n_assign = n_tok * TOP_K
    rows = MOE_ROWS
    n_blocks = n_assign // rows + N_EXPERTS
    top_logit, top_idx = lax.top_k(logits, TOP_K)
    gates = jax.nn.softmax(top_logit, axis=-1)
    e_flat = top_idx.reshape(-1)
    tok_flat = jnp.repeat(jnp.arange(n_tok, dtype=jnp.int32), TOP_K)
    order = jnp.argsort(e_flat)
    e_sorted = e_flat[order]
    tok_sorted = tok_flat[order]
    counts = jnp.zeros((N_EXPERTS,), jnp.int32).at[e_flat].add(1)
    padded = (counts + rows - 1) // rows * rows
    start = jnp.cumsum(counts) - counts
    padded_end = jnp.cumsum(padded)
    padded_start = padded_end - padded
    dest = padded_start[e_sorted] + jnp.arange(n_assign, dtype=jnp.int32) - start[e_sorted]
    row_tok = jnp.zeros((n_blocks * rows,), jnp.int32).at[dest].set(tok_sorted)
    pos = jnp.zeros((n_assign,), jnp.int32).at[order].set(dest).reshape(n_tok, TOP_K)
    block_start = jnp.arange(n_blocks, dtype=jnp.int32) * rows
    block_expert = jnp.minimum(jnp.searchsorted(padded_end, block_start, side='right'),
                               N_EXPERTS - 1).astype(jnp.int32)
    n_valid = (padded_end[-1] // rows).astype(jnp.int32).reshape(1)
    return row_tok, pos, gates, block_expert, n_valid


def _permute_w_in(w_in):
    def cols(a, b):
        return w_in[:, a:b]
    q, k, v, z = cols(0, 384), cols(384, 512), cols(512, 640), cols(640, 1024)
    xbc, dt = cols(1024, 1664), cols(1664, 1676)
    r, kr, vr = cols(1676, 1932), cols(1932, 2188), cols(2188, 2444)
    walo, glo = cols(2444, 2572), cols(2572, 2700)
    dt = jnp.pad(dt, ((0, 0), (0, LANES - dt.shape[1])))
    return jnp.concatenate([q, z, r, kr, k, v, dt, walo, glo, xbc, vr], axis=1).astype(BF16)


def kernel(x_prompt, x_sample, cache_attn_k, cache_attn_v, state_ssd, state_rwkv, c, c_ctx,
           norm1_w, norm2_w, ada_w, ada_b, w_in, q_norm_w, k_norm_w,
           ssd_conv_w, ssd_conv_b, ssd_dt_bias, ssd_a_log, ssd_d, ssd_norm_w,
           rwkv_w0, rwkv_w_up, rwkv_a0, rwkv_a_up, rwkv_g_up, rwkv_k_k, rwkv_k_a, rwkv_r_k,
           rwkv_ln_w, rwkv_ln_b, w_out, router_w, router_b,
           moe_w_gate_up, moe_b_gate_up, moe_w_down, moe_b_down, final_norm_w):
    batch, seq, d = x_prompt.shape
    dec_batch, dec_seq, _ = x_sample.shape
    depth = norm1_w.shape[0]
    past = cache_attn_k.shape[2]
    n_ctx, n_lat = batch * seq, dec_batch * dec_seq
    tm = TOKEN_TILE
    ctx_tiles, lat_tiles = n_ctx // tm, n_lat // tm

    tile_mod = np.concatenate([np.zeros(ctx_tiles, np.int32),
                               1 + np.arange(lat_tiles, dtype=np.int32) // (dec_seq // tm)])
    seq_tiles = np.concatenate([np.full(ctx_tiles, seq // tm), np.full(lat_tiles, dec_seq // tm)])
    tile_in_seq = np.concatenate([np.arange(ctx_tiles) % (seq // tm),
                                  np.arange(lat_tiles) % (dec_seq // tm)])
    first = jnp.asarray((tile_in_seq == 0).astype(np.int32))
    last = jnp.asarray((tile_in_seq == seq_tiles - 1).astype(np.int32))
    tile_mod = jnp.asarray(tile_mod)

    cond = jnp.concatenate([c_ctx[None, :], c], axis=0)
    pad_rows = -cond.shape[0] % 8
    cond = jnp.pad(cond, ((0, pad_rows), (0, 0)))
    mod_all = _ada(cond, ada_w, ada_b).reshape(depth, cond.shape[0], 6, d)

    rope_tabs = _rope_tables(dec_seq)
    h = jnp.concatenate([x_prompt.reshape(n_ctx, d), x_sample.reshape(n_lat, d)], axis=0)
    ks, vs, ssd_states, rwkv_states = [], [], [], []
    for l in range(depth):
        lp = dict(rwkv_w0=rwkv_w0[l], rwkv_w_up=rwkv_w_up[l], rwkv_a0=rwkv_a0[l],
                  rwkv_a_up=rwkv_a_up[l], rwkv_g_up=rwkv_g_up[l], rwkv_k_k=rwkv_k_k[l],
                  rwkv_k_a=rwkv_k_a[l], rwkv_r_k=rwkv_r_k[l], rwkv_ln_w=rwkv_ln_w[l],
                  rwkv_ln_b=rwkv_ln_b[l])
        mod = mod_all[l]
        proj = _in_proj(tile_mod, h, mod, norm1_w[l], _permute_w_in(w_in[l]))

        q_c, k_c, v_c, k_norm = _attn_prep(proj, 0, n_ctx, q_norm_w[l], k_norm_w[l], None, 1)
        q_l, k_l, v_l = _attn_prep(proj, ctx_tiles, n_lat, q_norm_w[l], k_norm_w[l], rope_tabs,
                                   dec_seq // tm)
        attn_c = _attention(q_c.reshape(batch, seq, -1), k_c.reshape(batch, seq, -1),
                            v_c.reshape(batch, seq, -1), seq, seq)
        k_full = jnp.concatenate([cache_attn_k[:, l].reshape(dec_batch, past, -1).astype(BF16),
                                  k_l.reshape(dec_batch, dec_seq, -1)], axis=1)
        v_full = jnp.concatenate([cache_attn_v[:, l].reshape(dec_batch, past, -1).astype(BF16),
                                  v_l.reshape(dec_batch, dec_seq, -1)], axis=1)
        attn_l = _attention(q_l.reshape(dec_batch, dec_seq, -1), k_full, v_full, 256, 512)
        attn = jnp.concatenate([attn_c.reshape(n_ctx, -1), attn_l.reshape(n_lat, -1)], axis=0)
        ks.append(k_norm.reshape(batch, seq, ATTN_KV_HEADS, HEAD_DIM))
        vs.append(proj[:n_ctx, COL_V:COL_V + KV_WIDTH].reshape(batch, seq, ATTN_KV_HEADS, HEAD_DIM))

        xbc = _ssd_conv(first, last, proj, ssd_conv_w[l], ssd_conv_b[l])
        ssd_c, ssd_fin = _ssd(proj, xbc, 0, batch, seq, ssd_dt_bias[l], ssd_a_log[l], ssd_d[l],
                              ssd_norm_w[l], None)
        ssd_l, _ = _ssd(proj, xbc, ctx_tiles, dec_batch, dec_seq, ssd_dt_bias[l], ssd_a_log[l],
                        ssd_d[l], ssd_norm_w[l], state_ssd[:, l])
        ssd = jnp.concatenate([ssd_c, ssd_l], axis=0)
        ssd_states.append(ssd_fin)

        rwkv_c, rwkv_fin = _rwkv(proj, 0, batch, seq, lp, None)
        rwkv_l, _ = _rwkv(proj, ctx_tiles, dec_batch, dec_seq, lp, state_rwkv[:, l])
        rwkv = jnp.concatenate([rwkv_c, rwkv_l], axis=0)
        rwkv_states.append(rwkv_fin)

        h, u, logits = _out_proj(tile_mod, attn, ssd, rwkv, h, mod, norm2_w[l],
                                 w_out[l].astype(BF16), router_w[l], router_b[l])

        row_tok, pos, gates, block_expert, n_valid = _route(logits[:, :N_EXPERTS])
        x_rows = _gather_rows(u, row_tok)
        y_rows = _moe_experts(block_expert, n_valid, x_rows, moe_w_gate_up[l].astype(BF16),
                              moe_b_gate_up[l], moe_w_down[l].astype(BF16), moe_b_down[l])
        h = _moe_combine(tile_mod, pos, gates, h, mod, final_norm_w, y_rows, l == depth - 1)

    y_prompt = h[:n_ctx].reshape(batch, seq, d)
    y_sample = h[n_ctx:].reshape(dec_batch, dec_seq, d)
    return (y_prompt, y_sample, jnp.stack(ks, axis=1), jnp.stack(vs, axis=1),
            jnp.stack(ssd_states, axis=1), jnp.stack(rwkv_states, axis=1))
```

```python
import functools
import math

import numpy as np
import jax
import jax.numpy as jnp
from jax import lax
from jax.experimental import pallas as pl
from jax.experimental.pallas import tpu as pltpu

F32 = jnp.float32
BF16 = jnp.bfloat16
HI = lax.Precision.HIGHEST

D_MODEL = 1024
GRID_W = 64
HEAD_DIM = 64
ATTN_HEADS = 6
ATTN_KV_HEADS = 2
ROPE_THETA = 10000.0
SSD_HEADS = 6
SSD_GROUPS = 2
SSD_STATE = 64
SSD_CONV = 5
SSD_CHUNK = 128
RWKV_HEADS = 4
RWKV_CHUNK = 64
N_EXPERTS = 32
TOP_K = 4
D_EXPERT = 1024
SWIGLU_LIMIT = 7.0
SWIGLU_ALPHA = 1.702
RMS_EPS = 1e-6
GN_EPS = 64e-5

ATTN_WIDTH = ATTN_HEADS * HEAD_DIM
KV_WIDTH = ATTN_KV_HEADS * HEAD_DIM
SSD_WIDTH = SSD_HEADS * HEAD_DIM
XBC_WIDTH = SSD_WIDTH + 2 * SSD_GROUPS * SSD_STATE
RWKV_WIDTH = RWKV_HEADS * HEAD_DIM
LANES = 128

COL_Q, COL_Z, COL_R, COL_KR = 0, 384, 768, 1024
COL_K, COL_V, COL_DT, COL_WALO, COL_GLO = 1280, 1408, 1536, 1664, 1792
COL_XBC, COL_VR = 1920, 2560
PROJ_WIDTH = 2816

TOKEN_TILE = 256
MOE_ROWS = 512
GATHER_ROWS = 512
COMBINE_TILE = 128
VMEM_LIMIT = 56 * 1024 * 1024


def _cparams(*sem):
    return pltpu.CompilerParams(dimension_semantics=sem, vmem_limit_bytes=VMEM_LIMIT)


def _nt(a, b, precision=None):
    return lax.dot_general(a, b, (((1,), (1,)), ((), ())), precision=precision,
                           preferred_element_type=F32)


def _tn(a, b, precision=None):
    return lax.dot_general(a, b, (((0,), (0,)), ((), ())), precision=precision,
                           preferred_element_type=F32)


def _mm(a, b, precision=None):
    return jnp.dot(a, b, precision=precision, preferred_element_type=F32)


def _sigmoid(x):
    return 1.0 / (1.0 + jnp.exp(-x))


def _rms(x, w):
    ms = jnp.mean(x * x, axis=-1, keepdims=True)
    return x * lax.rsqrt(ms + RMS_EPS) * w


def _ada_body(c_ref, w_ref, b_ref, o_ref):
    x = c_ref[...]
    x = x * _sigmoid(x)
    o_ref[0] = _mm(x, w_ref[0], HI) + b_ref[0]


def _ada(cond, ada_w, ada_b):
    depth, d, n = ada_w.shape
    rows = cond.shape[0]
    tn = 512
    return pl.pallas_call(
        _ada_body,
        grid=(depth, n // tn),
        in_specs=[pl.BlockSpec((rows, d), lambda l, j: (0, 0)),
                  pl.BlockSpec((1, d, tn), lambda l, j: (l, 0, j)),
                  pl.BlockSpec((1, 1, tn), lambda l, j: (l, 0, j))],
        out_specs=pl.BlockSpec((1, rows, tn), lambda l, j: (l, 0, j)),
        out_shape=jax.ShapeDtypeStruct((depth, rows, n), F32),
        compiler_params=_cparams("parallel", "parallel"),
        name="ada_mod",
    )(cond, ada_w, ada_b.reshape(depth, 1, n))


def _in_proj_body(tmod_ref, h_ref, mod_ref, nw_ref, w_ref, o_ref):
    del tmod_ref
    mod = mod_ref[0]
    u = _rms(h_ref[...], nw_ref[...]) * (1.0 + mod[1:2]) + mod[0:1]
    o_ref[...] = _mm(u.astype(BF16), w_ref[...])


def _in_proj(tile_mod, h, mod, norm_w, w):
    n_tok, d = h.shape
    tm = TOKEN_TILE
    grid_spec = pltpu.PrefetchScalarGridSpec(
        num_scalar_prefetch=1,
        grid=(n_tok // tm,),
        in_specs=[pl.BlockSpec((tm, d), lambda i, t: (i, 0)),
                  pl.BlockSpec((1, 6, d), lambda i, t: (t[i], 0, 0)),
                  pl.BlockSpec((1, d), lambda i, t: (0, 0)),
                  pl.BlockSpec((d, PROJ_WIDTH), lambda i, t: (0, 0))],
        out_specs=pl.BlockSpec((tm, PROJ_WIDTH), lambda i, t: (i, 0)))
    return pl.pallas_call(
        _in_proj_body, grid_spec=grid_spec,
        out_shape=jax.ShapeDtypeStruct((n_tok, PROJ_WIDTH), F32),
        compiler_params=_cparams("parallel"),
        name="in_proj",
    )(tile_mod, h, mod, norm_w.reshape(1, d), w)


def _rope_apply(x, cos, sin_signed):
    width = x.shape[-1]
    quarter = HEAD_DIM // 4
    up = pltpu.roll(x, width - quarter, 1)
    down = pltpu.roll(x, quarter, 1)
    lane = lax.broadcasted_iota(jnp.int32, x.shape, 1)
    first = (lane % (2 * quarter)) < quarter
    return x * cos + jnp.where(first, up, down) * sin_signed


def _prep_body(*refs, rope):
    if rope:
        (q_ref, k_ref, v_ref, qnw_ref, knw_ref, bd_ref, cq_ref, sq_ref, ck_ref, sk_ref,
         qo_ref, ko_ref, vo_ref) = refs
    else:
        q_ref, k_ref, v_ref, qnw_ref, knw_ref, bd_ref, qo_ref, ko_ref, vo_ref, kn_ref = refs
    bd = bd_ref[...]
    q = q_ref[...]
    k = k_ref[...]
    qn = q * lax.rsqrt(_mm(q * q, bd, HI) + RMS_EPS) * qnw_ref[...]
    kn = k * lax.rsqrt(_mm(k * k, bd[:KV_WIDTH, :KV_WIDTH], HI) + RMS_EPS) * knw_ref[...]
    if rope:
        qn = _rope_apply(qn, cq_ref[...], sq_ref[...])
        kr = _rope_apply(kn, ck_ref[...], sk_ref[...])
    else:
        kn_ref[...] = kn
        kr = kn
    qo_ref[...] = (qn * (HEAD_DIM ** -0.5)).astype(BF16)
    ko_ref[...] = kr.astype(BF16)
    vo_ref[...] = v_ref[...].astype(BF16)


def _attn_prep(proj, tile_off, n_tok, q_norm_w, k_norm_w, rope_tabs, tiles_per_seq):
    tm = TOKEN_TILE
    rope = rope_tabs is not None
    head_of = np.arange(ATTN_WIDTH) // HEAD_DIM
    bd = jnp.asarray((head_of[:, None] == head_of[None, :]).astype(np.float32) / HEAD_DIM)
    qnw = jnp.tile(q_norm_w, ATTN_HEADS).reshape(1, ATTN_WIDTH)
    knw = jnp.tile(k_norm_w, ATTN_KV_HEADS).reshape(1, KV_WIDTH)
    in_specs = [pl.BlockSpec((tm, ATTN_WIDTH), lambda i: (i + tile_off, COL_Q // ATTN_WIDTH)),
                pl.BlockSpec((tm, KV_WIDTH), lambda i: (i + tile_off, COL_K // KV_WIDTH)),
                pl.BlockSpec((tm, KV_WIDTH), lambda i: (i + tile_off, COL_V // KV_WIDTH)),
                pl.BlockSpec((1, ATTN_WIDTH), lambda i: (0, 0)),
                pl.BlockSpec((1, KV_WIDTH), lambda i: (0, 0)),
                pl.BlockSpec((ATTN_WIDTH, ATTN_WIDTH), lambda i: (0, 0))]
    args = [proj, proj, proj, qnw, knw, bd]
    out_specs = [pl.BlockSpec((tm, ATTN_WIDTH), lambda i: (i, 0)),
                 pl.BlockSpec((tm, KV_WIDTH), lambda i: (i, 0)),
                 pl.BlockSpec((tm, KV_WIDTH), lambda i: (i, 0))]
    out_shape = [jax.ShapeDtypeStruct((n_tok, ATTN_WIDTH), BF16),
                 jax.ShapeDtypeStruct((n_tok, KV_WIDTH), BF16),
                 jax.ShapeDtypeStruct((n_tok, KV_WIDTH), BF16)]
    if rope:
        cq, sq, ck, sk = rope_tabs
        in_specs += [pl.BlockSpec((tm, ATTN_WIDTH), lambda i: (i % tiles_per_seq, 0)),
                     pl.BlockSpec((tm, ATTN_WIDTH), lambda i: (i % tiles_per_seq, 0)),
                     pl.BlockSpec((tm, KV_WIDTH), lambda i: (i % tiles_per_seq, 0)),
                     pl.BlockSpec((tm, KV_WIDTH), lambda i: (i % tiles_per_seq, 0))]
        args += [cq, sq, ck, sk]
    else:
        out_specs.append(pl.BlockSpec((tm, KV_WIDTH), lambda i: (i, 0)))
        out_shape.append(jax.ShapeDtypeStruct((n_tok, KV_WIDTH), F32))
    return pl.pallas_call(
        functools.partial(_prep_body, rope=rope),
        grid=(n_tok // tm,), in_specs=in_specs, out_specs=out_specs, out_shape=out_shape,
        compiler_params=_cparams("parallel"),
        name="attn_prep_rope" if rope else "attn_prep",
    )(*args)


def _rope_tables(seq_len):
    quarter = HEAD_DIM // 4
    half = HEAD_DIM // 2
    t = np.arange(seq_len)
    pos = np.stack([t // GRID_W, t % GRID_W], axis=1).astype(np.float32)
    inv_freq = jnp.power(ROPE_THETA, -jnp.arange(0, half, 2, dtype=F32) / half)
    d = np.arange(HEAD_DIM)
    which = d // half
    freq = inv_freq[d % quarter]
    ang = jnp.asarray(pos)[:, which] * freq[None, :]
    sign = np.where((d % half) < quarter, -1.0, 1.0).astype(np.float32)
    cos, sin = jnp.cos(ang), jnp.sin(ang) * sign[None, :]
    return (jnp.tile(cos, (1, ATTN_HEADS)), jnp.tile(sin, (1, ATTN_HEADS)),
            jnp.tile(cos, (1, ATTN_KV_HEADS)), jnp.tile(sin, (1, ATTN_KV_HEADS)))


def _attn_body(q_ref, k_ref, v_ref, o_ref, m_ref, l_ref, acc_ref, *, nk):
    ki = pl.program_id(2)

    @pl.when(ki == 0)
    def _():
        m_ref[...] = jnp.full(m_ref.shape, -jnp.inf, F32)
        l_ref[...] = jnp.zeros(l_ref.shape, F32)
        acc_ref[...] = jnp.zeros(acc_ref.shape, F32)

    groups = ATTN_HEADS // ATTN_KV_HEADS
    for h in range(ATTN_HEADS):
        kv = h // groups
        q = q_ref[0, :, h * HEAD_DIM:(h + 1) * HEAD_DIM]
        k = k_ref[0, :, kv * HEAD_DIM:(kv + 1) * HEAD_DIM]
        v = v_ref[0, :, kv * HEAD_DIM:(kv + 1) * HEAD_DIM]
        s = _nt(q, k)
        m_prev = m_ref[h]
        m_new = jnp.maximum(m_prev, jnp.max(s, axis=-1, keepdims=True))
        alpha = jnp.exp(m_prev - m_new)
        p = jnp.exp(s - m_new)
        l_ref[h] = alpha * l_ref[h] + jnp.sum(p, axis=-1, keepdims=True)
        acc_ref[h] = alpha * acc_ref[h] + _mm(p.astype(BF16), v)
        m_ref[h] = m_new

    @pl.when(ki == nk - 1)
    def _():
        out = [acc_ref[h] / l_ref[h] for h in range(ATTN_HEADS)]
        o_ref[0] = jnp.concatenate(out, axis=-1).astype(o_ref.dtype)


def _attention(q, k, v, tq, tk):
    b, t, _ = q.shape
    s = k.shape[1]
    nk = s // tk
    return pl.pallas_call(
        functools.partial(_attn_body, nk=nk),
        grid=(b, t // tq, nk),
        in_specs=[pl.BlockSpec((1, tq, ATTN_WIDTH), lambda bi, qi, ki: (bi, qi, 0)),
                  pl.BlockSpec((1, tk, KV_WIDTH), lambda bi, qi, ki: (bi, ki, 0)),
                  pl.BlockSpec((1, tk, KV_WIDTH), lambda bi, qi, ki: (bi, ki, 0))],
        out_specs=pl.BlockSpec((1, tq, ATTN_WIDTH), lambda bi, qi, ki: (bi, qi, 0)),
        out_shape=jax.ShapeDtypeStruct((b, t, ATTN_WIDTH), BF16),
        scratch_shapes=[pltpu.VMEM((ATTN_HEADS, tq, 1), F32),
                        pltpu.VMEM((ATTN_HEADS, tq, 1), F32),
                        pltpu.VMEM((ATTN_HEADS, tq, HEAD_DIM), F32)],
        compiler_params=_cparams("parallel", "parallel", "arbitrary"),
        name="attention",
    )(q, k, v)


def _conv_body(first_ref, last_ref, cur_ref, prev_ref, next_ref, w_ref, b_ref, o_ref):
    i = pl.program_id(0)
    x = cur_ref[...]
    tm = x.shape[0]
    prev = jnp.where(first_ref[i] == 1, 0.0, prev_ref[...])
    nxt = jnp.where(last_ref[i] == 1, 0.0, next_ref[...])
    row8 = lax.broadcasted_iota(jnp.int32, prev.shape, 0)
    w = w_ref[...]
    half = SSD_CONV // 2
    acc = b_ref[...] + w[half:half + 1] * x
    for s in range(1, half + 1):
        rolled = pltpu.roll(x, s, 0)
        top = jnp.where(row8 < s, pltpu.roll(prev, s, 0), rolled[:8])
        acc = acc + w[half - s:half - s + 1] * jnp.concatenate([top, rolled[8:]], axis=0)
        rolled = pltpu.roll(x, tm - s, 0)
        bot = jnp.where(row8 >= 8 - s, pltpu.roll(nxt, 8 - s, 0), rolled[tm - 8:])
        acc = acc + w[half + s:half + s + 1] * jnp.concatenate([rolled[:tm - 8], bot], axis=0)
    o_ref[...] = acc * _sigmoid(acc)


def _ssd_conv(first, last, proj, conv_w, conv_b):
    n_tok = proj.shape[0]
    tm = TOKEN_TILE
    sub = tm // 8
    col = COL_XBC // XBC_WIDTH
    w8 = jnp.concatenate([conv_w, jnp.zeros((8 - SSD_CONV, XBC_WIDTH), F32)], axis=0)
    grid_spec = pltpu.PrefetchScalarGridSpec(
        num_scalar_prefetch=2,
        grid=(n_tok // tm,),
        in_specs=[pl.BlockSpec((tm, XBC_WIDTH), lambda i, f, l: (i, col)),
                  pl.BlockSpec((8, XBC_WIDTH), lambda i, f, l: (jnp.maximum(i * sub - 1, 0), col)),
                  pl.BlockSpec((8, XBC_WIDTH),
                               lambda i, f, l: (jnp.minimum((i + 1) * sub, n_tok // 8 - 1), col)),
                  pl.BlockSpec((8, XBC_WIDTH), lambda i, f, l: (0, 0)),
                  pl.BlockSpec((1, XBC_WIDTH), lambda i, f, l: (0, 0))],
        out_specs=pl.BlockSpec((tm, XBC_WIDTH), lambda i, f, l: (i, 0)))
    return pl.pallas_call(
        _conv_body, grid_spec=grid_spec,
        out_shape=jax.ShapeDtypeStruct((n_tok, XBC_WIDTH), F32),
        compiler_params=_cparams("parallel"),
        name="ssd_conv",
    )(first, last, proj, proj, proj, w8, conv_b.reshape(1, XBC_WIDTH))


def _softplus(x):
    return jnp.maximum(x, 0.0) + jnp.log1p(jnp.exp(-jnp.abs(x)))


def _ssd_body(*refs, nc, has_init):
    if has_init:
        (xs_ref, bm_ref, cm_ref, dt_ref, z_ref, dtb_ref, an_ref, drow_ref, nw_ref, h0_ref,
         y_ref, fin_ref, y_scr, h_scr) = refs
    else:
        (xs_ref, bm_ref, cm_ref, dt_ref, z_ref, dtb_ref, an_ref, drow_ref, nw_ref,
         y_ref, fin_ref, y_scr, h_scr) = refs
    sweep_id = pl.program_id(1)
    c = pl.program_id(2)
    chunk_len = SSD_CHUNK
    rep = SSD_HEADS // SSD_GROUPS

    @pl.when(c == 0)
    def _():
        if has_init:
            h_scr[...] = h0_ref[0, 0]
        else:
            h_scr[...] = jnp.zeros(h_scr.shape, F32)

    def sweep(direction):
        chunk = c if direction == 0 else nc - 1 - c
        row0 = pl.multiple_of(chunk * chunk_len, chunk_len)
        xs = xs_ref[...]
        bm = bm_ref[...]
        cm = cm_ref[...]
        dt_all = _softplus(dt_ref[...] + dtb_ref[...])
        la_all = dt_all * an_ref[...]
        ti = lax.broadcasted_iota(jnp.int32, (chunk_len, chunk_len), 0)
        si = lax.broadcasted_iota(jnp.int32, (chunk_len, chunk_len), 1)
        before_eq = (si <= ti) if direction == 0 else (si >= ti)
        cs_all = _mm(before_eq.astype(F32), la_all, HI)
        cs_t = cs_all.T
        last = chunk_len - 1 if direction == 0 else 0
        heads = []
        for g in range(SSD_GROUPS):
            b_g = bm[:, g * SSD_STATE:(g + 1) * SSD_STATE].astype(BF16)
            c_g = cm[:, g * SSD_STATE:(g + 1) * SSD_STATE].astype(BF16)
            cb = _nt(c_g, b_g)
            for hh in range(rep):
                h = g * rep + hh
                j = direction * SSD_HEADS + h
                cs_c = cs_all[:, j:j + 1]
                cs_r = cs_t[j:j + 1, :]
                decay = jnp.exp(jnp.where(before_eq, cs_c - cs_r, -jnp.inf))
                xd = xs[:, h * HEAD_DIM:(h + 1) * HEAD_DIM] * dt_all[:, j:j + 1]
                state = h_scr[h]
                y = _mm((cb * decay).astype(BF16), xd.astype(BF16))
                y = y + _nt(c_g, state.astype(BF16)) * jnp.exp(cs_c)
                tot = cs_all[last:last + 1, j:j + 1]
                to_end = jnp.exp(tot - cs_c)
                h_scr[h] = jnp.exp(tot) * state + _tn((xd * to_end).astype(BF16), b_g)
                heads.append(y)
        y_dir = jnp.concatenate(heads, axis=-1)
        if direction == 0:
            y_scr[pl.ds(row0, chunk_len), :] = y_dir + drow_ref[...] * xs
        else:
            y = y_scr[pl.ds(row0, chunk_len), :] + y_dir
            z = z_ref[...]
            y = y * (z * _sigmoid(z))
            y_ref[...] = _rms(y, nw_ref[...]).astype(y_ref.dtype)

    @pl.when(sweep_id == 0)
    def _():
        sweep(0)

    @pl.when(sweep_id == 1)
    def _():
        sweep(1)

    @pl.when(c == nc - 1)
    def _():
        fin_ref[0, 0] = h_scr[...]


def _ssd(proj, xbc, tile_off, batch, seq_len, dt_bias, a_log, d_skip, norm_w, init):
    chunk_len = SSD_CHUNK
    nc = seq_len // chunk_len
    base = tile_off * (TOKEN_TILE // chunk_len)
    has_init = init is not None

    def chunk_of(s, c):
        return c + s * (nc - 1 - 2 * c)

    def row(b, s, c):
        return base + b * nc + chunk_of(s, c)

    def z_row(b, s, c):
        return base + b * nc + s * (nc - 1 - c) + (1 - s) * (nc - 1)

    def y_row(b, s, c):
        return b * nc + s * (nc - 1 - c) + (1 - s) * (nc - 1)

    dtb = jnp.zeros((1, LANES), F32).at[0, :2 * SSD_HEADS].set(dt_bias.reshape(-1))
    a_neg = jnp.zeros((1, LANES), F32).at[0, :2 * SSD_HEADS].set(-jnp.exp(a_log.reshape(-1)))
    d_row = jnp.repeat(d_skip, HEAD_DIM).reshape(1, SSD_WIDTH)
    xs_blk = COL_XBC // XBC_WIDTH * (XBC_WIDTH // LANES)
    in_specs = [
        pl.BlockSpec((chunk_len, SSD_WIDTH), lambda b, s, c: (row(b, s, c), 0)),
        pl.BlockSpec((chunk_len, LANES), lambda b, s, c: (row(b, s, c), SSD_WIDTH // LANES)),
        pl.BlockSpec((chunk_len, LANES), lambda b, s, c: (row(b, s, c), SSD_WIDTH // LANES + 1)),
        pl.BlockSpec((chunk_len, LANES), lambda b, s, c: (row(b, s, c), COL_DT // LANES)),
        pl.BlockSpec((chunk_len, SSD_WIDTH), lambda b, s, c: (z_row(b, s, c), COL_Z // SSD_WIDTH)),
        pl.BlockSpec((1, LANES), lambda b, s, c: (0, 0)),
        pl.BlockSpec((1, LANES), lambda b, s, c: (0, 0)),
        pl.BlockSpec((1, SSD_WIDTH), lambda b, s, c: (0, 0)),
        pl.BlockSpec((1, SSD_WIDTH), lambda b, s, c: (0, 0)),
    ]
    del xs_blk
    args = [xbc, xbc, xbc, proj, proj, dtb, a_neg, d_row, norm_w.reshape(1, SSD_WIDTH)]
    state_block = (1, 1, SSD_HEADS, HEAD_DIM, SSD_STATE)
    if has_init:
        in_specs.append(pl.BlockSpec(state_block, lambda b, s, c: (b, s, 0, 0, 0)))
        args.append(init)
    return pl.pallas_call(
        functools.partial(_ssd_body, nc=nc, has_init=has_init),
        grid=(batch, 2, nc),
        in_specs=in_specs,
        out_specs=[pl.BlockSpec((chunk_len, SSD_WIDTH), lambda b, s, c: (y_row(b, s, c), 0)),
                   pl.BlockSpec(state_block, lambda b, s, c: (b, s, 0, 0, 0))],
        out_shape=[jax.ShapeDtypeStruct((batch * seq_len, SSD_WIDTH), BF16),
                   jax.ShapeDtypeStruct((batch, 2, SSD_HEADS, HEAD_DIM, SSD_STATE), F32)],
        scratch_shapes=[pltpu.VMEM((seq_len, SSD_WIDTH), F32),
                        pltpu.VMEM((SSD_HEADS, HEAD_DIM, SSD_STATE), F32)],
        compiler_params=_cparams("parallel", "arbitrary", "arbitrary"),
        name="ssd",
    )(*args)


def _rwkv_body(*refs, nc, has_init):
    if has_init:
        (r_ref, k_ref, v_ref, walo_ref, glo_ref, w0_ref, wup_ref, a0_ref, aup_ref, gup_ref,
         kk_ref, ka_ref, rk_ref, lnw_ref, lnb_ref, s0_ref, y_ref, fin_ref, y_scr, s_scr) = refs
    else:
        (r_ref, k_ref, v_ref, walo_ref, glo_ref, w0_ref, wup_ref, a0_ref, aup_ref, gup_ref,
         kk_ref, ka_ref, rk_ref, lnw_ref, lnb_ref, y_ref, fin_ref, y_scr, s_scr) = refs
    sweep_id = pl.program_id(1)
    c = pl.program_id(2)
    cl = RWKV_CHUNK
    n = HEAD_DIM
    lora = walo_ref.shape[-1] // 2

    @pl.when(c == 0)
    def _():
        if has_init:
            s_scr[...] = s0_ref[0, 0]
        else:
            s_scr[...] = jnp.zeros(s_scr.shape, F32)

    def sweep(direction):
        chunk = c if direction == 0 else nc - 1 - c
        row0 = pl.multiple_of(chunk * cl, cl)
        r = r_ref[...]
        k = k_ref[...]
        v = v_ref[...]
        walo = walo_ref[...]
        zw = w0_ref[direction:direction + 1] + _mm(jnp.tanh(walo[:, :lora]), wup_ref[direction], HI)
        lw = -math.exp(-0.5) * _sigmoid(zw)
        a = _sigmoid(a0_ref[direction:direction + 1] + _mm(walo[:, lora:], aup_ref[direction], HI))
        kd = k * (1.0 + (a - 1.0) * ka_ref[...])
        kk_raw = k * kk_ref[...]
        ti = lax.broadcasted_iota(jnp.int32, (cl, cl), 0)
        si = lax.broadcasted_iota(jnp.int32, (cl, cl), 1)
        before_eq = (si <= ti) if direction == 0 else (si >= ti)
        before = (si < ti) if direction == 0 else (si > ti)
        lp_all = _mm(before_eq.astype(F32), lw, HI)
        last = cl - 1 if direction == 0 else 0
        outs = []
        for h in range(RWKV_HEADS):
            sl = slice(h * n, (h + 1) * n)
            kk = kk_raw[:, sl]
            kk = kk * lax.rsqrt(jnp.sum(kk * kk, axis=-1, keepdims=True) + 1e-12)
            lp = lp_all[:, sl]
            lw_h = lw[:, sl]
            dec_in = jnp.exp(lp)
            dec_out = jnp.exp(-lp)
            kap = kk * jnp.exp(lp - lw_h)
            rt = r[:, sl] * dec_in
            kt = kd[:, sl] * dec_out
            bt = a[:, sl] * kk * dec_out
            v_h = v[:, sl]
            state = s_scr[h]
            g = _nt(jnp.concatenate([kap, rt], axis=0), jnp.concatenate([kt, bt], axis=0), HI)
            n_mat = jnp.where(before, g[:cl, :cl], 0.0)
            l_mat = jnp.where(before, g[:cl, cl:], 0.0)
            rk_mat = jnp.where(before_eq, g[cl:, :cl], 0.0)
            rb_mat = jnp.where(before_eq, g[cl:, cl:], 0.0)
            from_state = _nt(jnp.concatenate([kap, rt], axis=0), state, HI)
            x = from_state[:cl] + _mm(n_mat, v_h, HI)
            x = x - _mm(l_mat, x, HI)
            pw = l_mat
            for _ in range(int(math.log2(cl)) - 1):
                pw = _mm(pw, pw, HI)
                x = x + _mm(pw, x, HI)
            u = x
            y_h = from_state[cl:] + _mm(rk_mat, v_h, HI) - _mm(rb_mat, u, HI)
            lp_last = lp[last:last + 1]
            to_end = jnp.exp(lp_last - lp)
            upd = _tn(jnp.concatenate([v_h, -u], axis=0),
                      jnp.concatenate([kd[:, sl] * to_end, a[:, sl] * kk * to_end], axis=0), HI)
            s_scr[h] = state * jnp.exp(lp_last) + upd
            outs.append(y_h)
        y_dir = jnp.concatenate(outs, axis=-1)
        if direction == 0:
            y_scr[pl.ds(row0, cl), :] = y_dir
        else:
            y = y_scr[pl.ds(row0, cl), :] + y_dir
            gate = _mm(_sigmoid(glo_ref[...]), gup_ref[...], HI)
            rkw = r * k * rk_ref[...]
            normed = []
            for h in range(RWKV_HEADS):
                sl = slice(h * n, (h + 1) * n)
                y_h = y[:, sl]
                mu = jnp.mean(y_h, axis=-1, keepdims=True)
                var = jnp.mean(jnp.square(y_h - mu), axis=-1, keepdims=True)
                bonus = jnp.sum(rkw[:, sl], axis=-1, keepdims=True) * v[:, sl]
                normed.append(((y_h - mu) * lax.rsqrt(var + GN_EPS), bonus))
            y_n = jnp.concatenate([t[0] for t in normed], axis=-1)
            bonus = jnp.concatenate([t[1] for t in normed], axis=-1)
            out = (y_n * lnw_ref[...] + lnb_ref[...] + bonus) * gate
            y_ref[...] = out.astype(y_ref.dtype)

    @pl.when(sweep_id == 0)
    def _():
        sweep(0)

    @pl.when(sweep_id == 1)
    def _():
        sweep(1)

    @pl.when(c == nc - 1)
    def _():
        fin_ref[0, 0] = s_scr[...]


def _rwkv(proj, tile_off, batch, seq_len, lp, init):
    cl = RWKV_CHUNK
    nc = seq_len // cl
    base = tile_off * (TOKEN_TILE // cl)
    has_init = init is not None
    w = RWKV_WIDTH

    def row(b, s, c):
        return base + b * nc + c + s * (nc - 1 - 2 * c)

    def tail_row(b, s, c):
        return base + b * nc + s * (nc - 1 - c) + (1 - s) * (nc - 1)

    def y_row(b, s, c):
        return b * nc + s * (nc - 1 - c) + (1 - s) * (nc - 1)

    def const(shape):
        return pl.BlockSpec(shape, lambda b, s, c: (0,) * len(shape))

    in_specs = [
        pl.BlockSpec((cl, w), lambda b, s, c: (row(b, s, c), COL_R // w)),
        pl.BlockSpec((cl, w), lambda b, s, c: (row(b, s, c), COL_KR // w)),
        pl.BlockSpec((cl, w), lambda b, s, c: (row(b, s, c), COL_VR // w)),
        pl.BlockSpec((cl, LANES), lambda b, s, c: (row(b, s, c), COL_WALO // LANES)),
        pl.BlockSpec((cl, LANES), lambda b, s, c: (tail_row(b, s, c), COL_GLO // LANES)),
        const((2, w)), const((2, LANES // 2, w)), const((2, w)), const((2, LANES // 2, w)),
        const((LANES, w)), const((1, w)), const((1, w)), const((1, w)), const((1, w)), const((1, w)),
    ]
    args = [proj, proj, proj, proj, proj, lp['rwkv_w0'], lp['rwkv_w_up'], lp['rwkv_a0'],
            lp['rwkv_a_up'], lp['rwkv_g_up'], lp['rwkv_k_k'].reshape(1, w),
            lp['rwkv_k_a'].reshape(1, w), lp['rwkv_r_k'].reshape(1, w),
            lp['rwkv_ln_w'].reshape(1, w), lp['rwkv_ln_b'].reshape(1, w)]
    state_block = (1, 1, RWKV_HEADS, HEAD_DIM, HEAD_DIM)
    if has_init:
        in_specs.append(pl.BlockSpec(state_block, lambda b, s, c: (b, s, 0, 0, 0)))
        args.append(init)
    return pl.pallas_call(
        functools.partial(_rwkv_body, nc=nc, has_init=has_init),
        grid=(batch, 2, nc),
        in_specs=in_specs,
        out_specs=[pl.BlockSpec((cl, w), lambda b, s, c: (y_row(b, s, c), 0)),
                   pl.BlockSpec(state_block, lambda b, s, c: (b, s, 0, 0, 0))],
        out_shape=[jax.ShapeDtypeStruct((batch * seq_len, w), BF16),
                   jax.ShapeDtypeStruct((batch, 2, RWKV_HEADS, HEAD_DIM, HEAD_DIM), F32)],
        scratch_shapes=[pltpu.VMEM((seq_len, w), F32),
                        pltpu.VMEM((RWKV_HEADS, HEAD_DIM, HEAD_DIM), F32)],
        compiler_params=_cparams("parallel", "arbitrary", "arbitrary"),
        name="rwkv",
    )(*args)


def _rwkv2_body(*refs, nc, has_init):
    fwd_refs, bwd_refs = refs[0:5], refs[5:10]
    (w0_ref, wup_ref, a0_ref, aup_ref, gup_ref, kk_ref, ka_ref, rk_ref, lnw_ref,
     lnb_ref) = refs[10:20]
    if has_init:
        s0_ref, y_ref, fin_ref, y_scr, s_scr = refs[20:]
    else:
        y_ref, fin_ref, y_scr, s_scr = refs[20:]
    c = pl.program_id(1)
    cl = RWKV_CHUNK
    n = HEAD_DIM
    lora = LANES // 2

    @pl.when(c == 0)
    def _():
        if has_init:
            s_scr[...] = s0_ref[0]
        else:
            s_scr[...] = jnp.zeros(s_scr.shape, F32)

    ti = lax.broadcasted_iota(jnp.int32, (cl, cl), 0)
    si = lax.broadcasted_iota(jnp.int32, (cl, cl), 1)
    lane2 = lax.broadcasted_iota(jnp.int32, (cl, 2 * cl), 1)
    left = lane2 < cl
    sign2 = jnp.where(left, 1.0, -1.0)
    t2 = lax.broadcasted_iota(jnp.int32, (cl, 2 * cl), 0)
    s2 = jnp.where(left, lane2, lane2 - cl)

    chains = []
    for direction, (r_ref, k_ref, v_ref, walo_ref, _) in enumerate((fwd_refs, bwd_refs)):
        r = r_ref[...]
        k = k_ref[...]
        v = v_ref[...]
        walo = walo_ref[...]
        zw = w0_ref[direction:direction + 1] + _mm(jnp.tanh(walo[:, :lora]), wup_ref[direction], HI)
        lw = -math.exp(-0.5) * _sigmoid(zw)
        a = _sigmoid(a0_ref[direction:direction + 1] + _mm(walo[:, lora:], aup_ref[direction], HI))
        kd = k * (1.0 + (a - 1.0) * ka_ref[...])
        kk_raw = k * kk_ref[...]
        before_eq = (si <= ti) if direction == 0 else (si >= ti)
        before = (si < ti) if direction == 0 else (si > ti)
        before_eq2 = (s2 <= t2) if direction == 0 else (s2 >= t2)
        lp_all = _mm(before_eq.astype(F32), lw, HI)
        last = cl - 1 if direction == 0 else 0
        lp_last = lp_all[last:last + 1]
        dec_in = jnp.exp(lp_all)
        dec_out = jnp.exp(-lp_all)
        dec_before = jnp.exp(lp_all - lw)
        to_end = jnp.exp(lp_last - lp_all)
        end_decay = jnp.exp(lp_last)
        rt_all = r * dec_in
        kt_all = kd * dec_out
        kend_all = kd * to_end
        for h in range(RWKV_HEADS):
            sl = slice(h * n, (h + 1) * n)
            kk = kk_raw[:, sl]
            kk = kk * lax.rsqrt(jnp.sum(kk * kk, axis=-1, keepdims=True) + 1e-12)
            akk = a[:, sl] * kk
            kap_rt = jnp.concatenate([kk * dec_before[:, sl], rt_all[:, sl]], axis=0).astype(BF16)
            kt_bt = jnp.concatenate([kt_all[:, sl], akk * dec_out[:, sl]], axis=0).astype(BF16)
            v_h = v[:, sl]
            state = s_scr[direction, h]
            g = _nt(kap_rt, kt_bt)
            n_mat = jnp.where(before, g[:cl, :cl], 0.0).astype(BF16)
            l_mat = jnp.where(before, g[:cl, cl:], 0.0)
            rkb = (jnp.where(before_eq2, g[cl:], 0.0) * sign2).astype(BF16)
            from_state = _nt(kap_rt, state.astype(BF16))
            x0 = from_state[:cl] + _mm(n_mat, v_h.astype(BF16))
            chains.append(dict(
                z=jnp.concatenate([-l_mat, x0], axis=1), rkb=rkb, y0=from_state[cl:], v=v_h,
                state=state, end_decay=end_decay[:, sl],
                kend=jnp.concatenate([kend_all[:, sl], akk * to_end[:, sl]], axis=0).astype(BF16)))

    for _ in range(int(math.log2(cl))):
        for ch in chains:
            z = ch['z']
            pz = _mm(z[:, :cl].astype(BF16), z.astype(BF16))
            ch['z'] = jnp.where(left, pz, z + pz)

    ys = []
    for idx, ch in enumerate(chains):
        direction, h = divmod(idx, RWKV_HEADS)
        u = ch['z'][:, cl:]
        vu = jnp.concatenate([ch['v'], u], axis=0).astype(BF16)
        ys.append(ch['y0'] + _mm(ch['rkb'], vu))
        v_nu = jnp.concatenate([ch['v'], -u], axis=0).astype(BF16)
        s_scr[direction, h] = ch['state'] * ch['end_decay'] + _tn(v_nu, ch['kend'])
    y_f = jnp.concatenate(ys[:RWKV_HEADS], axis=-1)
    y_b = jnp.concatenate(ys[RWKV_HEADS:], axis=-1)
    row_f = pl.multiple_of(c * cl, cl)
    row_b = pl.multiple_of((nc - 1 - c) * cl, cl)

    def finish(y, io_refs):
        r_ref, k_ref, v_ref, _, glo_ref = io_refs
        r = r_ref[...]
        v = v_ref[...]
        gate = _mm(_sigmoid(glo_ref[...]).astype(BF16), gup_ref[...].astype(BF16))
        rkw = r * k_ref[...] * rk_ref[...]
        y_n, bonus = [], []
        for h in range(RWKV_HEADS):
            sl = slice(h * n, (h + 1) * n)
            y_h = y[:, sl]
            mu = jnp.mean(y_h, axis=-1, keepdims=True)
            var = jnp.mean(jnp.square(y_h - mu), axis=-1, keepdims=True)
            y_n.append((y_h - mu) * lax.rsqrt(var + GN_EPS))
            bonus.append(jnp.sum(rkw[:, sl], axis=-1, keepdims=True) * v[:, sl])
        y_n = jnp.concatenate(y_n, axis=-1)
        bonus = jnp.concatenate(bonus, axis=-1)
        return ((y_n * lnw_ref[...] + lnb_ref[...] + bonus) * gate).astype(y_ref.dtype)

    @pl.when(c < nc // 2)
    def _():
        y_scr[pl.ds(row_f, cl), :] = y_f
        y_scr[pl.ds(row_b, cl), :] = y_b

    @pl.when(c >= nc // 2)
    def _():
        y_ref[pl.ds(row_f, cl), :] = finish(y_scr[pl.ds(row_f, cl), :] + y_f, fwd_refs)
        y_ref[pl.ds(row_b, cl), :] = finish(y_scr[pl.ds(row_b, cl), :] + y_b, bwd_refs)

    @pl.when(c == nc - 1)
    def _():
        fin_ref[0] = s_scr[...]


def _rwkv2(proj, tile_off, batch, seq_len, lp, init):
    cl = RWKV_CHUNK
    nc = seq_len // cl
    assert nc % 2 == 0
    base = tile_off * (TOKEN_TILE // cl)
    has_init = init is not None
    w = RWKV_WIDTH

    def fwd_row(b, c):
        return base + b * nc + c

    def bwd_row(b, c):
        return base + b * nc + nc - 1 - c

    def const(shape):
        return pl.BlockSpec(shape, lambda b, c: (0,) * len(shape))

    def chunk_specs(row):
        return [pl.BlockSpec((cl, w), lambda b, c: (row(b, c), COL_R // w)),
                pl.BlockSpec((cl, w), lambda b, c: (row(b, c), COL_KR // w)),
                pl.BlockSpec((cl, w), lambda b, c: (row(b, c), COL_VR // w)),
                pl.BlockSpec((cl, LANES), lambda b, c: (row(b, c), COL_WALO // LANES)),
                pl.BlockSpec((cl, LANES), lambda b, c: (row(b, c), COL_GLO // LANES))]

    in_specs = chunk_specs(fwd_row) + chunk_specs(bwd_row) + [
        const((2, w)), const((2, LANES // 2, w)), const((2, w)), const((2, LANES // 2, w)),
        const((LANES, w)), const((1, w)), const((1, w)), const((1, w)), const((1, w)), const((1, w))]
    args = [proj] * 10 + [lp['rwkv_w0'], lp['rwkv_w_up'], lp['rwkv_a0'],
                          lp['rwkv_a_up'], lp['rwkv_g_up'], lp['rwkv_k_k'].reshape(1, w),
                          lp['rwkv_k_a'].reshape(1, w), lp['rwkv_r_k'].reshape(1, w),
                          lp['rwkv_ln_w'].reshape(1, w), lp['rwkv_ln_b'].reshape(1, w)]
    state_block = (1, 2, RWKV_HEADS, HEAD_DIM, HEAD_DIM)
    if has_init:
        in_specs.append(pl.BlockSpec(state_block, lambda b, c: (b, 0, 0, 0, 0)))
        args.append(init)
    return pl.pallas_call(
        functools.partial(_rwkv2_body, nc=nc, has_init=has_init),
        grid=(batch, nc),
        in_specs=in_specs,
        out_specs=[pl.BlockSpec((seq_len, w), lambda b, c: (b, 0)),
                   pl.BlockSpec(state_block, lambda b, c: (b, 0, 0, 0, 0))],
        out_shape=[jax.ShapeDtypeStruct((batch * seq_len, w), BF16),
                   jax.ShapeDtypeStruct((batch, 2, RWKV_HEADS, HEAD_DIM, HEAD_DIM), F32)],
        scratch_shapes=[pltpu.VMEM((seq_len, w), F32),
                        pltpu.VMEM((2, RWKV_HEADS, HEAD_DIM, HEAD_DIM), F32)],
        compiler_params=_cparams("parallel", "arbitrary"),
        name="rwkv",
    )(*args)


def _out_proj_body(tmod_ref, attn_ref, ssd_ref, rwkv_ref, h_ref, mod_ref, nw_ref, w_ref, rw_ref,
                   rb_ref, h_out, u_out, lg_out):
    del tmod_ref
    mixed = (_mm(attn_ref[...], w_ref[:ATTN_WIDTH])
             + _mm(ssd_ref[...], w_ref[ATTN_WIDTH:ATTN_WIDTH + SSD_WIDTH])
             + _mm(rwkv_ref[...], w_ref[ATTN_WIDTH + SSD_WIDTH:]))
    mod = mod_ref[0]
    h = h_ref[...] + mod[2:3] * mixed
    h_out[...] = h
    u = _rms(h, nw_ref[...]) * (1.0 + mod[4:5]) + mod[3:4]
    u_out[...] = u
    lg_out[...] = _mm(u, rw_ref[...], HI) + rb_ref[...]


def _out_proj(tile_mod, attn, ssd, rwkv, h, mod, norm_w, w, router_w, router_b):
    n_tok, d = h.shape
    tm = TOKEN_TILE
    rw = jnp.zeros((d, LANES), F32).at[:, :N_EXPERTS].set(router_w)
    rb = jnp.zeros((1, LANES), F32).at[0, :N_EXPERTS].set(router_b)
    grid_spec = pltpu.PrefetchScalarGridSpec(
        num_scalar_prefetch=1,
        grid=(n_tok // tm,),
        in_specs=[pl.BlockSpec((tm, ATTN_WIDTH), lambda i, t: (i, 0)),
                  pl.BlockSpec((tm, SSD_WIDTH), lambda i, t: (i, 0)),
                  pl.BlockSpec((tm, RWKV_WIDTH), lambda i, t: (i, 0)),
                  pl.BlockSpec((tm, d), lambda i, t: (i, 0)),
                  pl.BlockSpec((1, 6, d), lambda i, t: (t[i], 0, 0)),
                  pl.BlockSpec((1, d), lambda i, t: (0, 0)),
                  pl.BlockSpec((d, d), lambda i, t: (0, 0)),
                  pl.BlockSpec((d, LANES), lambda i, t: (0, 0)),
                  pl.BlockSpec((1, LANES), lambda i, t: (0, 0))],
        out_specs=[pl.BlockSpec((tm, d), lambda i, t: (i, 0)),
                   pl.BlockSpec((tm, d), lambda i, t: (i, 0)),
                   pl.BlockSpec((tm, LANES), lambda i, t: (i, 0))])
    return pl.pallas_call(
        _out_proj_body, grid_spec=grid_spec,
        out_shape=[jax.ShapeDtypeStruct((n_tok, d), F32),
                   jax.ShapeDtypeStruct((n_tok, d), F32),
                   jax.ShapeDtypeStruct((n_tok, LANES), F32)],
        compiler_params=_cparams("parallel"),
        name="out_proj",
    )(tile_mod, attn, ssd, rwkv, h, mod, norm_w.reshape(1, d), w, rw, rb)


def _gather_body(idx_ref, x_hbm, o_ref, idx_smem, isem, sem):
    rows = o_ref.shape[0]
    cp = pltpu.make_async_copy(idx_ref.at[0, 0], idx_smem, isem)
    cp.start()
    cp.wait()

    def issue(r, carry):
        pltpu.make_async_copy(x_hbm.at[pl.ds(idx_smem[r], 1)], o_ref.at[pl.ds(r, 1)], sem).start()
        return carry

    lax.fori_loop(0, rows, issue, 0, unroll=8)
    pltpu.make_async_copy(x_hbm.at[pl.ds(0, rows)], o_ref, sem).wait()


def _gather_rows(x, idx):
    n_rows = idx.shape[0]
    rows = GATHER_ROWS
    d = x.shape[1]
    steps = n_rows // rows
    return pl.pallas_call(
        _gather_body,
        grid=(steps,),
        in_specs=[pl.BlockSpec((1, 1, rows), lambda i: (i, 0, 0)),
                  pl.BlockSpec(memory_space=pl.ANY)],
        out_specs=pl.BlockSpec((rows, d), lambda i: (i, 0)),
        out_shape=jax.ShapeDtypeStruct((n_rows, d), x.dtype),
        scratch_shapes=[pltpu.SMEM((rows,), jnp.int32), pltpu.SemaphoreType.DMA,
                        pltpu.SemaphoreType.DMA],
        compiler_params=_cparams("arbitrary"),
        name="moe_dispatch",
    )(idx.reshape(steps, 1, rows), x)


def _moe_body(be_ref, nv_ref, x_ref, wgu_ref, bgu_ref, wdn_ref, bdn_ref, o_ref):
    del be_ref
    i = pl.program_id(0)

    @pl.when(i < nv_ref[0])
    def _():
        gu = _mm(x_ref[...].astype(BF16), wgu_ref[0]) + bgu_ref[0]
        gate = jnp.minimum(gu[:, :D_EXPERT], SWIGLU_LIMIT)
        up = jnp.clip(gu[:, D_EXPERT:], -SWIGLU_LIMIT, SWIGLU_LIMIT)
        act = (up + 1.0) * gate * _sigmoid(SWIGLU_ALPHA * gate)
        o_ref[...] = _mm(act.astype(BF16), wdn_ref[0]) + bdn_ref[0]

    @pl.when(i >= nv_ref[0])
    def _():
        o_ref[...] = jnp.zeros(o_ref.shape, o_ref.dtype)


def _moe_experts(block_expert, n_valid, x_rows, w_gu, b_gu, w_dn, b_dn):
    n_rows, d = x_rows.shape
    rows = MOE_ROWS
    grid_spec = pltpu.PrefetchScalarGridSpec(
        num_scalar_prefetch=2,
        grid=(n_rows // rows,),
        in_specs=[pl.BlockSpec((rows, d), lambda i, be, nv: (i, 0)),
                  pl.BlockSpec((1, d, 2 * D_EXPERT), lambda i, be, nv: (be[i], 0, 0)),
                  pl.BlockSpec((1, 1, 2 * D_EXPERT), lambda i, be, nv: (be[i], 0, 0)),
                  pl.BlockSpec((1, D_EXPERT, d), lambda i, be, nv: (be[i], 0, 0)),
                  pl.BlockSpec((1, 1, d), lambda i, be, nv: (be[i], 0, 0))],
        out_specs=pl.BlockSpec((rows, d), lambda i, be, nv: (i, 0)))
    return pl.pallas_call(
        _moe_body, grid_spec=grid_spec,
        out_shape=jax.ShapeDtypeStruct((n_rows, d), F32),
        compiler_params=_cparams("arbitrary"),
        name="moe_experts",
    )(block_expert, n_valid, x_rows, w_gu, b_gu.reshape(N_EXPERTS, 1, -1), w_dn,
      b_dn.reshape(N_EXPERTS, 1, -1))


def _combine_body(tmod_ref, pos_ref, gates_ref, h_ref, mod_ref, fw_ref, y_hbm, o_ref, pos_smem,
                  buf, isem, sem, *, final):
    del tmod_ref
    tm = h_ref.shape[0]
    cp = pltpu.make_async_copy(pos_ref.at[0, 0], pos_smem, isem)
    cp.start()
    cp.wait()

    def issue(r, carry):
        pltpu.make_async_copy(y_hbm.at[pl.ds(pos_smem[r], 1)], buf.at[pl.ds(r, 1)], sem).start()
        return carry

    lax.fori_loop(0, TOP_K * tm, issue, 0)
    pltpu.make_async_copy(y_hbm.at[pl.ds(0, TOP_K * tm)], buf, sem).wait()
    gates = gates_ref[...]
    y = gates[:, 0:1] * buf[0:tm]
    for kk in range(1, TOP_K):
        y = y + gates[:, kk:kk + 1] * buf[kk * tm:(kk + 1) * tm]
    h = h_ref[...] + mod_ref[0][5:6] * y
    if final:
        o_ref[...] = _rms(h, fw_ref[...])
    else:
        o_ref[...] = h


def _moe_combine(tile_mod, pos, gates, h, mod, final_w, y_rows, final):
    n_tok, d = h.shape
    tm = COMBINE_TILE
    steps = n_tok // tm
    ratio = TOKEN_TILE // tm
    pos_t = pos.reshape(steps, tm, TOP_K).transpose(0, 2, 1).reshape(steps, 1, TOP_K * tm)
    grid_spec = pltpu.PrefetchScalarGridSpec(
        num_scalar_prefetch=1,
        grid=(steps,),
        in_specs=[pl.BlockSpec((1, 1, TOP_K * tm), lambda i, t: (i, 0, 0)),
                  pl.BlockSpec((tm, TOP_K), lambda i, t: (i, 0)),
                  pl.BlockSpec((tm, d), lambda i, t: (i, 0)),
                  pl.BlockSpec((1, 6, d), lambda i, t: (t[i // ratio], 0, 0)),
                  pl.BlockSpec((1, d), lambda i, t: (0, 0)),
                  pl.BlockSpec(memory_space=pl.ANY)],
        out_specs=pl.BlockSpec((tm, d), lambda i, t: (i, 0)),
        scratch_shapes=[pltpu.SMEM((TOP_K * tm,), jnp.int32),
                        pltpu.VMEM((TOP_K * tm, d), F32),
                        pltpu.SemaphoreType.DMA, pltpu.SemaphoreType.DMA])
    return pl.pallas_call(
        functools.partial(_combine_body, final=final), grid_spec=grid_spec,
        out_shape=jax.ShapeDtypeStruct((n_tok, d), F32),
        compiler_params=_cparams("arbitrary"),
        name="moe_combine",
    )(tile_mod, pos_t, gates, h, mod, final_w.reshape(1, d), y_rows)


def _route(logits):
    n_tok = logits.shape[0]
    n_assign = n_tok * TOP_K
    rows = MOE_ROWS
    n_blocks = n_assign // rows + N_EXPERTS
    top_logit, top_idx = lax.top_k(logits, TOP_K)
    gates = jax.nn.softmax(top_logit, axis=-1)
    e_flat = top_idx.reshape(-1)
    tok_flat = jnp.repeat(jnp.arange(n_tok, dtype=jnp.int32), TOP_K)
    order = jnp.argsort(e_flat)
    e_sorted = e_flat[order]
    tok_sorted = tok_flat[order]
    counts = jnp.zeros((N_EXPERTS,), jnp.int32).at[e_flat].add(1)
    padded = (counts + rows - 1) // rows * rows
    start = jnp.cumsum(counts) - counts
    padded_end = jnp.cumsum(padded)
    padded_start = padded_end - padded
    dest = padded_start[e_sorted] + jnp.arange(n_assign, dtype=jnp.int32) - start[e_sorted]
    row_tok = jnp.zeros((n_blocks * rows,), jnp.int32).at[dest].set(tok_sorted)
    pos = jnp.zeros((n_assign,), jnp.int32).at[order].set(dest).reshape(n_tok, TOP_K)
    block_start = jnp.arange(n_blocks, dtype=jnp.int32) * rows
    block_expert = jnp.minimum(jnp.searchsorted(padded_end, block_start, side='right'),
                               N_EXPERTS - 1).astype(jnp.int32)
    n_valid = (padded_end[-1] // rows).astype(jnp.int32).reshape(1)
    return row_tok, pos, gates, block_expert, n_valid


def _permute_w_in(w_in):
    def cols(a, b):
        return w_in[:, a:b]
    q, k, v, z = cols(0, 384), cols(384, 512), cols(512, 640), cols(640, 1024)
    xbc, dt = cols(1024, 1664), cols(1664, 1676)
    r, kr, vr = cols(1676, 1932), cols(1932, 2188), cols(2188, 2444)
    walo, glo = cols(2444, 2572), cols(2572, 2700)
    dt = jnp.pad(dt, ((0, 0), (0, LANES - dt.shape[1])))
    return jnp.concatenate([q, z, r, kr, k, v, dt, walo, glo, xbc, vr], axis=1).astype(BF16)


def kernel(x_prompt, x_sample, cache_attn_k, cache_attn_v, state_ssd, state_rwkv, c, c_ctx,
           norm1_w, norm2_w, ada_w, ada_b, w_in, q_norm_w, k_norm_w,
           ssd_conv_w, ssd_conv_b, ssd_dt_bias, ssd_a_log, ssd_d, ssd_norm_w,
           rwkv_w0, rwkv_w_up, rwkv_a0, rwkv_a_up, rwkv_g_up, rwkv_k_k, rwkv_k_a, rwkv_r_k,
           rwkv_ln_w, rwkv_ln_b, w_out, router_w, router_b,
           moe_w_gate_up, moe_b_gate_up, moe_w_down, moe_b_down, final_norm_w):
    batch, seq, d = x_prompt.shape
    dec_batch, dec_seq, _ = x_sample.shape
    depth = norm1_w.shape[0]
    past = cache_attn_k.shape[2]
    n_ctx, n_lat = batch * seq, dec_batch * dec_seq
    tm = TOKEN_TILE
    ctx_tiles, lat_tiles = n_ctx // tm, n_lat // tm

    tile_mod = np.concatenate([np.zeros(ctx_tiles, np.int32),
                               1 + np.arange(lat_tiles, dtype=np.int32) // (dec_seq // tm)])
    seq_tiles = np.concatenate([np.full(ctx_tiles, seq // tm), np.full(lat_tiles, dec_seq // tm)])
    tile_in_seq = np.concatenate([np.arange(ctx_tiles) % (seq // tm),
                                  np.arange(lat_tiles) % (dec_seq // tm)])
    first = jnp.asarray((tile_in_seq == 0).astype(np.int32))
    last = jnp.asarray((tile_in_seq == seq_tiles - 1).astype(np.int32))
    tile_mod = jnp.asarray(tile_mod)

    cond = jnp.concatenate([c_ctx[None, :], c], axis=0)
    pad_rows = -cond.shape[0] % 8
    cond = jnp.pad(cond, ((0, pad_rows), (0, 0)))
    mod_all = _ada(cond, ada_w, ada_b).reshape(depth, cond.shape[0], 6, d)

    rope_tabs = _rope_tables(dec_seq)
    h = jnp.concatenate([x_prompt.reshape(n_ctx, d), x_sample.reshape(n_lat, d)], axis=0)
    ks, vs, ssd_states, rwkv_states = [], [], [], []
    for l in range(depth):
        lp = dict(rwkv_w0=rwkv_w0[l], rwkv_w_up=rwkv_w_up[l], rwkv_a0=rwkv_a0[l],
                  rwkv_a_up=rwkv_a_up[l], rwkv_g_up=rwkv_g_up[l], rwkv_k_k=rwkv_k_k[l],
                  rwkv_k_a=rwkv_k_a[l], rwkv_r_k=rwkv_r_k[l], rwkv_ln_w=rwkv_ln_w[l],
                  rwkv_ln_b=rwkv_ln_b[l])
        mod = mod_all[l]
        proj = _in_proj(tile_mod, h, mod, norm1_w[l], _permute_w_in(w_in[l]))

        q_c, k_c, v_c, k_norm = _attn_prep(proj, 0, n_ctx, q_norm_w[l], k_norm_w[l], None, 1)
        q_l, k_l, v_l = _attn_prep(proj, ctx_tiles, n_lat, q_norm_w[l], k_norm_w[l], rope_tabs,
                                   dec_seq // tm)
        attn_c = _attention(q_c.reshape(batch, seq, -1), k_c.reshape(batch, seq, -1),
                            v_c.reshape(batch, seq, -1), seq, seq)
        k_full = jnp.concatenate([cache_attn_k[:, l].reshape(dec_batch, past, -1).astype(BF16),
                                  k_l.reshape(dec_batch, dec_seq, -1)], axis=1)
        v_full = jnp.concatenate([cache_attn_v[:, l].reshape(dec_batch, past, -1).astype(BF16),
                                  v_l.reshape(dec_batch, dec_seq, -1)], axis=1)
        attn_l = _attention(q_l.reshape(dec_batch, dec_seq, -1), k_full, v_full, 256, 512)
        attn = jnp.concatenate([attn_c.reshape(n_ctx, -1), attn_l.reshape(n_lat, -1)], axis=0)
        ks.append(k_norm.reshape(batch, seq, ATTN_KV_HEADS, HEAD_DIM))
        vs.append(proj[:n_ctx, COL_V:COL_V + KV_WIDTH].reshape(batch, seq, ATTN_KV_HEADS, HEAD_DIM))

        xbc = _ssd_conv(first, last, proj, ssd_conv_w[l], ssd_conv_b[l])
        ssd_c, ssd_fin = _ssd(proj, xbc, 0, batch, seq, ssd_dt_bias[l], ssd_a_log[l], ssd_d[l],
                              ssd_norm_w[l], None)
        ssd_l, _ = _ssd(proj, xbc, ctx_tiles, dec_batch, dec_seq, ssd_dt_bias[l], ssd_a_log[l],
                        ssd_d[l], ssd_norm_w[l], state_ssd[:, l])
        ssd = jnp.concatenate([ssd_c, ssd_l], axis=0)
        ssd_states.append(ssd_fin)

        rwkv_c, rwkv_fin = _rwkv2(proj, 0, batch, seq, lp, None)
        rwkv_l, _ = _rwkv2(proj, ctx_tiles, dec_batch, dec_seq, lp, state_rwkv[:, l])
        rwkv = jnp.concatenate([rwkv_c, rwkv_l], axis=0)
        rwkv_states.append(rwkv_fin)

        h, u, logits = _out_proj(tile_mod, attn, ssd, rwkv, h, mod, norm2_w[l],
                                 w_out[l].astype(BF16), router_w[l], router_b[l])

        row_tok, pos, gates, block_expert, n_valid = _route(logits[:, :N_EXPERTS])
        x_rows = _gather_rows(u, row_tok)
        y_rows = _moe_experts(block_expert, n_valid, x_rows, moe_w_gate_up[l].astype(BF16),
                              moe_b_gate_up[l], moe_w_down[l].astype(BF16), moe_b_down[l])
        h = _moe_combine(tile_mod, pos, gates, h, mod, final_norm_w, y_rows, l == depth - 1)

    y_prompt = h[:n_ctx].reshape(batch, seq, d)
    y_sample = h[n_ctx:].reshape(dec_batch, dec_seq, d)
    return (y_prompt, y_sample, jnp.stack(ks, axis=1), jnp.stack(vs, axis=1),
            jnp.stack(ssd_states, axis=1), jnp.stack(rwkv_states, axis=1))
```

```python
import functools
import math

import numpy as np
import jax
import jax.numpy as jnp
from jax import lax
from jax.experimental import pallas as pl
from jax.experimental.pallas import tpu as pltpu

F32 = jnp.float32
BF16 = jnp.bfloat16
HI = lax.Precision.HIGHEST

D_MODEL = 1024
GRID_W = 64
HEAD_DIM = 64
ATTN_HEADS = 6
ATTN_KV_HEADS = 2
ROPE_THETA = 10000.0
SSD_HEADS = 6
SSD_GROUPS = 2
SSD_STATE = 64
SSD_CONV = 5
SSD_CHUNK = 128
RWKV_HEADS = 4
RWKV_CHUNK = 64
N_EXPERTS = 32
TOP_K = 4
D_EXPERT = 1024
SWIGLU_LIMIT = 7.0
SWIGLU_ALPHA = 1.702
RMS_EPS = 1e-6
GN_EPS = 64e-5

ATTN_WIDTH = ATTN_HEADS * HEAD_DIM
KV_WIDTH = ATTN_KV_HEADS * HEAD_DIM
SSD_WIDTH = SSD_HEADS * HEAD_DIM
XBC_WIDTH = SSD_WIDTH + 2 * SSD_GROUPS * SSD_STATE
RWKV_WIDTH = RWKV_HEADS * HEAD_DIM
LANES = 128

COL_Q, COL_Z, COL_R, COL_KR = 0, 384, 768, 1024
COL_K, COL_V, COL_DT, COL_WALO, COL_GLO = 1280, 1408, 1536, 1664, 1792
COL_XBC, COL_VR = 1920, 2560
PROJ_WIDTH = 2816

TOKEN_TILE = 256
MOE_ROWS = 512
DISPATCH_TILE = 256
COMBINE_TILE = 128
VMEM_LIMIT = 56 * 1024 * 1024


def _cparams(*sem):
    return pltpu.CompilerParams(dimension_semantics=sem, vmem_limit_bytes=VMEM_LIMIT)


def _nt(a, b, precision=None):
    return lax.dot_general(a, b, (((1,), (1,)), ((), ())), precision=precision,
                           preferred_element_type=F32)


def _tn(a, b, precision=None):
    return lax.dot_general(a, b, (((0,), (0,)), ((), ())), precision=precision,
                           preferred_element_type=F32)


def _mm(a, b, precision=None):
    return jnp.dot(a, b, precision=precision, preferred_element_type=F32)


def _sigmoid(x):
    return 1.0 / (1.0 + jnp.exp(-x))


def _rms(x, w):
    ms = jnp.mean(x * x, axis=-1, keepdims=True)
    return x * lax.rsqrt(ms + RMS_EPS) * w


def _ada_body(c_ref, w_ref, b_ref, o_ref):
    x = c_ref[...]
    x = x * _sigmoid(x)
    o_ref[0] = _mm(x, w_ref[0], HI) + b_ref[0]


def _ada(cond, ada_w, ada_b):
    depth, d, n = ada_w.shape
    rows = cond.shape[0]
    tn = 512
    return pl.pallas_call(
        _ada_body,
        grid=(depth, n // tn),
        in_specs=[pl.BlockSpec((rows, d), lambda l, j: (0, 0)),
                  pl.BlockSpec((1, d, tn), lambda l, j: (l, 0, j)),
                  pl.BlockSpec((1, 1, tn), lambda l, j: (l, 0, j))],
        out_specs=pl.BlockSpec((1, rows, tn), lambda l, j: (l, 0, j)),
        out_shape=jax.ShapeDtypeStruct((depth, rows, n), F32),
        compiler_params=_cparams("parallel", "parallel"),
        name="ada_mod",
    )(cond, ada_w, ada_b.reshape(depth, 1, n))


def _in_proj_body(tmod_ref, h_ref, mod_ref, nw_ref, w_ref, o_ref):
    del tmod_ref
    mod = mod_ref[0]
    u = _rms(h_ref[...], nw_ref[...]) * (1.0 + mod[1:2]) + mod[0:1]
    o_ref[...] = _mm(u.astype(BF16), w_ref[...])


def _in_proj(tile_mod, h, mod, norm_w, w):
    n_tok, d = h.shape
    tm = TOKEN_TILE
    grid_spec = pltpu.PrefetchScalarGridSpec(
        num_scalar_prefetch=1,
        grid=(n_tok // tm,),
        in_specs=[pl.BlockSpec((tm, d), lambda i, t: (i, 0)),
                  pl.BlockSpec((1, 6, d), lambda i, t: (t[i], 0, 0)),
                  pl.BlockSpec((1, d), lambda i, t: (0, 0)),
                  pl.BlockSpec((d, PROJ_WIDTH), lambda i, t: (0, 0))],
        out_specs=pl.BlockSpec((tm, PROJ_WIDTH), lambda i, t: (i, 0)))
    return pl.pallas_call(
        _in_proj_body, grid_spec=grid_spec,
        out_shape=jax.ShapeDtypeStruct((n_tok, PROJ_WIDTH), F32),
        compiler_params=_cparams("parallel"),
        name="in_proj",
    )(tile_mod, h, mod, norm_w.reshape(1, d), w)


def _rope_apply(x, cos, sin_signed):
    width = x.shape[-1]
    quarter = HEAD_DIM // 4
    up = pltpu.roll(x, width - quarter, 1)
    down = pltpu.roll(x, quarter, 1)
    lane = lax.broadcasted_iota(jnp.int32, x.shape, 1)
    first = (lane % (2 * quarter)) < quarter
    return x * cos + jnp.where(first, up, down) * sin_signed


def _prep_body(*refs, rope):
    if rope:
        (q_ref, k_ref, v_ref, qnw_ref, knw_ref, bd_ref, cq_ref, sq_ref, ck_ref, sk_ref,
         qo_ref, ko_ref, vo_ref) = refs
    else:
        q_ref, k_ref, v_ref, qnw_ref, knw_ref, bd_ref, qo_ref, ko_ref, vo_ref, kn_ref = refs
    bd = bd_ref[...]
    q = q_ref[...]
    k = k_ref[...]
    qn = q * lax.rsqrt(_mm(q * q, bd, HI) + RMS_EPS) * qnw_ref[...]
    kn = k * lax.rsqrt(_mm(k * k, bd[:KV_WIDTH, :KV_WIDTH], HI) + RMS_EPS) * knw_ref[...]
    if rope:
        qn = _rope_apply(qn, cq_ref[...], sq_ref[...])
        kr = _rope_apply(kn, ck_ref[...], sk_ref[...])
    else:
        kn_ref[...] = kn
        kr = kn
    qo_ref[...] = (qn * (HEAD_DIM ** -0.5 * math.log2(math.e))).T.astype(BF16)
    ko_ref[...] = kr.astype(BF16)
    vo_ref[...] = v_ref[...].T.astype(BF16)


def _attn_prep(proj, tile_off, n_tok, q_norm_w, k_norm_w, rope_tabs, tiles_per_seq):
    tm = TOKEN_TILE
    rope = rope_tabs is not None
    head_of = np.arange(ATTN_WIDTH) // HEAD_DIM
    bd = jnp.asarray((head_of[:, None] == head_of[None, :]).astype(np.float32) / HEAD_DIM)
    qnw = jnp.tile(q_norm_w, ATTN_HEADS).reshape(1, ATTN_WIDTH)
    knw = jnp.tile(k_norm_w, ATTN_KV_HEADS).reshape(1, KV_WIDTH)
    in_specs = [pl.BlockSpec((tm, ATTN_WIDTH), lambda i: (i + tile_off, COL_Q // ATTN_WIDTH)),
                pl.BlockSpec((tm, KV_WIDTH), lambda i: (i + tile_off, COL_K // KV_WIDTH)),
                pl.BlockSpec((tm, KV_WIDTH), lambda i: (i + tile_off, COL_V // KV_WIDTH)),
                pl.BlockSpec((1, ATTN_WIDTH), lambda i: (0, 0)),
                pl.BlockSpec((1, KV_WIDTH), lambda i: (0, 0)),
                pl.BlockSpec((ATTN_WIDTH, ATTN_WIDTH), lambda i: (0, 0))]
    args = [proj, proj, proj, qnw, knw, bd]
    out_specs = [pl.BlockSpec((ATTN_WIDTH, tm), lambda i: (0, i)),
                 pl.BlockSpec((tm, KV_WIDTH), lambda i: (i, 0)),
                 pl.BlockSpec((KV_WIDTH, tm), lambda i: (0, i))]
    out_shape = [jax.ShapeDtypeStruct((ATTN_WIDTH, n_tok), BF16),
                 jax.ShapeDtypeStruct((n_tok, KV_WIDTH), BF16),
                 jax.ShapeDtypeStruct((KV_WIDTH, n_tok), BF16)]
    if rope:
        cq, sq, ck, sk = rope_tabs
        in_specs += [pl.BlockSpec((tm, ATTN_WIDTH), lambda i: (i % tiles_per_seq, 0)),
                     pl.BlockSpec((tm, ATTN_WIDTH), lambda i: (i % tiles_per_seq, 0)),
                     pl.BlockSpec((tm, KV_WIDTH), lambda i: (i % tiles_per_seq, 0)),
                     pl.BlockSpec((tm, KV_WIDTH), lambda i: (i % tiles_per_seq, 0))]
        args += [cq, sq, ck, sk]
    else:
        out_specs.append(pl.BlockSpec((tm, KV_WIDTH), lambda i: (i, 0)))
        out_shape.append(jax.ShapeDtypeStruct((n_tok, KV_WIDTH), F32))
    return pl.pallas_call(
        functools.partial(_prep_body, rope=rope),
        grid=(n_tok // tm,), in_specs=in_specs, out_specs=out_specs, out_shape=out_shape,
        compiler_params=_cparams("parallel"),
        name="attn_prep_rope" if rope else "attn_prep",
    )(*args)


def _rope_tables(seq_len):
    quarter = HEAD_DIM // 4
    half = HEAD_DIM // 2
    t = np.arange(seq_len)
    pos = np.stack([t // GRID_W, t % GRID_W], axis=1).astype(np.float32)
    inv_freq = jnp.power(ROPE_THETA, -jnp.arange(0, half, 2, dtype=F32) / half)
    d = np.arange(HEAD_DIM)
    which = d // half
    freq = inv_freq[d % quarter]
    ang = jnp.asarray(pos)[:, which] * freq[None, :]
    sign = np.where((d % half) < quarter, -1.0, 1.0).astype(np.float32)
    cos, sin = jnp.cos(ang), jnp.sin(ang) * sign[None, :]
    return (jnp.tile(cos, (1, ATTN_HEADS)), jnp.tile(sin, (1, ATTN_HEADS)),
            jnp.tile(cos, (1, ATTN_KV_HEADS)), jnp.tile(sin, (1, ATTN_KV_HEADS)))


def _attn_body(qt_ref, k_ref, vt_ref, o_ref, m_ref, l_ref, acc_ref, *, nk):
    ki = pl.program_id(2)

    @pl.when(ki == 0)
    def _():
        m_ref[...] = jnp.full(m_ref.shape, -jnp.inf, F32)
        l_ref[...] = jnp.zeros(l_ref.shape, F32)
        acc_ref[...] = jnp.zeros(acc_ref.shape, F32)

    groups = ATTN_HEADS // ATTN_KV_HEADS
    m_all = m_ref[...]
    l_all = l_ref[...]
    k_all = k_ref[0]
    vt_all = vt_ref[0]
    scores, probs, new_m, new_l, alphas, pvs = {}, {}, {}, {}, {}, {}

    def stage_scores(h):
        kv = h // groups
        qt = qt_ref[h * HEAD_DIM:(h + 1) * HEAD_DIM, :]
        scores[h] = _mm(k_all[:, kv * HEAD_DIM:(kv + 1) * HEAD_DIM], qt)

    def stage_softmax(h):
        st = scores.pop(h)
        m_prev = m_all[h]
        m_new = jnp.maximum(m_prev, jnp.max(st, axis=0, keepdims=True))
        alphas[h] = jnp.exp2(m_prev - m_new)
        pt = jnp.exp2(st - m_new)
        new_l[h] = alphas[h] * l_all[h] + jnp.sum(pt, axis=0, keepdims=True)
        new_m[h] = m_new
        probs[h] = pt.astype(BF16)

    def stage_values(h):
        kv = h // groups
        pvs[h] = _mm(vt_all[kv * HEAD_DIM:(kv + 1) * HEAD_DIM, :], probs.pop(h))

    for step in range(ATTN_HEADS + 2):
        if step < ATTN_HEADS:
            stage_scores(step)
        if 0 <= step - 1 < ATTN_HEADS:
            stage_softmax(step - 1)
        if 0 <= step - 2 < ATTN_HEADS:
            stage_values(step - 2)
    for h in range(ATTN_HEADS):
        acc_ref[h] = alphas[h] * acc_ref[h] + pvs[h]
        m_ref[h] = new_m[h]
        l_ref[h] = new_l[h]

    @pl.when(ki == nk - 1)
    def _():
        out = [acc_ref[h] / l_ref[h] for h in range(ATTN_HEADS)]
        o_ref[...] = jnp.concatenate(out, axis=0).astype(o_ref.dtype)


def _attention(qt, k, vt, seq_len, tq, tk):
    b, s, _ = k.shape
    nq = seq_len // tq
    nk = s // tk
    return pl.pallas_call(
        functools.partial(_attn_body, nk=nk),
        grid=(b, nq, nk),
        in_specs=[pl.BlockSpec((ATTN_WIDTH, tq), lambda bi, qi, ki: (0, bi * nq + qi)),
                  pl.BlockSpec((1, tk, KV_WIDTH), lambda bi, qi, ki: (bi, ki, 0)),
                  pl.BlockSpec((1, KV_WIDTH, tk), lambda bi, qi, ki: (bi, 0, ki))],
        out_specs=pl.BlockSpec((ATTN_WIDTH, tq), lambda bi, qi, ki: (0, bi * nq + qi)),
        out_shape=jax.ShapeDtypeStruct((ATTN_WIDTH, b * seq_len), BF16),
        scratch_shapes=[pltpu.VMEM((ATTN_HEADS, 1, tq), F32),
                        pltpu.VMEM((ATTN_HEADS, 1, tq), F32),
                        pltpu.VMEM((ATTN_HEADS, HEAD_DIM, tq), F32)],
        compiler_params=_cparams("parallel", "parallel", "arbitrary"),
        name="attention",
    )(qt, k, vt)


def _conv_body(first_ref, last_ref, cur_ref, prev_ref, next_ref, w_ref, b_ref, o_ref):
    i = pl.program_id(0)
    x = cur_ref[...]
    tm = x.shape[0]
    prev = jnp.where(first_ref[i] == 1, 0.0, prev_ref[...])
    nxt = jnp.where(last_ref[i] == 1, 0.0, next_ref[...])
    row8 = lax.broadcasted_iota(jnp.int32, prev.shape, 0)
    w = w_ref[...]
    half = SSD_CONV // 2
    acc = b_ref[...] + w[half:half + 1] * x
    for s in range(1, half + 1):
        rolled = pltpu.roll(x, s, 0)
        top = jnp.where(row8 < s, pltpu.roll(prev, s, 0), rolled[:8])
        acc = acc + w[half - s:half - s + 1] * jnp.concatenate([top, rolled[8:]], axis=0)
        rolled = pltpu.roll(x, tm - s, 0)
        bot = jnp.where(row8 >= 8 - s, pltpu.roll(nxt, 8 - s, 0), rolled[tm - 8:])
        acc = acc + w[half + s:half + s + 1] * jnp.concatenate([rolled[:tm - 8], bot], axis=0)
    o_ref[...] = acc * _sigmoid(acc)


def _ssd_conv(first, last, proj, conv_w, conv_b):
    n_tok = proj.shape[0]
    tm = TOKEN_TILE
    sub = tm // 8
    col = COL_XBC // XBC_WIDTH
    w8 = jnp.concatenate([conv_w, jnp.zeros((8 - SSD_CONV, XBC_WIDTH), F32)], axis=0)
    grid_spec = pltpu.PrefetchScalarGridSpec(
        num_scalar_prefetch=2,
        grid=(n_tok // tm,),
        in_specs=[pl.BlockSpec((tm, XBC_WIDTH), lambda i, f, l: (i, col)),
                  pl.BlockSpec((8, XBC_WIDTH), lambda i, f, l: (jnp.maximum(i * sub - 1, 0), col)),
                  pl.BlockSpec((8, XBC_WIDTH),
                               lambda i, f, l: (jnp.minimum((i + 1) * sub, n_tok // 8 - 1), col)),
                  pl.BlockSpec((8, XBC_WIDTH), lambda i, f, l: (0, 0)),
                  pl.BlockSpec((1, XBC_WIDTH), lambda i, f, l: (0, 0))],
        out_specs=pl.BlockSpec((tm, XBC_WIDTH), lambda i, f, l: (i, 0)))
    return pl.pallas_call(
        _conv_body, grid_spec=grid_spec,
        out_shape=jax.ShapeDtypeStruct((n_tok, XBC_WIDTH), F32),
        compiler_params=_cparams("parallel"),
        name="ssd_conv",
    )(first, last, proj, proj, proj, w8, conv_b.reshape(1, XBC_WIDTH))


def _softplus(x):
    return jnp.maximum(x, 0.0) + jnp.log1p(jnp.exp(-jnp.abs(x)))


def _ssd_body(*refs, nc, has_init):
    if has_init:
        (xs_ref, bm_ref, cm_ref, dt_ref, z_ref, dtb_ref, an_ref, drow_ref, nw_ref, h0_ref,
         y_ref, fin_ref, y_scr, h_scr) = refs
    else:
        (xs_ref, bm_ref, cm_ref, dt_ref, z_ref, dtb_ref, an_ref, drow_ref, nw_ref,
         y_ref, fin_ref, y_scr, h_scr) = refs
    sweep_id = pl.program_id(1)
    c = pl.program_id(2)
    chunk_len = SSD_CHUNK
    rep = SSD_HEADS // SSD_GROUPS

    @pl.when(c == 0)
    def _():
        if has_init:
            h_scr[...] = h0_ref[0, 0]
        else:
            h_scr[...] = jnp.zeros(h_scr.shape, F32)

    def sweep(direction):
        chunk = c if direction == 0 else nc - 1 - c
        row0 = pl.multiple_of(chunk * chunk_len, chunk_len)
        xs = xs_ref[...]
        bm = bm_ref[...]
        cm = cm_ref[...]
        dt_all = _softplus(dt_ref[...] + dtb_ref[...])
        la_all = dt_all * an_ref[...]
        ti = lax.broadcasted_iota(jnp.int32, (chunk_len, chunk_len), 0)
        si = lax.broadcasted_iota(jnp.int32, (chunk_len, chunk_len), 1)
        before_eq = (si <= ti) if direction == 0 else (si >= ti)
        cs_all = _mm(before_eq.astype(F32), la_all, HI)
        cs_t = cs_all.T
        last = chunk_len - 1 if direction == 0 else 0
        b_g = [bm[:, g * SSD_STATE:(g + 1) * SSD_STATE].astype(BF16) for g in range(SSD_GROUPS)]
        c_g = [cm[:, g * SSD_STATE:(g + 1) * SSD_STATE].astype(BF16) for g in range(SSD_GROUPS)]
        cb = [_nt(c_g[g], b_g[g]) for g in range(SSD_GROUPS)]
        states = [h_scr[h] for h in range(SSD_HEADS)]
        y_off = [_nt(c_g[h // rep], states[h].astype(BF16)) for h in range(SSD_HEADS)]
        cs_c, xd, scores = [], [], []
        for h in range(SSD_HEADS):
            j = direction * SSD_HEADS + h
            cs_c.append(cs_all[:, j:j + 1])
            decay = jnp.exp(jnp.where(before_eq, cs_c[h] - cs_t[j:j + 1, :], -jnp.inf))
            xd.append(xs[:, h * HEAD_DIM:(h + 1) * HEAD_DIM] * dt_all[:, j:j + 1])
            scores.append((cb[h // rep] * decay).astype(BF16))
        y_diag = [_mm(scores[h], xd[h].astype(BF16)) for h in range(SSD_HEADS)]
        heads, tots, to_state = [], [], []
        for h in range(SSD_HEADS):
            j = direction * SSD_HEADS + h
            tots.append(cs_all[last:last + 1, j:j + 1])
            to_end = jnp.exp(tots[h] - cs_c[h])
            to_state.append(_tn((xd[h] * to_end).astype(BF16), b_g[h // rep]))
            heads.append(y_diag[h] + y_off[h] * jnp.exp(cs_c[h]))
        for h in range(SSD_HEADS):
            h_scr[h] = jnp.exp(tots[h]) * states[h] + to_state[h]
        y_dir = jnp.concatenate(heads, axis=-1)
        if direction == 0:
            y_scr[pl.ds(row0, chunk_len), :] = y_dir + drow_ref[...] * xs
        else:
            y = y_scr[pl.ds(row0, chunk_len), :] + y_dir
            z = z_ref[...]
            y = y * (z * _sigmoid(z))
            y_ref[...] = _rms(y, nw_ref[...]).astype(y_ref.dtype)

    @pl.when(sweep_id == 0)
    def _():
        sweep(0)

    @pl.when(sweep_id == 1)
    def _():
        sweep(1)

    @pl.when(c == nc - 1)
    def _():
        fin_ref[0, 0] = h_scr[...]


def _ssd(proj, xbc, tile_off, batch, seq_len, dt_bias, a_log, d_skip, norm_w, init):
    chunk_len = SSD_CHUNK
    nc = seq_len // chunk_len
    base = tile_off * (TOKEN_TILE // chunk_len)
    has_init = init is not None

    def chunk_of(s, c):
        return c + s * (nc - 1 - 2 * c)

    def row(b, s, c):
        return base + b * nc + chunk_of(s, c)

    def z_row(b, s, c):
        return base + b * nc + s * (nc - 1 - c) + (1 - s) * (nc - 1)

    def y_row(b, s, c):
        return b * nc + s * (nc - 1 - c) + (1 - s) * (nc - 1)

    dtb = jnp.zeros((1, LANES), F32).at[0, :2 * SSD_HEADS].set(dt_bias.reshape(-1))
    a_neg = jnp.zeros((1, LANES), F32).at[0, :2 * SSD_HEADS].set(-jnp.exp(a_log.reshape(-1)))
    d_row = jnp.repeat(d_skip, HEAD_DIM).reshape(1, SSD_WIDTH)
    xs_blk = COL_XBC // XBC_WIDTH * (XBC_WIDTH // LANES)
    in_specs = [
        pl.BlockSpec((chunk_len, SSD_WIDTH), lambda b, s, c: (row(b, s, c), 0)),
        pl.BlockSpec((chunk_len, LANES), lambda b, s, c: (row(b, s, c), SSD_WIDTH // LANES)),
        pl.BlockSpec((chunk_len, LANES), lambda b, s, c: (row(b, s, c), SSD_WIDTH // LANES + 1)),
        pl.BlockSpec((chunk_len, LANES), lambda b, s, c: (row(b, s, c), COL_DT // LANES)),
        pl.BlockSpec((chunk_len, SSD_WIDTH), lambda b, s, c: (z_row(b, s, c), COL_Z // SSD_WIDTH)),
        pl.BlockSpec((1, LANES), lambda b, s, c: (0, 0)),
        pl.BlockSpec((1, LANES), lambda b, s, c: (0, 0)),
        pl.BlockSpec((1, SSD_WIDTH), lambda b, s, c: (0, 0)),
        pl.BlockSpec((1, SSD_WIDTH), lambda b, s, c: (0, 0)),
    ]
    del xs_blk
    args = [xbc, xbc, xbc, proj, proj, dtb, a_neg, d_row, norm_w.reshape(1, SSD_WIDTH)]
    state_block = (1, 1, SSD_HEADS, HEAD_DIM, SSD_STATE)
    if has_init:
        in_specs.append(pl.BlockSpec(state_block, lambda b, s, c: (b, s, 0, 0, 0)))
        args.append(init)
    return pl.pallas_call(
        functools.partial(_ssd_body, nc=nc, has_init=has_init),
        grid=(batch, 2, nc),
        in_specs=in_specs,
        out_specs=[pl.BlockSpec((chunk_len, SSD_WIDTH), lambda b, s, c: (y_row(b, s, c), 0)),
                   pl.BlockSpec(state_block, lambda b, s, c: (b, s, 0, 0, 0))],
        out_shape=[jax.ShapeDtypeStruct((batch * seq_len, SSD_WIDTH), BF16),
                   jax.ShapeDtypeStruct((batch, 2, SSD_HEADS, HEAD_DIM, SSD_STATE), F32)],
        scratch_shapes=[pltpu.VMEM((seq_len, SSD_WIDTH), F32),
                        pltpu.VMEM((SSD_HEADS, HEAD_DIM, SSD_STATE), F32)],
        compiler_params=_cparams("parallel", "arbitrary", "arbitrary"),
        name="ssd",
    )(*args)


def _rwkv_body(*refs, nc, has_init):
    if has_init:
        (r_ref, k_ref, v_ref, walo_ref, glo_ref, w0_ref, wup_ref, a0_ref, aup_ref, gup_ref,
         kk_ref, ka_ref, rk_ref, lnw_ref, lnb_ref, s0_ref, y_ref, fin_ref, y_scr, s_scr) = refs
    else:
        (r_ref, k_ref, v_ref, walo_ref, glo_ref, w0_ref, wup_ref, a0_ref, aup_ref, gup_ref,
         kk_ref, ka_ref, rk_ref, lnw_ref, lnb_ref, y_ref, fin_ref, y_scr, s_scr) = refs
    sweep_id = pl.program_id(1)
    c = pl.program_id(2)
    cl = RWKV_CHUNK
    n = HEAD_DIM
    lora = walo_ref.shape[-1] // 2

    @pl.when(c == 0)
    def _():
        if has_init:
            s_scr[...] = s0_ref[0, 0]
        else:
            s_scr[...] = jnp.zeros(s_scr.shape, F32)

    def sweep(direction):
        chunk = c if direction == 0 else nc - 1 - c
        row0 = pl.multiple_of(chunk * cl, cl)
        r = r_ref[...]
        k = k_ref[...]
        v = v_ref[...]
        walo = walo_ref[...]
        zw = w0_ref[direction:direction + 1] + _mm(jnp.tanh(walo[:, :lora]), wup_ref[direction], HI)
        lw = -math.exp(-0.5) * _sigmoid(zw)
        a = _sigmoid(a0_ref[direction:direction + 1] + _mm(walo[:, lora:], aup_ref[direction], HI))
        kd = k * (1.0 + (a - 1.0) * ka_ref[...])
        kk_raw = k * kk_ref[...]
        ti = lax.broadcasted_iota(jnp.int32, (cl, cl), 0)
        si = lax.broadcasted_iota(jnp.int32, (cl, cl), 1)
        before_eq = (si <= ti) if direction == 0 else (si >= ti)
        before = (si < ti) if direction == 0 else (si > ti)
        lp_all = _mm(before_eq.astype(F32), lw, HI)
        last = cl - 1 if direction == 0 else 0
        outs = []
        for h in range(RWKV_HEADS):
            sl = slice(h * n, (h + 1) * n)
            kk = kk_raw[:, sl]
            kk = kk * lax.rsqrt(jnp.sum(kk * kk, axis=-1, keepdims=True) + 1e-12)
            lp = lp_all[:, sl]
            lw_h = lw[:, sl]
            dec_in = jnp.exp(lp)
            dec_out = jnp.exp(-lp)
            kap = kk * jnp.exp(lp - lw_h)
            rt = r[:, sl] * dec_in
            kt = kd[:, sl] * dec_out
            bt = a[:, sl] * kk * dec_out
            v_h = v[:, sl]
            state = s_scr[h]
            g = _nt(jnp.concatenate([kap, rt], axis=0), jnp.concatenate([kt, bt], axis=0), HI)
            n_mat = jnp.where(before, g[:cl, :cl], 0.0)
            l_mat = jnp.where(before, g[:cl, cl:], 0.0)
            rk_mat = jnp.where(before_eq, g[cl:, :cl], 0.0)
            rb_mat = jnp.where(before_eq, g[cl:, cl:], 0.0)
            from_state = _nt(jnp.concatenate([kap, rt], axis=0), state, HI)
            x = from_state[:cl] + _mm(n_mat, v_h, HI)
            x = x - _mm(l_mat, x, HI)
            pw = l_mat
            for _ in range(int(math.log2(cl)) - 1):
                pw = _mm(pw, pw, HI)
                x = x + _mm(pw, x, HI)
            u = x
            y_h = from_state[cl:] + _mm(rk_mat, v_h, HI) - _mm(rb_mat, u, HI)
            lp_last = lp[last:last + 1]
            to_end = jnp.exp(lp_last - lp)
            upd = _tn(jnp.concatenate([v_h, -u], axis=0),
                      jnp.concatenate([kd[:, sl] * to_end, a[:, sl] * kk * to_end], axis=0), HI)
            s_scr[h] = state * jnp.exp(lp_last) + upd
            outs.append(y_h)
        y_dir = jnp.concatenate(outs, axis=-1)
        if direction == 0:
            y_scr[pl.ds(row0, cl), :] = y_dir
        else:
            y = y_scr[pl.ds(row0, cl), :] + y_dir
            gate = _mm(_sigmoid(glo_ref[...]), gup_ref[...], HI)
            rkw = r * k * rk_ref[...]
            normed = []
            for h in range(RWKV_HEADS):
                sl = slice(h * n, (h + 1) * n)
                y_h = y[:, sl]
                mu = jnp.mean(y_h, axis=-1, keepdims=True)
                var = jnp.mean(jnp.square(y_h - mu), axis=-1, keepdims=True)
                bonus = jnp.sum(rkw[:, sl], axis=-1, keepdims=True) * v[:, sl]
                normed.append(((y_h - mu) * lax.rsqrt(var + GN_EPS), bonus))
            y_n = jnp.concatenate([t[0] for t in normed], axis=-1)
            bonus = jnp.concatenate([t[1] for t in normed], axis=-1)
            out = (y_n * lnw_ref[...] + lnb_ref[...] + bonus) * gate
            y_ref[...] = out.astype(y_ref.dtype)

    @pl.when(sweep_id == 0)
    def _():
        sweep(0)

    @pl.when(sweep_id == 1)
    def _():
        sweep(1)

    @pl.when(c == nc - 1)
    def _():
        fin_ref[0, 0] = s_scr[...]


def _rwkv(proj, tile_off, batch, seq_len, lp, init):
    cl = RWKV_CHUNK
    nc = seq_len // cl
    base = tile_off * (TOKEN_TILE // cl)
    has_init = init is not None
    w = RWKV_WIDTH

    def row(b, s, c):
        return base + b * nc + c + s * (nc - 1 - 2 * c)

    def tail_row(b, s, c):
        return base + b * nc + s * (nc - 1 - c) + (1 - s) * (nc - 1)

    def y_row(b, s, c):
        return b * nc + s * (nc - 1 - c) + (1 - s) * (nc - 1)

    def const(shape):
        return pl.BlockSpec(shape, lambda b, s, c: (0,) * len(shape))

    in_specs = [
        pl.BlockSpec((cl, w), lambda b, s, c: (row(b, s, c), COL_R // w)),
        pl.BlockSpec((cl, w), lambda b, s, c: (row(b, s, c), COL_KR // w)),
        pl.BlockSpec((cl, w), lambda b, s, c: (row(b, s, c), COL_VR // w)),
        pl.BlockSpec((cl, LANES), lambda b, s, c: (row(b, s, c), COL_WALO // LANES)),
        pl.BlockSpec((cl, LANES), lambda b, s, c: (tail_row(b, s, c), COL_GLO // LANES)),
        const((2, w)), const((2, LANES // 2, w)), const((2, w)), const((2, LANES // 2, w)),
        const((LANES, w)), const((1, w)), const((1, w)), const((1, w)), const((1, w)), const((1, w)),
    ]
    args = [proj, proj, proj, proj, proj, lp['rwkv_w0'], lp['rwkv_w_up'], lp['rwkv_a0'],
            lp['rwkv_a_up'], lp['rwkv_g_up'], lp['rwkv_k_k'].reshape(1, w),
            lp['rwkv_k_a'].reshape(1, w), lp['rwkv_r_k'].reshape(1, w),
            lp['rwkv_ln_w'].reshape(1, w), lp['rwkv_ln_b'].reshape(1, w)]
    state_block = (1, 1, RWKV_HEADS, HEAD_DIM, HEAD_DIM)
    if has_init:
        in_specs.append(pl.BlockSpec(state_block, lambda b, s, c: (b, s, 0, 0, 0)))
        args.append(init)
    return pl.pallas_call(
        functools.partial(_rwkv_body, nc=nc, has_init=has_init),
        grid=(batch, 2, nc),
        in_specs=in_specs,
        out_specs=[pl.BlockSpec((cl, w), lambda b, s, c: (y_row(b, s, c), 0)),
                   pl.BlockSpec(state_block, lambda b, s, c: (b, s, 0, 0, 0))],
        out_shape=[jax.ShapeDtypeStruct((batch * seq_len, w), BF16),
                   jax.ShapeDtypeStruct((batch, 2, RWKV_HEADS, HEAD_DIM, HEAD_DIM), F32)],
        scratch_shapes=[pltpu.VMEM((seq_len, w), F32),
                        pltpu.VMEM((RWKV_HEADS, HEAD_DIM, HEAD_DIM), F32)],
        compiler_params=_cparams("parallel", "arbitrary", "arbitrary"),
        name="rwkv",
    )(*args)


def _rwkv2_body(*refs, nc, has_init):
    fwd_refs, bwd_refs = refs[0:5], refs[5:10]
    (w0_ref, wup_ref, a0_ref, aup_ref, gup_ref, kk_ref, ka_ref, rk_ref, lnw_ref,
     lnb_ref) = refs[10:20]
    if has_init:
        s0_ref, y_ref, fin_ref, y_scr, s_scr = refs[20:]
    else:
        y_ref, fin_ref, y_scr, s_scr = refs[20:]
    c = pl.program_id(1)
    cl = RWKV_CHUNK
    n = HEAD_DIM
    lora = LANES // 2

    @pl.when(c == 0)
    def _():
        if has_init:
            s_scr[...] = s0_ref[0]
        else:
            s_scr[...] = jnp.zeros(s_scr.shape, F32)

    ti = lax.broadcasted_iota(jnp.int32, (cl, cl), 0)
    si = lax.broadcasted_iota(jnp.int32, (cl, cl), 1)
    lane2 = lax.broadcasted_iota(jnp.int32, (cl, 2 * cl), 1)
    left = lane2 < cl
    sign2 = jnp.where(left, 1.0, -1.0)
    t2 = lax.broadcasted_iota(jnp.int32, (cl, 2 * cl), 0)
    s2 = jnp.where(left, lane2, lane2 - cl)

    chains = []
    for direction, (r_ref, k_ref, v_ref, walo_ref, _) in enumerate((fwd_refs, bwd_refs)):
        r = r_ref[...]
        k = k_ref[...]
        v = v_ref[...]
        walo = walo_ref[...]
        zw = w0_ref[direction:direction + 1] + _mm(jnp.tanh(walo[:, :lora]), wup_ref[direction], HI)
        lw = -math.exp(-0.5) * _sigmoid(zw)
        a = _sigmoid(a0_ref[direction:direction + 1] + _mm(walo[:, lora:], aup_ref[direction], HI))
        kd = k * (1.0 + (a - 1.0) * ka_ref[...])
        kk_raw = k * kk_ref[...]
        before_eq = (si <= ti) if direction == 0 else (si >= ti)
        before = (si < ti) if direction == 0 else (si > ti)
        before_eq2 = (s2 <= t2) if direction == 0 else (s2 >= t2)
        lp_all = _mm(before_eq.astype(F32), lw, HI)
        last = cl - 1 if direction == 0 else 0
        lp_last = lp_all[last:last + 1]
        dec_in = jnp.exp(lp_all)
        dec_out = jnp.exp(-lp_all)
        dec_before = jnp.exp(lp_all - lw)
        to_end = jnp.exp(lp_last - lp_all)
        end_decay = jnp.exp(lp_last)
        rt_all = r * dec_in
        kt_all = kd * dec_out
        kend_all = kd * to_end
        for h in range(RWKV_HEADS):
            sl = slice(h * n, (h + 1) * n)
            kk = kk_raw[:, sl]
            kk = kk * lax.rsqrt(jnp.sum(kk * kk, axis=-1, keepdims=True) + 1e-12)
            akk = a[:, sl] * kk
            kap_rt = jnp.concatenate([kk * dec_before[:, sl], rt_all[:, sl]], axis=0).astype(BF16)
            kt_bt = jnp.concatenate([kt_all[:, sl], akk * dec_out[:, sl]], axis=0).astype(BF16)
            chains.append(dict(
                kap_rt=kap_rt, kt_bt=kt_bt, v=v[:, sl], state=s_scr[direction, h],
                before=before, before_eq2=before_eq2, end_decay=end_decay[:, sl],
                kend=jnp.concatenate([kend_all[:, sl], akk * to_end[:, sl]], axis=0).astype(BF16)))

    for ch in chains:
        ch['g'] = _nt(ch['kap_rt'], ch['kt_bt'])
    for ch in chains:
        ch['fs'] = _nt(ch['kap_rt'], ch['state'].astype(BF16))
    for ch in chains:
        g = ch.pop('g')
        ch['n_mat'] = jnp.where(ch['before'], g[:cl, :cl], 0.0).astype(BF16)
        ch['l_mat'] = jnp.where(ch['before'], g[:cl, cl:], 0.0)
        ch['rkb'] = (jnp.where(ch['before_eq2'], g[cl:], 0.0) * sign2).astype(BF16)
    for ch in chains:
        x0 = ch['fs'][:cl] + _mm(ch.pop('n_mat'), ch['v'].astype(BF16))
        ch['z'] = jnp.concatenate([-ch.pop('l_mat'), x0], axis=1)

    for _ in range(int(math.log2(cl))):
        for ch in chains:
            z = ch['z']
            pz = _mm(z[:, :cl].astype(BF16), z.astype(BF16))
            ch['z'] = jnp.where(left, pz, z + pz)

    ys = []
    for ch in chains:
        u = ch['z'][:, cl:]
        ys.append(ch['fs'][cl:] + _mm(ch['rkb'], jnp.concatenate([ch['v'], u], axis=0).astype(BF16)))
    for idx, ch in enumerate(chains):
        direction, h = divmod(idx, RWKV_HEADS)
        v_nu = jnp.concatenate([ch['v'], -ch['z'][:, cl:]], axis=0).astype(BF16)
        s_scr[direction, h] = ch['state'] * ch['end_decay'] + _tn(v_nu, ch['kend'])
    y_f = jnp.concatenate(ys[:RWKV_HEADS], axis=-1)
    y_b = jnp.concatenate(ys[RWKV_HEADS:], axis=-1)
    row_f = pl.multiple_of(c * cl, cl)
    row_b = pl.multiple_of((nc - 1 - c) * cl, cl)

    def finish(y, io_refs):
        r_ref, k_ref, v_ref, _, glo_ref = io_refs
        r = r_ref[...]
        v = v_ref[...]
        gate = _mm(_sigmoid(glo_ref[...]).astype(BF16), gup_ref[...].astype(BF16))
        rkw = r * k_ref[...] * rk_ref[...]
        y_n, bonus = [], []
        for h in range(RWKV_HEADS):
            sl = slice(h * n, (h + 1) * n)
            y_h = y[:, sl]
            mu = jnp.mean(y_h, axis=-1, keepdims=True)
            var = jnp.mean(jnp.square(y_h - mu), axis=-1, keepdims=True)
            y_n.append((y_h - mu) * lax.rsqrt(var + GN_EPS))
            bonus.append(jnp.sum(rkw[:, sl], axis=-1, keepdims=True) * v[:, sl])
        y_n = jnp.concatenate(y_n, axis=-1)
        bonus = jnp.concatenate(bonus, axis=-1)
        return ((y_n * lnw_ref[...] + lnb_ref[...] + bonus) * gate).astype(y_ref.dtype)

    @pl.when(c < nc // 2)
    def _():
        y_scr[pl.ds(row_f, cl), :] = y_f
        y_scr[pl.ds(row_b, cl), :] = y_b

    @pl.when(c >= nc // 2)
    def _():
        y_ref[pl.ds(row_f, cl), :] = finish(y_scr[pl.ds(row_f, cl), :] + y_f, fwd_refs)
        y_ref[pl.ds(row_b, cl), :] = finish(y_scr[pl.ds(row_b, cl), :] + y_b, bwd_refs)

    @pl.when(c == nc - 1)
    def _():
        fin_ref[0] = s_scr[...]


def _rwkv2(proj, tile_off, batch, seq_len, lp, init):
    cl = RWKV_CHUNK
    nc = seq_len // cl
    assert nc % 2 == 0
    base = tile_off * (TOKEN_TILE // cl)
    has_init = init is not None
    w = RWKV_WIDTH

    def fwd_row(b, c):
        return base + b * nc + c

    def bwd_row(b, c):
        return base + b * nc + nc - 1 - c

    def const(shape):
        return pl.BlockSpec(shape, lambda b, c: (0,) * len(shape))

    def chunk_specs(row):
        return [pl.BlockSpec((cl, w), lambda b, c: (row(b, c), COL_R // w)),
                pl.BlockSpec((cl, w), lambda b, c: (row(b, c), COL_KR // w)),
                pl.BlockSpec((cl, w), lambda b, c: (row(b, c), COL_VR // w)),
                pl.BlockSpec((cl, LANES), lambda b, c: (row(b, c), COL_WALO // LANES)),
                pl.BlockSpec((cl, LANES), lambda b, c: (row(b, c), COL_GLO // LANES))]

    in_specs = chunk_specs(fwd_row) + chunk_specs(bwd_row) + [
        const((2, w)), const((2, LANES // 2, w)), const((2, w)), const((2, LANES // 2, w)),
        const((LANES, w)), const((1, w)), const((1, w)), const((1, w)), const((1, w)), const((1, w))]
    args = [proj] * 10 + [lp['rwkv_w0'], lp['rwkv_w_up'], lp['rwkv_a0'],
                          lp['rwkv_a_up'], lp['rwkv_g_up'], lp['rwkv_k_k'].reshape(1, w),
                          lp['rwkv_k_a'].reshape(1, w), lp['rwkv_r_k'].reshape(1, w),
                          lp['rwkv_ln_w'].reshape(1, w), lp['rwkv_ln_b'].reshape(1, w)]
    state_block = (1, 2, RWKV_HEADS, HEAD_DIM, HEAD_DIM)
    if has_init:
        in_specs.append(pl.BlockSpec(state_block, lambda b, c: (b, 0, 0, 0, 0)))
        args.append(init)
    return pl.pallas_call(
        functools.partial(_rwkv2_body, nc=nc, has_init=has_init),
        grid=(batch, nc),
        in_specs=in_specs,
        out_specs=[pl.BlockSpec((seq_len, w), lambda b, c: (b, 0)),
                   pl.BlockSpec(state_block, lambda b, c: (b, 0, 0, 0, 0))],
        out_shape=[jax.ShapeDtypeStruct((batch * seq_len, w), BF16),
                   jax.ShapeDtypeStruct((batch, 2, RWKV_HEADS, HEAD_DIM, HEAD_DIM), F32)],
        scratch_shapes=[pltpu.VMEM((seq_len, w), F32),
                        pltpu.VMEM((2, RWKV_HEADS, HEAD_DIM, HEAD_DIM), F32)],
        compiler_params=_cparams("parallel", "arbitrary"),
        name="rwkv",
    )(*args)


def _out_proj_body(tmod_ref, attn_ref, ssd_ref, rwkv_ref, h_ref, mod_ref, nw_ref, w_ref, rw_ref,
                   rb_ref, h_out, u_out, lg_out):
    del tmod_ref
    mixed = (_tn(attn_ref[...], w_ref[:ATTN_WIDTH])
             + _mm(ssd_ref[...], w_ref[ATTN_WIDTH:ATTN_WIDTH + SSD_WIDTH])
             + _mm(rwkv_ref[...], w_ref[ATTN_WIDTH + SSD_WIDTH:]))
    mod = mod_ref[0]
    h = h_ref[...] + mod[2:3] * mixed
    h_out[...] = h
    u = _rms(h, nw_ref[...]) * (1.0 + mod[4:5]) + mod[3:4]
    u_out[...] = u
    lg_out[...] = _mm(u, rw_ref[...], HI) + rb_ref[...]


def _out_proj(tile_mod, attn, ssd, rwkv, h, mod, norm_w, w, router_w, router_b):
    n_tok, d = h.shape
    tm = TOKEN_TILE
    rw = jnp.zeros((d, LANES), F32).at[:, :N_EXPERTS].set(router_w)
    rb = jnp.zeros((1, LANES), F32).at[0, :N_EXPERTS].set(router_b)
    grid_spec = pltpu.PrefetchScalarGridSpec(
        num_scalar_prefetch=1,
        grid=(n_tok // tm,),
        in_specs=[pl.BlockSpec((ATTN_WIDTH, tm), lambda i, t: (0, i)),
                  pl.BlockSpec((tm, SSD_WIDTH), lambda i, t: (i, 0)),
                  pl.BlockSpec((tm, RWKV_WIDTH), lambda i, t: (i, 0)),
                  pl.BlockSpec((tm, d), lambda i, t: (i, 0)),
                  pl.BlockSpec((1, 6, d), lambda i, t: (t[i], 0, 0)),
                  pl.BlockSpec((1, d), lambda i, t: (0, 0)),
                  pl.BlockSpec((d, d), lambda i, t: (0, 0)),
                  pl.BlockSpec((d, LANES), lambda i, t: (0, 0)),
                  pl.BlockSpec((1, LANES), lambda i, t: (0, 0))],
        out_specs=[pl.BlockSpec((tm, d), lambda i, t: (i, 0)),
                   pl.BlockSpec((tm, d), lambda i, t: (i, 0)),
                   pl.BlockSpec((tm, LANES), lambda i, t: (i, 0))])
    return pl.pallas_call(
        _out_proj_body, grid_spec=grid_spec,
        out_shape=[jax.ShapeDtypeStruct((n_tok, d), F32),
                   jax.ShapeDtypeStruct((n_tok, d), F32),
                   jax.ShapeDtypeStruct((n_tok, LANES), F32)],
        compiler_params=_cparams("parallel"),
        name="out_proj",
    )(tile_mod, attn, ssd, rwkv, h, mod, norm_w.reshape(1, d), w, rw, rb)


def _dispatch_body(pos_ref, u_ref, init_hbm, o_hbm, pos_smem, isem, sem):
    del init_hbm
    tm = u_ref.shape[0]
    cp = pltpu.make_async_copy(pos_ref.at[0, 0], pos_smem, isem)
    cp.start()
    cp.wait()

    def issue(r, carry):
        for kk in range(TOP_K):
            pltpu.make_async_copy(u_ref.at[pl.ds(r, 1)],
                                  o_hbm.at[pl.ds(pos_smem[kk * tm + r], 1)], sem).start()
        return carry

    lax.fori_loop(0, tm, issue, 0, unroll=2)
    for _ in range(TOP_K):
        pltpu.make_async_copy(u_ref, o_hbm.at[pl.ds(0, tm)], sem).wait()


def _moe_dispatch(u, pos_t, n_rows):
    n_tok, d = u.shape
    tm = DISPATCH_TILE
    steps = n_tok // tm
    return pl.pallas_call(
        _dispatch_body,
        grid=(steps,),
        in_specs=[pl.BlockSpec((1, 1, TOP_K * tm), lambda i: (i, 0, 0)),
                  pl.BlockSpec((tm, d), lambda i: (i, 0)),
                  pl.BlockSpec(memory_space=pl.ANY)],
        out_specs=pl.BlockSpec(memory_space=pl.ANY),
        out_shape=jax.ShapeDtypeStruct((n_rows, d), u.dtype),
        scratch_shapes=[pltpu.SMEM((TOP_K * tm,), jnp.int32), pltpu.SemaphoreType.DMA,
                        pltpu.SemaphoreType.DMA],
        input_output_aliases={2: 0},
        compiler_params=_cparams("arbitrary"),
        name="moe_dispatch",
    )(pos_t, u, jnp.zeros((n_rows, d), u.dtype))


def _slot_major(pos, tm):
    steps = pos.shape[0] // tm
    return pos.reshape(steps, tm, TOP_K).transpose(0, 2, 1).reshape(steps, 1, TOP_K * tm)


def _moe_body(be_ref, nv_ref, x_ref, wgu_ref, bgu_ref, wdn_ref, bdn_ref, o_ref):
    del be_ref
    i = pl.program_id(0)

    @pl.when(i < nv_ref[0])
    def _():
        gu = _mm(x_ref[...].astype(BF16), wgu_ref[0]) + bgu_ref[0]
        gate = jnp.minimum(gu[:, :D_EXPERT], SWIGLU_LIMIT)
        up = jnp.clip(gu[:, D_EXPERT:], -SWIGLU_LIMIT, SWIGLU_LIMIT)
        act = (up + 1.0) * gate * _sigmoid(SWIGLU_ALPHA * gate)
        o_ref[...] = _mm(act.astype(BF16), wdn_ref[0]) + bdn_ref[0]

    @pl.when(i >= nv_ref[0])
    def _():
        o_ref[...] = jnp.zeros(o_ref.shape, o_ref.dtype)


def _moe_experts(block_expert, n_valid, x_rows, w_gu, b_gu, w_dn, b_dn):
    n_rows, d = x_rows.shape
    rows = MOE_ROWS
    grid_spec = pltpu.PrefetchScalarGridSpec(
        num_scalar_prefetch=2,
        grid=(n_rows // rows,),
        in_specs=[pl.BlockSpec((rows, d), lambda i, be, nv: (i, 0)),
                  pl.BlockSpec((1, d, 2 * D_EXPERT), lambda i, be, nv: (be[i], 0, 0)),
                  pl.BlockSpec((1, 1, 2 * D_EXPERT), lambda i, be, nv: (be[i], 0, 0)),
                  pl.BlockSpec((1, D_EXPERT, d), lambda i, be, nv: (be[i], 0, 0)),
                  pl.BlockSpec((1, 1, d), lambda i, be, nv: (be[i], 0, 0))],
        out_specs=pl.BlockSpec((rows, d), lambda i, be, nv: (i, 0)))
    return pl.pallas_call(
        _moe_body, grid_spec=grid_spec,
        out_shape=jax.ShapeDtypeStruct((n_rows, d), F32),
        compiler_params=_cparams("arbitrary"),
        name="moe_experts",
    )(block_expert, n_valid, x_rows, w_gu, b_gu.reshape(N_EXPERTS, 1, -1), w_dn,
      b_dn.reshape(N_EXPERTS, 1, -1))


def _combine_body(tmod_ref, pos_ref, gates_ref, h_ref, mod_ref, fw_ref, y_hbm, o_ref, pos_smem,
                  buf, isem, sem, *, final):
    del tmod_ref
    tm = h_ref.shape[0]
    cp = pltpu.make_async_copy(pos_ref.at[0, 0], pos_smem, isem)
    cp.start()
    cp.wait()

    def issue(r, carry):
        pltpu.make_async_copy(y_hbm.at[pl.ds(pos_smem[r], 1)], buf.at[pl.ds(r, 1)], sem).start()
        return carry

    lax.fori_loop(0, TOP_K * tm, issue, 0, unroll=8)
    pltpu.make_async_copy(y_hbm.at[pl.ds(0, TOP_K * tm)], buf, sem).wait()
    gates = gates_ref[...]
    y = gates[:, 0:1] * buf[0:tm]
    for kk in range(1, TOP_K):
        y = y + gates[:, kk:kk + 1] * buf[kk * tm:(kk + 1) * tm]
    h = h_ref[...] + mod_ref[0][5:6] * y
    if final:
        o_ref[...] = _rms(h, fw_ref[...])
    else:
        o_ref[...] = h


def _moe_combine(tile_mod, pos, gates, h, mod, final_w, y_rows, final):
    n_tok, d = h.shape
    tm = COMBINE_TILE
    steps = n_tok // tm
    ratio = TOKEN_TILE // tm
    pos_t = _slot_major(pos, tm)
    grid_spec = pltpu.PrefetchScalarGridSpec(
        num_scalar_prefetch=1,
        grid=(steps,),
        in_specs=[pl.BlockSpec((1, 1, TOP_K * tm), lambda i, t: (i, 0, 0)),
                  pl.BlockSpec((tm, TOP_K), lambda i, t: (i, 0)),
                  pl.BlockSpec((tm, d), lambda i, t: (i, 0)),
                  pl.BlockSpec((1, 6, d), lambda i, t: (t[i // ratio], 0, 0)),
                  pl.BlockSpec((1, d), lambda i, t: (0, 0)),
                  pl.BlockSpec(memory_space=pl.ANY)],
        out_specs=pl.BlockSpec((tm, d), lambda i, t: (i, 0)),
        scratch_shapes=[pltpu.SMEM((TOP_K * tm,), jnp.int32),
                        pltpu.VMEM((TOP_K * tm, d), F32),
                        pltpu.SemaphoreType.DMA, pltpu.SemaphoreType.DMA])
    return pl.pallas_call(
        functools.partial(_combine_body, final=final), grid_spec=grid_spec,
        out_shape=jax.ShapeDtypeStruct((n_tok, d), F32),
        compiler_params=_cparams("arbitrary"),
        name="moe_combine",
    )(tile_mod, pos_t, gates, h, mod, final_w.reshape(1, d), y_rows)


def _route(logits):
    n_tok = logits.shape[0]
    rows = MOE_ROWS
    n_blocks = n_tok * TOP_K // rows + N_EXPERTS
    top_logit, top_idx = lax.top_k(logits, TOP_K)
    gates = jax.nn.softmax(top_logit, axis=-1)
    hit = top_idx[:, :, None] == jnp.arange(N_EXPERTS, dtype=top_idx.dtype)[None, None, :]
    per_tok = jnp.sum(hit, axis=1, dtype=jnp.int32)
    before = jnp.cumsum(per_tok, axis=0) - per_tok
    counts = before[-1] + per_tok[-1]
    padded = (counts + rows - 1) // rows * rows
    padded_end = jnp.cumsum(padded)
    padded_start = padded_end - padded
    pos = jnp.sum(jnp.where(hit, (before + padded_start[None, :])[:, None, :], 0), axis=-1)
    block_start = jnp.arange(n_blocks, dtype=jnp.int32) * rows
    block_expert = jnp.minimum(jnp.searchsorted(padded_end, block_start, side='right'),
                               N_EXPERTS - 1).astype(jnp.int32)
    n_valid = (padded_end[-1] // rows).astype(jnp.int32).reshape(1)
    return pos.astype(jnp.int32), gates, block_expert, n_valid, n_blocks * rows


def _permute_w_in(w_in):
    def cols(a, b):
        return w_in[:, a:b]
    q, k, v, z = cols(0, 384), cols(384, 512), cols(512, 640), cols(640, 1024)
    xbc, dt = cols(1024, 1664), cols(1664, 1676)
    r, kr, vr = cols(1676, 1932), cols(1932, 2188), cols(2188, 2444)
    walo, glo = cols(2444, 2572), cols(2572, 2700)
    dt = jnp.pad(dt, ((0, 0), (0, LANES - dt.shape[1])))
    return jnp.concatenate([q, z, r, kr, k, v, dt, walo, glo, xbc, vr], axis=1).astype(BF16)


def kernel(x_prompt, x_sample, cache_attn_k, cache_attn_v, state_ssd, state_rwkv, c, c_ctx,
           norm1_w, norm2_w, ada_w, ada_b, w_in, q_norm_w, k_norm_w,
           ssd_conv_w, ssd_conv_b, ssd_dt_bias, ssd_a_log, ssd_d, ssd_norm_w,
           rwkv_w0, rwkv_w_up, rwkv_a0, rwkv_a_up, rwkv_g_up, rwkv_k_k, rwkv_k_a, rwkv_r_k,
           rwkv_ln_w, rwkv_ln_b, w_out, router_w, router_b,
           moe_w_gate_up, moe_b_gate_up, moe_w_down, moe_b_down, final_norm_w):
    batch, seq, d = x_prompt.shape
    dec_batch, dec_seq, _ = x_sample.shape
    depth = norm1_w.shape[0]
    past = cache_attn_k.shape[2]
    n_ctx, n_lat = batch * seq, dec_batch * dec_seq
    tm = TOKEN_TILE
    ctx_tiles, lat_tiles = n_ctx // tm, n_lat // tm

    tile_mod = np.concatenate([np.zeros(ctx_tiles, np.int32),
                               1 + np.arange(lat_tiles, dtype=np.int32) // (dec_seq // tm)])
    seq_tiles = np.concatenate([np.full(ctx_tiles, seq // tm), np.full(lat_tiles, dec_seq // tm)])
    tile_in_seq = np.concatenate([np.arange(ctx_tiles) % (seq // tm),
                                  np.arange(lat_tiles) % (dec_seq // tm)])
    first = jnp.asarray((tile_in_seq == 0).astype(np.int32))
    last = jnp.asarray((tile_in_seq == seq_tiles - 1).astype(np.int32))
    tile_mod = jnp.asarray(tile_mod)

    cond = jnp.concatenate([c_ctx[None, :], c], axis=0)
    pad_rows = -cond.shape[0] % 8
    cond = jnp.pad(cond, ((0, pad_rows), (0, 0)))
    mod_all = _ada(cond, ada_w, ada_b).reshape(depth, cond.shape[0], 6, d)

    rope_tabs = _rope_tables(dec_seq)
    h = jnp.concatenate([x_prompt.reshape(n_ctx, d), x_sample.reshape(n_lat, d)], axis=0)
    ks, vs, ssd_states, rwkv_states = [], [], [], []
    for l in range(depth):
        lp = dict(rwkv_w0=rwkv_w0[l], rwkv_w_up=rwkv_w_up[l], rwkv_a0=rwkv_a0[l],
                  rwkv_a_up=rwkv_a_up[l], rwkv_g_up=rwkv_g_up[l], rwkv_k_k=rwkv_k_k[l],
                  rwkv_k_a=rwkv_k_a[l], rwkv_r_k=rwkv_r_k[l], rwkv_ln_w=rwkv_ln_w[l],
                  rwkv_ln_b=rwkv_ln_b[l])
        mod = mod_all[l]
        proj = _in_proj(tile_mod, h, mod, norm1_w[l], _permute_w_in(w_in[l]))

        q_c, k_c, v_c, k_norm = _attn_prep(proj, 0, n_ctx, q_norm_w[l], k_norm_w[l], None, 1)
        q_l, k_l, v_l = _attn_prep(proj, ctx_tiles, n_lat, q_norm_w[l], k_norm_w[l], rope_tabs,
                                   dec_seq // tm)
        v_c = v_c.reshape(KV_WIDTH, batch, seq).transpose(1, 0, 2)
        attn_c = _attention(q_c, k_c.reshape(batch, seq, -1), v_c, seq, seq, seq)
        k_full = jnp.concatenate([cache_attn_k[:, l].reshape(dec_batch, past, -1).astype(BF16),
                                  k_l.reshape(dec_batch, dec_seq, -1)], axis=1)
        v_past = cache_attn_v[:, l].reshape(dec_batch, past, -1).astype(BF16).transpose(0, 2, 1)
        v_full = jnp.concatenate(
            [v_past, v_l.reshape(KV_WIDTH, dec_batch, dec_seq).transpose(1, 0, 2)], axis=2)
        attn_l = _attention(q_l, k_full, v_full, dec_seq, 256, 512)
        attn = jnp.concatenate([attn_c, attn_l], axis=1)
        ks.append(k_norm.reshape(batch, seq, ATTN_KV_HEADS, HEAD_DIM))
        vs.append(proj[:n_ctx, COL_V:COL_V + KV_WIDTH].reshape(batch, seq, ATTN_KV_HEADS, HEAD_DIM))

        xbc = _ssd_conv(first, last, proj, ssd_conv_w[l], ssd_conv_b[l])
        ssd_c, ssd_fin = _ssd(proj, xbc, 0, batch, seq, ssd_dt_bias[l], ssd_a_log[l], ssd_d[l],
                              ssd_norm_w[l], None)
        ssd_l, _ = _ssd(proj, xbc, ctx_tiles, dec_batch, dec_seq, ssd_dt_bias[l], ssd_a_log[l],
                        ssd_d[l], ssd_norm_w[l], state_ssd[:, l])
        ssd = jnp.concatenate([ssd_c, ssd_l], axis=0)
        ssd_states.append(ssd_fin)

        rwkv_c, rwkv_fin = _rwkv2(proj, 0, batch, seq, lp, None)
        rwkv_l, _ = _rwkv2(proj, ctx_tiles, dec_batch, dec_seq, lp, state_rwkv[:, l])
        rwkv = jnp.concatenate([rwkv_c, rwkv_l], axis=0)
        rwkv_states.append(rwkv_fin)

        h, u, logits = _out_proj(tile_mod, attn, ssd, rwkv, h, mod, norm2_w[l],
                                 w_out[l].astype(BF16), router_w[l], router_b[l])

        pos, gates, block_expert, n_valid, n_rows = _route(logits[:, :N_EXPERTS])
        x_rows = _moe_dispatch(u, _slot_major(pos, DISPATCH_TILE), n_rows)
        y_rows = _moe_experts(block_expert, n_valid, x_rows, moe_w_gate_up[l].astype(BF16),
                              moe_b_gate_up[l], moe_w_down[l].astype(BF16), moe_b_down[l])
        h = _moe_combine(tile_mod, pos, gates, h, mod, final_norm_w, y_rows, l == depth - 1)

    y_prompt = h[:n_ctx].reshape(batch, seq, d)
    y_sample = h[n_ctx:].reshape(dec_batch, dec_seq, d)
    return (y_prompt, y_sample, jnp.stack(ks, axis=1), jnp.stack(vs, axis=1),
            jnp.stack(ssd_states, axis=1), jnp.stack(rwkv_states, axis=1))
```

```python
import functools
import math

import numpy as np
import jax
import jax.numpy as jnp
from jax import lax
from jax.experimental import pallas as pl
from jax.experimental.pallas import tpu as pltpu

F32 = jnp.float32
BF16 = jnp.bfloat16
HI = lax.Precision.HIGHEST

D_MODEL = 1024
GRID_W = 64
HEAD_DIM = 64
ATTN_HEADS = 6
ATTN_KV_HEADS = 2
ROPE_THETA = 10000.0
SSD_HEADS = 6
SSD_GROUPS = 2
SSD_STATE = 64
SSD_CONV = 5
SSD_CHUNK = 128
RWKV_HEADS = 4
RWKV_CHUNK = 64
N_EXPERTS = 32
TOP_K = 4
D_EXPERT = 1024
SWIGLU_LIMIT = 7.0
SWIGLU_ALPHA = 1.702
RMS_EPS = 1e-6
GN_EPS = 64e-5

ATTN_WIDTH = ATTN_HEADS * HEAD_DIM
KV_WIDTH = ATTN_KV_HEADS * HEAD_DIM
SSD_WIDTH = SSD_HEADS * HEAD_DIM
XBC_WIDTH = SSD_WIDTH + 2 * SSD_GROUPS * SSD_STATE
RWKV_WIDTH = RWKV_HEADS * HEAD_DIM
LANES = 128

COL_Q, COL_Z, COL_R, COL_KR = 0, 384, 768, 1024
COL_K, COL_V, COL_DT, COL_WALO, COL_GLO = 1280, 1408, 1536, 1664, 1792
COL_XBC, COL_VR = 1920, 2560
PROJ_WIDTH = 2816

TOKEN_TILE = 256
MOE_ROWS = 512
DISPATCH_TILE = 256
COMBINE_TILE = 128
ATTN_Q_TILE = 256
ATTN_K_TILE_MAX = 1536
RWKV_SEQS = 2
VMEM_LIMIT = 56 * 1024 * 1024


def _cparams(*sem):
    return pltpu.CompilerParams(dimension_semantics=sem, vmem_limit_bytes=VMEM_LIMIT)


def _nt(a, b, precision=None):
    return lax.dot_general(a, b, (((1,), (1,)), ((), ())), precision=precision,
                           preferred_element_type=F32)


def _tn(a, b, precision=None):
    return lax.dot_general(a, b, (((0,), (0,)), ((), ())), precision=precision,
                           preferred_element_type=F32)


def _mm(a, b, precision=None):
    return jnp.dot(a, b, precision=precision, preferred_element_type=F32)


def _split2(x):
    hi = x.astype(BF16)
    return hi, (x - hi.astype(F32)).astype(BF16)


def _split3(x):
    hi = x.astype(BF16)
    rest = x - hi.astype(F32)
    mid = rest.astype(BF16)
    return hi, mid, (rest - mid.astype(F32)).astype(BF16)


def _mm_3pass(a, b_hi, b_lo):
    a_hi, a_lo = _split2(a)
    return _mm(a_hi, b_hi) + (_mm(a_lo, b_hi) + _mm(a_hi, b_lo))


def _mm_exact_lhs(m_bf16, x):
    hi, mid, lo = _split3(x)
    return _mm(m_bf16, hi) + (_mm(m_bf16, mid) + _mm(m_bf16, lo))


def _mm_exact_rhs(x, m_bf16):
    hi, mid, lo = _split3(x)
    return _mm(hi, m_bf16) + (_mm(mid, m_bf16) + _mm(lo, m_bf16))


def _sigmoid(x):
    return 1.0 / (1.0 + jnp.exp(-x))


def _rms(x, w):
    ms = jnp.mean(x * x, axis=-1, keepdims=True)
    return x * lax.rsqrt(ms + RMS_EPS) * w


def _ada_body(c_ref, w_ref, b_ref, o_ref):
    x = c_ref[...]
    x = x * _sigmoid(x)
    o_ref[0] = _mm(x, w_ref[0], HI) + b_ref[0]


def _ada(cond, ada_w, ada_b):
    depth, d, n = ada_w.shape
    rows = cond.shape[0]
    tn = 512
    return pl.pallas_call(
        _ada_body,
        grid=(depth, n // tn),
        in_specs=[pl.BlockSpec((rows, d), lambda l, j: (0, 0)),
                  pl.BlockSpec((1, d, tn), lambda l, j: (l, 0, j)),
                  pl.BlockSpec((1, 1, tn), lambda l, j: (l, 0, j))],
        out_specs=pl.BlockSpec((1, rows, tn), lambda l, j: (l, 0, j)),
        out_shape=jax.ShapeDtypeStruct((depth, rows, n), F32),
        compiler_params=_cparams("parallel", "parallel"),
        name="ada_mod",
    )(cond, ada_w, ada_b.reshape(depth, 1, n))


def _in_proj_body(tmod_ref, h_ref, mod_ref, nw_ref, w_ref, o_ref):
    del tmod_ref
    mod = mod_ref[0]
    u = _rms(h_ref[...], nw_ref[...]) * (1.0 + mod[1:2]) + mod[0:1]
    o_ref[...] = _mm(u.astype(BF16), w_ref[...])


def _in_proj(tile_mod, h, mod, norm_w, w):
    n_tok, d = h.shape
    tm = TOKEN_TILE
    grid_spec = pltpu.PrefetchScalarGridSpec(
        num_scalar_prefetch=1,
        grid=(n_tok // tm,),
        in_specs=[pl.BlockSpec((tm, d), lambda i, t: (i, 0)),
                  pl.BlockSpec((1, 6, d), lambda i, t: (t[i], 0, 0)),
                  pl.BlockSpec((1, d), lambda i, t: (0, 0)),
                  pl.BlockSpec((d, PROJ_WIDTH), lambda i, t: (0, 0))],
        out_specs=pl.BlockSpec((tm, PROJ_WIDTH), lambda i, t: (i, 0)))
    return pl.pallas_call(
        _in_proj_body, grid_spec=grid_spec,
        out_shape=jax.ShapeDtypeStruct((n_tok, PROJ_WIDTH), F32),
        compiler_params=_cparams("parallel"),
        name="in_proj",
    )(tile_mod, h, mod, norm_w.reshape(1, d), w)


def _rope_apply(x, cos, sin_signed):
    width = x.shape[-1]
    quarter = HEAD_DIM // 4
    up = pltpu.roll(x, width - quarter, 1)
    down = pltpu.roll(x, quarter, 1)
    lane = lax.broadcasted_iota(jnp.int32, x.shape, 1)
    first = (lane % (2 * quarter)) < quarter
    return x * cos + jnp.where(first, up, down) * sin_signed


def _prep_body(*refs, rope):
    if rope:
        (q_ref, k_ref, v_ref, qnw_ref, knw_ref, bd_ref, cq_ref, sq_ref, ck_ref, sk_ref,
         qo_ref, ko_ref, vo_ref) = refs
    else:
        q_ref, k_ref, v_ref, qnw_ref, knw_ref, bd_ref, qo_ref, ko_ref, vo_ref, kn_ref = refs
    bd = bd_ref[...]
    q = q_ref[...]
    k = k_ref[...]
    qn = q * lax.rsqrt(_mm(q * q, bd, HI) + RMS_EPS) * qnw_ref[...]
    kn = k * lax.rsqrt(_mm(k * k, bd[:KV_WIDTH, :KV_WIDTH], HI) + RMS_EPS) * knw_ref[...]
    if rope:
        qn = _rope_apply(qn, cq_ref[...], sq_ref[...])
        kr = _rope_apply(kn, ck_ref[...], sk_ref[...])
    else:
        kn_ref[...] = kn
        kr = kn
    qo_ref[...] = (qn * (HEAD_DIM ** -0.5 * math.log2(math.e))).T.astype(BF16)
    ko_ref[...] = kr.astype(BF16)
    vo_ref[...] = v_ref[...].T.astype(BF16)


def _attn_prep(proj, tile_off, n_tok, q_norm_w, k_norm_w, rope_tabs, tiles_per_seq):
    tm = TOKEN_TILE
    rope = rope_tabs is not None
    head_of = np.arange(ATTN_WIDTH) // HEAD_DIM
    bd = jnp.asarray((head_of[:, None] == head_of[None, :]).astype(np.float32) / HEAD_DIM)
    qnw = jnp.tile(q_norm_w, ATTN_HEADS).reshape(1, ATTN_WIDTH)
    knw = jnp.tile(k_norm_w, ATTN_KV_HEADS).reshape(1, KV_WIDTH)
    in_specs = [pl.BlockSpec((tm, ATTN_WIDTH), lambda i: (i + tile_off, COL_Q // ATTN_WIDTH)),
                pl.BlockSpec((tm, KV_WIDTH), lambda i: (i + tile_off, COL_K // KV_WIDTH)),
                pl.BlockSpec((tm, KV_WIDTH), lambda i: (i + tile_off, COL_V // KV_WIDTH)),
                pl.BlockSpec((1, ATTN_WIDTH), lambda i: (0, 0)),
                pl.BlockSpec((1, KV_WIDTH), lambda i: (0, 0)),
                pl.BlockSpec((ATTN_WIDTH, ATTN_WIDTH), lambda i: (0, 0))]
    args = [proj, proj, proj, qnw, knw, bd]
    out_specs = [pl.BlockSpec((ATTN_WIDTH, tm), lambda i: (0, i)),
                 pl.BlockSpec((tm, KV_WIDTH), lambda i: (i, 0)),
                 pl.BlockSpec((KV_WIDTH, tm), lambda i: (0, i))]
    out_shape = [jax.ShapeDtypeStruct((ATTN_WIDTH, n_tok), BF16),
                 jax.ShapeDtypeStruct((n_tok, KV_WIDTH), BF16),
                 jax.ShapeDtypeStruct((KV_WIDTH, n_tok), BF16)]
    if rope:
        cq, sq, ck, sk = rope_tabs
        in_specs += [pl.BlockSpec((tm, ATTN_WIDTH), lambda i: (i % tiles_per_seq, 0)),
                     pl.BlockSpec((tm, ATTN_WIDTH), lambda i: (i % tiles_per_seq, 0)),
                     pl.BlockSpec((tm, KV_WIDTH), lambda i: (i % tiles_per_seq, 0)),
                     pl.BlockSpec((tm, KV_WIDTH), lambda i: (i % tiles_per_seq, 0))]
        args += [cq, sq, ck, sk]
    else:
        out_specs.append(pl.BlockSpec((tm, KV_WIDTH), lambda i: (i, 0)))
        out_shape.append(jax.ShapeDtypeStruct((n_tok, KV_WIDTH), F32))
    return pl.pallas_call(
        functools.partial(_prep_body, rope=rope),
        grid=(n_tok // tm,), in_specs=in_specs, out_specs=out_specs, out_shape=out_shape,
        compiler_params=_cparams("parallel"),
        name="attn_prep_rope" if rope else "attn_prep",
    )(*args)


def _rope_tables(seq_len):
    quarter = HEAD_DIM // 4
    half = HEAD_DIM // 2
    t = np.arange(seq_len)
    pos = np.stack([t // GRID_W, t % GRID_W], axis=1).astype(np.float32)
    inv_freq = jnp.power(ROPE_THETA, -jnp.arange(0, half, 2, dtype=F32) / half)
    d = np.arange(HEAD_DIM)
    which = d // half
    freq = inv_freq[d % quarter]
    ang = jnp.asarray(pos)[:, which] * freq[None, :]
    sign = np.where((d % half) < quarter, -1.0, 1.0).astype(np.float32)
    cos, sin = jnp.cos(ang), jnp.sin(ang) * sign[None, :]
    return (jnp.tile(cos, (1, ATTN_HEADS)), jnp.tile(sin, (1, ATTN_HEADS)),
            jnp.tile(cos, (1, ATTN_KV_HEADS)), jnp.tile(sin, (1, ATTN_KV_HEADS)))


def _attn_body(qt_ref, k_ref, vt_ref, o_ref, m_ref, l_ref, acc_ref, *, nk):
    ki = pl.program_id(2)

    @pl.when(ki == 0)
    def _():
        m_ref[...] = jnp.full(m_ref.shape, -jnp.inf, F32)
        l_ref[...] = jnp.zeros(l_ref.shape, F32)
        acc_ref[...] = jnp.zeros(acc_ref.shape, F32)

    groups = ATTN_HEADS // ATTN_KV_HEADS
    m_all = m_ref[...]
    l_all = l_ref[...]
    k_all = k_ref[0]
    vt_all = vt_ref[0]
    scores, probs, new_m, new_l, alphas, pvs = {}, {}, {}, {}, {}, {}

    def stage_scores(h):
        kv = h // groups
        qt = qt_ref[h * HEAD_DIM:(h + 1) * HEAD_DIM, :]
        scores[h] = _mm(k_all[:, kv * HEAD_DIM:(kv + 1) * HEAD_DIM], qt)

    def stage_softmax(h):
        st = scores.pop(h)
        m_prev = m_all[h]
        m_new = jnp.maximum(m_prev, jnp.max(st, axis=0, keepdims=True))
        alphas[h] = jnp.exp2(m_prev - m_new)
        pt = jnp.exp2(st - m_new)
        new_l[h] = alphas[h] * l_all[h] + jnp.sum(pt, axis=0, keepdims=True)
        new_m[h] = m_new
        probs[h] = pt.astype(BF16)

    def stage_values(h):
        kv = h // groups
        pvs[h] = _mm(vt_all[kv * HEAD_DIM:(kv + 1) * HEAD_DIM, :], probs.pop(h))

    for step in range(ATTN_HEADS + 2):
        if step < ATTN_HEADS:
            stage_scores(step)
        if 0 <= step - 1 < ATTN_HEADS:
            stage_softmax(step - 1)
        if 0 <= step - 2 < ATTN_HEADS:
            stage_values(step - 2)
    for h in range(ATTN_HEADS):
        acc_ref[h] = alphas[h] * acc_ref[h] + pvs[h]
        m_ref[h] = new_m[h]
        l_ref[h] = new_l[h]

    @pl.when(ki == nk - 1)
    def _():
        out = [acc_ref[h] / l_ref[h] for h in range(ATTN_HEADS)]
        o_ref[...] = jnp.concatenate(out, axis=0).astype(o_ref.dtype)


def _key_tile(n_keys):
    tk = min(n_keys, ATTN_K_TILE_MAX)
    while n_keys % tk:
        tk -= LANES
    return tk


def _attention(qt, k, vt, seq_len):
    b, s, _ = k.shape
    tq = ATTN_Q_TILE
    tk = _key_tile(s)
    nq = seq_len // tq
    nk = s // tk
    return pl.pallas_call(
        functools.partial(_attn_body, nk=nk),
        grid=(b, nq, nk),
        in_specs=[pl.BlockSpec((ATTN_WIDTH, tq), lambda bi, qi, ki: (0, bi * nq + qi)),
                  pl.BlockSpec((1, tk, KV_WIDTH), lambda bi, qi, ki: (bi, ki, 0)),
                  pl.BlockSpec((1, KV_WIDTH, tk), lambda bi, qi, ki: (bi, 0, ki))],
        out_specs=pl.BlockSpec((ATTN_WIDTH, tq), lambda bi, qi, ki: (0, bi * nq + qi)),
        out_shape=jax.ShapeDtypeStruct((ATTN_WIDTH, b * seq_len), BF16),
        scratch_shapes=[pltpu.VMEM((ATTN_HEADS, 1, tq), F32),
                        pltpu.VMEM((ATTN_HEADS, 1, tq), F32),
                        pltpu.VMEM((ATTN_HEADS, HEAD_DIM, tq), F32)],
        compiler_params=_cparams("parallel", "parallel", "arbitrary"),
        name="attention",
    )(qt, k, vt)


def _conv_body(first_ref, last_ref, cur_ref, prev_ref, next_ref, w_ref, b_ref, o_ref):
    i = pl.program_id(0)
    x = cur_ref[...]
    tm = x.shape[0]
    prev = jnp.where(first_ref[i] == 1, 0.0, prev_ref[...])
    nxt = jnp.where(last_ref[i] == 1, 0.0, next_ref[...])
    row8 = lax.broadcasted_iota(jnp.int32, prev.shape, 0)
    w = w_ref[...]
    half = SSD_CONV // 2
    acc = b_ref[...] + w[half:half + 1] * x
    for s in range(1, half + 1):
        rolled = pltpu.roll(x, s, 0)
        top = jnp.where(row8 < s, pltpu.roll(prev, s, 0), rolled[:8])
        acc = acc + w[half - s:half - s + 1] * jnp.concatenate([top, rolled[8:]], axis=0)
        rolled = pltpu.roll(x, tm - s, 0)
        bot = jnp.where(row8 >= 8 - s, pltpu.roll(nxt, 8 - s, 0), rolled[tm - 8:])
        acc = acc + w[half + s:half + s + 1] * jnp.concatenate([rolled[:tm - 8], bot], axis=0)
    o_ref[...] = acc * _sigmoid(acc)


def _ssd_conv(first, last, proj, conv_w, conv_b):
    n_tok = proj.shape[0]
    tm = TOKEN_TILE
    sub = tm // 8
    col = COL_XBC // XBC_WIDTH
    w8 = jnp.concatenate([conv_w, jnp.zeros((8 - SSD_CONV, XBC_WIDTH), F32)], axis=0)
    grid_spec = pltpu.PrefetchScalarGridSpec(
        num_scalar_prefetch=2,
        grid=(n_tok // tm,),
        in_specs=[pl.BlockSpec((tm, XBC_WIDTH), lambda i, f, l: (i, col)),
                  pl.BlockSpec((8, XBC_WIDTH), lambda i, f, l: (jnp.maximum(i * sub - 1, 0), col)),
                  pl.BlockSpec((8, XBC_WIDTH),
                               lambda i, f, l: (jnp.minimum((i + 1) * sub, n_tok // 8 - 1), col)),
                  pl.BlockSpec((8, XBC_WIDTH), lambda i, f, l: (0, 0)),
                  pl.BlockSpec((1, XBC_WIDTH), lambda i, f, l: (0, 0))],
        out_specs=pl.BlockSpec((tm, XBC_WIDTH), lambda i, f, l: (i, 0)))
    return pl.pallas_call(
        _conv_body, grid_spec=grid_spec,
        out_shape=jax.ShapeDtypeStruct((n_tok, XBC_WIDTH), F32),
        compiler_params=_cparams("parallel"),
        name="ssd_conv",
    )(first, last, proj, proj, proj, w8, conv_b.reshape(1, XBC_WIDTH))


def _softplus(x):
    return jnp.maximum(x, 0.0) + jnp.log1p(jnp.exp(-jnp.abs(x)))


def _ssd_body(*refs, nc, has_init):
    if has_init:
        (xs_ref, bm_ref, cm_ref, dt_ref, z_ref, dtb_ref, an_ref, drow_ref, nw_ref, h0_ref,
         y_ref, fin_ref, y_scr, h_scr) = refs
    else:
        (xs_ref, bm_ref, cm_ref, dt_ref, z_ref, dtb_ref, an_ref, drow_ref, nw_ref,
         y_ref, fin_ref, y_scr, h_scr) = refs
    sweep_id = pl.program_id(1)
    c = pl.program_id(2)
    chunk_len = SSD_CHUNK
    rep = SSD_HEADS // SSD_GROUPS

    @pl.when(c == 0)
    def _():
        if has_init:
            h_scr[...] = h0_ref[0, 0]
        else:
            h_scr[...] = jnp.zeros(h_scr.shape, F32)

    def sweep(direction):
        chunk = c if direction == 0 else nc - 1 - c
        row0 = pl.multiple_of(chunk * chunk_len, chunk_len)
        xs = xs_ref[...]
        bm = bm_ref[...]
        cm = cm_ref[...]
        dt_all = _softplus(dt_ref[...] + dtb_ref[...])
        la_all = dt_all * an_ref[...]
        ti = lax.broadcasted_iota(jnp.int32, (chunk_len, chunk_len), 0)
        si = lax.broadcasted_iota(jnp.int32, (chunk_len, chunk_len), 1)
        before_eq = (si <= ti) if direction == 0 else (si >= ti)
        cs_all = _mm_exact_lhs(jnp.where(before_eq, 1.0, 0.0).astype(BF16), la_all)
        cs_t = cs_all.T
        last = chunk_len - 1 if direction == 0 else 0
        b_g = [bm[:, g * SSD_STATE:(g + 1) * SSD_STATE].astype(BF16) for g in range(SSD_GROUPS)]
        c_g = [cm[:, g * SSD_STATE:(g + 1) * SSD_STATE].astype(BF16) for g in range(SSD_GROUPS)]
        cb = [_nt(c_g[g], b_g[g]) for g in range(SSD_GROUPS)]
        states = [h_scr[h] for h in range(SSD_HEADS)]
        y_off = [_nt(c_g[h // rep], states[h].astype(BF16)) for h in range(SSD_HEADS)]
        cs_c, xd, scores = [], [], []
        for h in range(SSD_HEADS):
            j = direction * SSD_HEADS + h
            cs_c.append(cs_all[:, j:j + 1])
            decay = jnp.exp(jnp.where(before_eq, cs_c[h] - cs_t[j:j + 1, :], -jnp.inf))
            xd.append(xs[:, h * HEAD_DIM:(h + 1) * HEAD_DIM] * dt_all[:, j:j + 1])
            scores.append((cb[h // rep] * decay).astype(BF16))
        y_diag = [_mm(scores[h], xd[h].astype(BF16)) for h in range(SSD_HEADS)]
        heads, tots, to_state = [], [], []
        for h in range(SSD_HEADS):
            j = direction * SSD_HEADS + h
            tots.append(cs_all[last:last + 1, j:j + 1])
            to_end = jnp.exp(tots[h] - cs_c[h])
            to_state.append(_tn((xd[h] * to_end).astype(BF16), b_g[h // rep]))
            heads.append(y_diag[h] + y_off[h] * jnp.exp(cs_c[h]))
        for h in range(SSD_HEADS):
            h_scr[h] = jnp.exp(tots[h]) * states[h] + to_state[h]
        y_dir = jnp.concatenate(heads, axis=-1)
        if direction == 0:
            y_scr[pl.ds(row0, chunk_len), :] = y_dir + drow_ref[...] * xs
        else:
            y = y_scr[pl.ds(row0, chunk_len), :] + y_dir
            z = z_ref[...]
            y = y * (z * _sigmoid(z))
            y_ref[...] = _rms(y, nw_ref[...]).astype(y_ref.dtype)

    @pl.when(sweep_id == 0)
    def _():
        sweep(0)

    @pl.when(sweep_id == 1)
    def _():
        sweep(1)

    @pl.when(c == nc - 1)
    def _():
        fin_ref[0, 0] = h_scr[...]


def _ssd(proj, xbc, tile_off, batch, seq_len, dt_bias, a_log, d_skip, norm_w, init):
    chunk_len = SSD_CHUNK
    nc = seq_len // chunk_len
    base = tile_off * (TOKEN_TILE // chunk_len)
    has_init = init is not None

    def row(b, s, c):
        return base + b * nc + c + s * (nc - 1 - 2 * c)

    def z_row(b, s, c):
        return base + b * nc + s * (nc - 1 - c) + (1 - s) * (nc - 1)

    def y_row(b, s, c):
        return b * nc + s * (nc - 1 - c) + (1 - s) * (nc - 1)

    dtb = jnp.zeros((1, LANES), F32).at[0, :2 * SSD_HEADS].set(dt_bias.reshape(-1))
    a_neg = jnp.zeros((1, LANES), F32).at[0, :2 * SSD_HEADS].set(-jnp.exp(a_log.reshape(-1)))
    d_row = jnp.repeat(d_skip, HEAD_DIM).reshape(1, SSD_WIDTH)
    in_specs = [
        pl.BlockSpec((chunk_len, SSD_WIDTH), lambda b, s, c: (row(b, s, c), 0)),
        pl.BlockSpec((chunk_len, LANES), lambda b, s, c: (row(b, s, c), SSD_WIDTH // LANES)),
        pl.BlockSpec((chunk_len, LANES), lambda b, s, c: (row(b, s, c), SSD_WIDTH // LANES + 1)),
        pl.BlockSpec((chunk_len, LANES), lambda b, s, c: (row(b, s, c), COL_DT // LANES)),
        pl.BlockSpec((chunk_len, SSD_WIDTH), lambda b, s, c: (z_row(b, s, c), COL_Z // SSD_WIDTH)),
        pl.BlockSpec((1, LANES), lambda b, s, c: (0, 0)),
        pl.BlockSpec((1, LANES), lambda b, s, c: (0, 0)),
        pl.BlockSpec((1, SSD_WIDTH), lambda b, s, c: (0, 0)),
        pl.BlockSpec((1, SSD_WIDTH), lambda b, s, c: (0, 0)),
    ]
    args = [xbc, xbc, xbc, proj, proj, dtb, a_neg, d_row, norm_w.reshape(1, SSD_WIDTH)]
    state_block = (1, 1, SSD_HEADS, HEAD_DIM, SSD_STATE)
    if has_init:
        in_specs.append(pl.BlockSpec(state_block, lambda b, s, c: (b, s, 0, 0, 0)))
        args.append(init)
    return pl.pallas_call(
        functools.partial(_ssd_body, nc=nc, has_init=has_init),
        grid=(batch, 2, nc),
        in_specs=in_specs,
        out_specs=[pl.BlockSpec((chunk_len, SSD_WIDTH), lambda b, s, c: (y_row(b, s, c), 0)),
                   pl.BlockSpec(state_block, lambda b, s, c: (b, s, 0, 0, 0))],
        out_shape=[jax.ShapeDtypeStruct((batch * seq_len, SSD_WIDTH), BF16),
                   jax.ShapeDtypeStruct((batch, 2, SSD_HEADS, HEAD_DIM, SSD_STATE), F32)],
        scratch_shapes=[pltpu.VMEM((seq_len, SSD_WIDTH), F32),
                        pltpu.VMEM((SSD_HEADS, HEAD_DIM, SSD_STATE), F32)],
        compiler_params=_cparams("parallel", "arbitrary", "arbitrary"),
        name="ssd",
    )(*args)


def _rwkv_body(*refs, nc, seq_len, has_init):
    n_io = 10 * RWKV_SEQS
    io_refs = [refs[5 * i:5 * i + 5] for i in range(2 * RWKV_SEQS)]
    (w0_ref, wuph_ref, wupl_ref, a0_ref, auph_ref, aupl_ref, gup_ref, kk_ref, ka_ref, rk_ref,
     lnw_ref, lnb_ref, heads_ref) = refs[n_io:n_io + 13]
    if has_init:
        s0_ref, y_ref, fin_ref, y_scr, s_scr = refs[n_io + 13:]
    else:
        y_ref, fin_ref, y_scr, s_scr = refs[n_io + 13:]

    def head_sum(x):
        return _mm_exact_rhs(x, heads_ref[...])
    c = pl.program_id(1)
    cl = RWKV_CHUNK
    n = HEAD_DIM
    lora = LANES // 2

    @pl.when(c == 0)
    def _():
        if has_init:
            s_scr[...] = s0_ref[...]
        else:
            s_scr[...] = jnp.zeros(s_scr.shape, F32)

    ti = lax.broadcasted_iota(jnp.int32, (cl, cl), 0)
    si = lax.broadcasted_iota(jnp.int32, (cl, cl), 1)
    lane2 = lax.broadcasted_iota(jnp.int32, (cl, 2 * cl), 1)
    left = lane2 < cl
    sign2 = jnp.where(left, 1.0, -1.0)
    t2 = lax.broadcasted_iota(jnp.int32, (cl, 2 * cl), 0)
    s2 = jnp.where(left, lane2, lane2 - cl)

    chains = []
    for idx, (r_ref, k_ref, v_ref, walo_ref, _) in enumerate(io_refs):
        seq, direction = divmod(idx, 2)
        r = r_ref[...]
        k = k_ref[...]
        v = v_ref[...]
        walo = walo_ref[...]
        zw = w0_ref[direction:direction + 1] + _mm_3pass(
            jnp.tanh(walo[:, :lora]), wuph_ref[direction], wupl_ref[direction])
        lw = -math.exp(-0.5) * _sigmoid(zw)
        a = _sigmoid(a0_ref[direction:direction + 1] + _mm_3pass(
            walo[:, lora:], auph_ref[direction], aupl_ref[direction]))
        kd = k * (1.0 + (a - 1.0) * ka_ref[...])
        kk_raw = k * kk_ref[...]
        kk_all = kk_raw * lax.rsqrt(head_sum(kk_raw * kk_raw) + 1e-12)
        akk_all = a * kk_all
        before_eq = (si <= ti) if direction == 0 else (si >= ti)
        before = (si < ti) if direction == 0 else (si > ti)
        before_eq2 = (s2 <= t2) if direction == 0 else (s2 >= t2)
        lp_all = _mm_exact_lhs(jnp.where(before_eq, 1.0, 0.0).astype(BF16), lw)
        last = cl - 1 if direction == 0 else 0
        lp_last = lp_all[last:last + 1]
        dec_in = jnp.exp(lp_all)
        dec_out = jnp.exp(-lp_all)
        dec_before = jnp.exp(lp_all - lw)
        to_end = jnp.exp(lp_last - lp_all)
        end_decay = jnp.exp(lp_last)
        kap_all = (kk_all * dec_before).astype(BF16)
        rt_all = (r * dec_in).astype(BF16)
        kt_all = (kd * dec_out).astype(BF16)
        bt_all = (akk_all * dec_out).astype(BF16)
        kend_all = (kd * to_end).astype(BF16)
        bend_all = (akk_all * to_end).astype(BF16)
        for h in range(RWKV_HEADS):
            sl = slice(h * n, (h + 1) * n)
            chains.append(dict(
                where=(seq, direction, h), v=v[:, sl],
                kap_rt=jnp.concatenate([kap_all[:, sl], rt_all[:, sl]], axis=0),
                kt_bt=jnp.concatenate([kt_all[:, sl], bt_all[:, sl]], axis=0),
                state=s_scr[seq, direction, h], before=before, before_eq2=before_eq2,
                end_decay=end_decay[:, sl],
                kend=jnp.concatenate([kend_all[:, sl], bend_all[:, sl]], axis=0)))

    for ch in chains:
        ch['g'] = _nt(ch['kap_rt'], ch['kt_bt'])
    for ch in chains:
        ch['fs'] = _nt(ch['kap_rt'], ch['state'].astype(BF16))
    for ch in chains:
        g = ch.pop('g')
        ch['n_mat'] = jnp.where(ch['before'], g[:cl, :cl], 0.0).astype(BF16)
        ch['l_mat'] = jnp.where(ch['before'], g[:cl, cl:], 0.0)
        ch['rkb'] = (jnp.where(ch['before_eq2'], g[cl:], 0.0) * sign2).astype(BF16)
    for ch in chains:
        x0 = ch['fs'][:cl] + _mm(ch.pop('n_mat'), ch['v'].astype(BF16))
        ch['z'] = jnp.concatenate([-ch.pop('l_mat'), x0], axis=1)

    for _ in range(int(math.log2(cl))):
        for ch in chains:
            z = ch['z']
            pz = _mm(z[:, :cl].astype(BF16), z.astype(BF16))
            ch['z'] = jnp.where(left, pz, z + pz)

    for ch in chains:
        u = ch['z'][:, cl:]
        ch['y'] = ch['fs'][cl:] + _mm(ch['rkb'],
                                      jnp.concatenate([ch['v'], u], axis=0).astype(BF16))
    for ch in chains:
        seq, direction, h = ch['where']
        v_nu = jnp.concatenate([ch['v'], -ch['z'][:, cl:]], axis=0).astype(BF16)
        s_scr[seq, direction, h] = ch['state'] * ch['end_decay'] + _tn(v_nu, ch['kend'])

    def finish(y, refs_):
        r_ref, k_ref, v_ref, _, glo_ref = refs_
        r = r_ref[...]
        v = v_ref[...]
        gate = _mm(_sigmoid(glo_ref[...]).astype(BF16), gup_ref[...])
        bonus = head_sum(r * k_ref[...] * rk_ref[...]) * v
        centred = y - head_sum(y) * (1.0 / n)
        var = head_sum(centred * centred) * (1.0 / n)
        y_n = centred * lax.rsqrt(var + GN_EPS)
        return ((y_n * lnw_ref[...] + lnb_ref[...] + bonus) * gate).astype(y_ref.dtype)

    ys, rows = [], []
    for idx in range(2 * RWKV_SEQS):
        seq, direction = divmod(idx, 2)
        ys.append(jnp.concatenate(
            [ch['y'] for ch in chains[idx * RWKV_HEADS:(idx + 1) * RWKV_HEADS]], axis=-1))
        chunk = c if direction == 0 else nc - 1 - c
        rows.append(pl.multiple_of(seq * seq_len + chunk * cl, cl))

    @pl.when(c < nc // 2)
    def _():
        for idx in range(2 * RWKV_SEQS):
            y_scr[pl.ds(rows[idx], cl), :] = ys[idx]

    @pl.when(c >= nc // 2)
    def _():
        for idx in range(2 * RWKV_SEQS):
            y_ref[pl.ds(rows[idx], cl), :] = finish(y_scr[pl.ds(rows[idx], cl), :] + ys[idx],
                                                    io_refs[idx])

    @pl.when(c == nc - 1)
    def _():
        fin_ref[...] = s_scr[...]


def _rwkv(proj, tile_off, batch, seq_len, lp, init):
    cl = RWKV_CHUNK
    nc = seq_len // cl
    assert nc % 2 == 0 and batch % RWKV_SEQS == 0
    base = tile_off * (TOKEN_TILE // cl)
    has_init = init is not None
    w = RWKV_WIDTH

    def const(shape):
        return pl.BlockSpec(shape, lambda b, c: (0,) * len(shape))

    def chunk_specs(seq, direction):
        def row(b, c):
            first = base + (b * RWKV_SEQS + seq) * nc
            return first + (c if direction == 0 else nc - 1 - c)
        return [pl.BlockSpec((cl, w), lambda b, c: (row(b, c), COL_R // w)),
                pl.BlockSpec((cl, w), lambda b, c: (row(b, c), COL_KR // w)),
                pl.BlockSpec((cl, w), lambda b, c: (row(b, c), COL_VR // w)),
                pl.BlockSpec((cl, LANES), lambda b, c: (row(b, c), COL_WALO // LANES)),
                pl.BlockSpec((cl, LANES), lambda b, c: (row(b, c), COL_GLO // LANES))]

    in_specs = []
    for seq in range(RWKV_SEQS):
        for direction in range(2):
            in_specs += chunk_specs(seq, direction)
    lora_shape = (2, LANES // 2, w)
    in_specs += [const((2, w)), const(lora_shape), const(lora_shape), const((2, w)),
                 const(lora_shape), const(lora_shape), const((LANES, w)),
                 const((1, w)), const((1, w)), const((1, w)), const((1, w)), const((1, w)),
                 const((w, w))]
    wup_hi, wup_lo = _split2(lp['rwkv_w_up'])
    aup_hi, aup_lo = _split2(lp['rwkv_a_up'])
    head_of = np.arange(w) // HEAD_DIM
    same_head = jnp.asarray((head_of[:, None] == head_of[None, :]).astype(np.float32), BF16)
    args = [proj] * (10 * RWKV_SEQS) + [
        lp['rwkv_w0'], wup_hi, wup_lo, lp['rwkv_a0'], aup_hi, aup_lo,
        lp['rwkv_g_up'].astype(BF16), lp['rwkv_k_k'].reshape(1, w), lp['rwkv_k_a'].reshape(1, w),
        lp['rwkv_r_k'].reshape(1, w), lp['rwkv_ln_w'].reshape(1, w),
        lp['rwkv_ln_b'].reshape(1, w), same_head]
    state_block = (RWKV_SEQS, 2, RWKV_HEADS, HEAD_DIM, HEAD_DIM)
    if has_init:
        in_specs.append(pl.BlockSpec(state_block, lambda b, c: (b, 0, 0, 0, 0)))
        args.append(init)
    return pl.pallas_call(
        functools.partial(_rwkv_body, nc=nc, seq_len=seq_len, has_init=has_init),
        grid=(batch // RWKV_SEQS, nc),
        in_specs=in_specs,
        out_specs=[pl.BlockSpec((RWKV_SEQS * seq_len, w), lambda b, c: (b, 0)),
                   pl.BlockSpec(state_block, lambda b, c: (b, 0, 0, 0, 0))],
        out_shape=[jax.ShapeDtypeStruct((batch * seq_len, w), BF16),
                   jax.ShapeDtypeStruct((batch, 2, RWKV_HEADS, HEAD_DIM, HEAD_DIM), F32)],
        scratch_shapes=[pltpu.VMEM((RWKV_SEQS * seq_len, w), F32),
                        pltpu.VMEM(state_block, F32)],
        compiler_params=_cparams("parallel", "arbitrary"),
        name="rwkv",
    )(*args)


def _out_proj_body(tmod_ref, attn_ref, ssd_ref, rwkv_ref, h_ref, mod_ref, nw_ref, w_ref, rwh_ref,
                   rwl_ref, rb_ref, h_out, u_out, lg_out):
    del tmod_ref
    mixed = (_tn(attn_ref[...], w_ref[:ATTN_WIDTH])
             + _mm(ssd_ref[...], w_ref[ATTN_WIDTH:ATTN_WIDTH + SSD_WIDTH])
             + _mm(rwkv_ref[...], w_ref[ATTN_WIDTH + SSD_WIDTH:]))
    mod = mod_ref[0]
    h = h_ref[...] + mod[2:3] * mixed
    h_out[...] = h
    u = _rms(h, nw_ref[...]) * (1.0 + mod[4:5]) + mod[3:4]
    u_out[...] = u
    lg_out[...] = _mm_3pass(u, rwh_ref[...], rwl_ref[...]) + rb_ref[...]


def _out_proj(tile_mod, attn_t, ssd, rwkv, h, mod, norm_w, w, router_w, router_b):
    n_tok, d = h.shape
    tm = TOKEN_TILE
    rw = jnp.zeros((d, LANES), F32).at[:, :N_EXPERTS].set(router_w)
    rw_hi, rw_lo = _split2(rw)
    rb = jnp.zeros((1, LANES), F32).at[0, :N_EXPERTS].set(router_b)
    grid_spec = pltpu.PrefetchScalarGridSpec(
        num_scalar_prefetch=1,
        grid=(n_tok // tm,),
        in_specs=[pl.BlockSpec((ATTN_WIDTH, tm), lambda i, t: (0, i)),
                  pl.BlockSpec((tm, SSD_WIDTH), lambda i, t: (i, 0)),
                  pl.BlockSpec((tm, RWKV_WIDTH), lambda i, t: (i, 0)),
                  pl.BlockSpec((tm, d), lambda i, t: (i, 0)),
                  pl.BlockSpec((1, 6, d), lambda i, t: (t[i], 0, 0)),
                  pl.BlockSpec((1, d), lambda i, t: (0, 0)),
                  pl.BlockSpec((d, d), lambda i, t: (0, 0)),
                  pl.BlockSpec((d, LANES), lambda i, t: (0, 0)),
                  pl.BlockSpec((d, LANES), lambda i, t: (0, 0)),
                  pl.BlockSpec((1, LANES), lambda i, t: (0, 0))],
        out_specs=[pl.BlockSpec((tm, d), lambda i, t: (i, 0)),
                   pl.BlockSpec((tm, d), lambda i, t: (i, 0)),
                   pl.BlockSpec((tm, LANES), lambda i, t: (i, 0))])
    return pl.pallas_call(
        _out_proj_body, grid_spec=grid_spec,
        out_shape=[jax.ShapeDtypeStruct((n_tok, d), F32),
                   jax.ShapeDtypeStruct((n_tok, d), F32),
                   jax.ShapeDtypeStruct((n_tok, LANES), F32)],
        compiler_params=_cparams("parallel"),
        name="out_proj",
    )(tile_mod, attn_t, ssd, rwkv, h, mod, norm_w.reshape(1, d), w, rw_hi, rw_lo, rb)


def _dispatch_body(pos_ref, u_ref, init_hbm, o_hbm, pos_smem, isem, sem):
    del init_hbm
    tm = u_ref.shape[0]
    cp = pltpu.make_async_copy(pos_ref.at[0, 0], pos_smem, isem)
    cp.start()
    cp.wait()

    def issue(r, carry):
        for kk in range(TOP_K):
            pltpu.make_async_copy(u_ref.at[pl.ds(r, 1)],
                                  o_hbm.at[pl.ds(pos_smem[kk * tm + r], 1)], sem).start()
        return carry

    lax.fori_loop(0, tm, issue, 0, unroll=2)
    for _ in range(TOP_K):
        pltpu.make_async_copy(u_ref, o_hbm.at[pl.ds(0, tm)], sem).wait()


def _moe_dispatch(u, pos_t, n_rows):
    n_tok, d = u.shape
    tm = DISPATCH_TILE
    steps = n_tok // tm
    return pl.pallas_call(
        _dispatch_body,
        grid=(steps,),
        in_specs=[pl.BlockSpec((1, 1, TOP_K * tm), lambda i: (i, 0, 0)),
                  pl.BlockSpec((tm, d), lambda i: (i, 0)),
                  pl.BlockSpec(memory_space=pl.ANY)],
        out_specs=pl.BlockSpec(memory_space=pl.ANY),
        out_shape=jax.ShapeDtypeStruct((n_rows, d), u.dtype),
        scratch_shapes=[pltpu.SMEM((TOP_K * tm,), jnp.int32), pltpu.SemaphoreType.DMA,
                        pltpu.SemaphoreType.DMA],
        input_output_aliases={2: 0},
        compiler_params=_cparams("arbitrary"),
        name="moe_dispatch",
    )(pos_t, u, jnp.zeros((n_rows, d), u.dtype))


def _slot_major(pos, tm):
    steps = pos.shape[0] // tm
    return pos.reshape(steps, tm, TOP_K).transpose(0, 2, 1).reshape(steps, 1, TOP_K * tm)


def _moe_body(be_ref, nv_ref, new_ref, x_ref, wgu_ref, bgu_ref, wdn_ref, bdn_ref, o_ref,
              wgu_bf, wdn_bf):
    del be_ref
    i = pl.program_id(0)

    @pl.when(new_ref[i] == 1)
    def _():
        wgu_bf[...] = wgu_ref[0].astype(BF16)
        wdn_bf[...] = wdn_ref[0].astype(BF16)

    @pl.when(i < nv_ref[0])
    def _():
        gu = _mm(x_ref[...].astype(BF16), wgu_bf[...]) + bgu_ref[0]
        gate = jnp.minimum(gu[:, :D_EXPERT], SWIGLU_LIMIT)
        up = jnp.clip(gu[:, D_EXPERT:], -SWIGLU_LIMIT, SWIGLU_LIMIT)
        act = (up + 1.0) * gate * _sigmoid(SWIGLU_ALPHA * gate)
        o_ref[...] = _mm(act.astype(BF16), wdn_bf[...]) + bdn_ref[0]

    @pl.when(i >= nv_ref[0])
    def _():
        o_ref[...] = jnp.zeros(o_ref.shape, o_ref.dtype)


def _moe_experts(block_expert, n_valid, x_rows, w_gu, b_gu, w_dn, b_dn):
    n_rows, d = x_rows.shape
    rows = MOE_ROWS
    new_expert = jnp.concatenate([jnp.ones((1,), jnp.int32),
                                  (block_expert[1:] != block_expert[:-1]).astype(jnp.int32)])
    grid_spec = pltpu.PrefetchScalarGridSpec(
        num_scalar_prefetch=3,
        grid=(n_rows // rows,),
        in_specs=[pl.BlockSpec((rows, d), lambda i, be, nv, ne: (i, 0)),
                  pl.BlockSpec((1, d, 2 * D_EXPERT), lambda i, be, nv, ne: (be[i], 0, 0)),
                  pl.BlockSpec((1, 1, 2 * D_EXPERT), lambda i, be, nv, ne: (be[i], 0, 0)),
                  pl.BlockSpec((1, D_EXPERT, d), lambda i, be, nv, ne: (be[i], 0, 0)),
                  pl.BlockSpec((1, 1, d), lambda i, be, nv, ne: (be[i], 0, 0))],
        out_specs=pl.BlockSpec((rows, d), lambda i, be, nv, ne: (i, 0)),
        scratch_shapes=[pltpu.VMEM((d, 2 * D_EXPERT), BF16), pltpu.VMEM((D_EXPERT, d), BF16)])
    return pl.pallas_call(
        _moe_body, grid_spec=grid_spec,
        out_shape=jax.ShapeDtypeStruct((n_rows, d), F32),
        compiler_params=_cparams("arbitrary"),
        name="moe_experts",
    )(block_expert, n_valid, new_expert, x_rows, w_gu, b_gu.reshape(N_EXPERTS, 1, -1), w_dn,
      b_dn.reshape(N_EXPERTS, 1, -1))


def _combine_body(tmod_ref, pos_ref, gates_ref, h_ref, mod_ref, fw_ref, y_hbm, o_ref, pos_smem,
                  buf, isem, sem, *, final):
    del tmod_ref
    tm = h_ref.shape[0]
    cp = pltpu.make_async_copy(pos_ref.at[0, 0], pos_smem, isem)
    cp.start()
    cp.wait()

    def issue(r, carry):
        pltpu.make_async_copy(y_hbm.at[pl.ds(pos_smem[r], 1)], buf.at[pl.ds(r, 1)], sem).start()
        return carry

    lax.fori_loop(0, TOP_K * tm, issue, 0, unroll=8)
    pltpu.make_async_copy(y_hbm.at[pl.ds(0, TOP_K * tm)], buf, sem).wait()
    gates = gates_ref[...]
    y = gates[:, 0:1] * buf[0:tm]
    for kk in range(1, TOP_K):
        y = y + gates[:, kk:kk + 1] * buf[kk * tm:(kk + 1) * tm]
    h = h_ref[...] + mod_ref[0][5:6] * y
    if final:
        o_ref[...] = _rms(h, fw_ref[...])
    else:
        o_ref[...] = h


def _moe_combine(tile_mod, pos, gates, h, mod, final_w, y_rows, final):
    n_tok, d = h.shape
    tm = COMBINE_TILE
    steps = n_tok // tm
    ratio = TOKEN_TILE // tm
    pos_t = _slot_major(pos, tm)
    grid_spec = pltpu.PrefetchScalarGridSpec(
        num_scalar_prefetch=1,
        grid=(steps,),
        in_specs=[pl.BlockSpec((1, 1, TOP_K * tm), lambda i, t: (i, 0, 0)),
                  pl.BlockSpec((tm, TOP_K), lambda i, t: (i, 0)),
                  pl.BlockSpec((tm, d), lambda i, t: (i, 0)),
                  pl.BlockSpec((1, 6, d), lambda i, t: (t[i // ratio], 0, 0)),
                  pl.BlockSpec((1, d), lambda i, t: (0, 0)),
                  pl.BlockSpec(memory_space=pl.ANY)],
        out_specs=pl.BlockSpec((tm, d), lambda i, t: (i, 0)),
        scratch_shapes=[pltpu.SMEM((TOP_K * tm,), jnp.int32),
                        pltpu.VMEM((TOP_K * tm, d), F32),
                        pltpu.SemaphoreType.DMA, pltpu.SemaphoreType.DMA])
    return pl.pallas_call(
        functools.partial(_combine_body, final=final), grid_spec=grid_spec,
        out_shape=jax.ShapeDtypeStruct((n_tok, d), F32),
        compiler_params=_cparams("arbitrary"),
        name="moe_combine",
    )(tile_mod, pos_t, gates, h, mod, final_w.reshape(1, d), y_rows)


def _route(logits):
    n_tok = logits.shape[0]
    rows = MOE_ROWS
    n_blocks = n_tok * TOP_K // rows + N_EXPERTS
    top_logit, top_idx = lax.top_k(logits, TOP_K)
    gates = jax.nn.softmax(top_logit, axis=-1)
    hit = top_idx[:, :, None] == jnp.arange(N_EXPERTS, dtype=top_idx.dtype)[None, None, :]
    per_tok = jnp.sum(hit, axis=1, dtype=jnp.int32)
    before = jnp.cumsum(per_tok, axis=0) - per_tok
    counts = before[-1] + per_tok[-1]
    padded = (counts + rows - 1) // rows * rows
    padded_end = jnp.cumsum(padded)
    padded_start = padded_end - padded
    pos = jnp.sum(jnp.where(hit, (before + padded_start[None, :])[:, None, :], 0), axis=-1)
    block_start = jnp.arange(n_blocks, dtype=jnp.int32) * rows
    block_expert = jnp.minimum(jnp.searchsorted(padded_end, block_start, side='right'),
                               N_EXPERTS - 1).astype(jnp.int32)
    n_valid = (padded_end[-1] // rows).astype(jnp.int32).reshape(1)
    return pos.astype(jnp.int32), gates, block_expert, n_valid, n_blocks * rows


def _permute_w_in(w_in):
    def cols(a, b):
        return w_in[:, a:b]
    q, k, v, z = cols(0, 384), cols(384, 512), cols(512, 640), cols(640, 1024)
    xbc, dt = cols(1024, 1664), cols(1664, 1676)
    r, kr, vr = cols(1676, 1932), cols(1932, 2188), cols(2188, 2444)
    walo, glo = cols(2444, 2572), cols(2572, 2700)
    dt = jnp.pad(dt, ((0, 0), (0, LANES - dt.shape[1])))
    return jnp.concatenate([q, z, r, kr, k, v, dt, walo, glo, xbc, vr], axis=1).astype(BF16)


def kernel(x_prompt, x_sample, cache_attn_k, cache_attn_v, state_ssd, state_rwkv, c, c_ctx,
           norm1_w, norm2_w, ada_w, ada_b, w_in, q_norm_w, k_norm_w,
           ssd_conv_w, ssd_conv_b, ssd_dt_bias, ssd_a_log, ssd_d, ssd_norm_w,
           rwkv_w0, rwkv_w_up, rwkv_a0, rwkv_a_up, rwkv_g_up, rwkv_k_k, rwkv_k_a, rwkv_r_k,
           rwkv_ln_w, rwkv_ln_b, w_out, router_w, router_b,
           moe_w_gate_up, moe_b_gate_up, moe_w_down, moe_b_down, final_norm_w):
    batch, seq, d = x_prompt.shape
    dec_batch, dec_seq, _ = x_sample.shape
    depth = norm1_w.shape[0]
    past = cache_attn_k.shape[2]
    n_ctx, n_lat = batch * seq, dec_batch * dec_seq
    tm = TOKEN_TILE
    ctx_tiles, lat_tiles = n_ctx // tm, n_lat // tm

    tile_mod = np.concatenate([np.zeros(ctx_tiles, np.int32),
                               1 + np.arange(lat_tiles, dtype=np.int32) // (dec_seq // tm)])
    seq_tiles = np.concatenate([np.full(ctx_tiles, seq // tm), np.full(lat_tiles, dec_seq // tm)])
    tile_in_seq = np.concatenate([np.arange(ctx_tiles) % (seq // tm),
                                  np.arange(lat_tiles) % (dec_seq // tm)])
    first = jnp.asarray((tile_in_seq == 0).astype(np.int32))
    last = jnp.asarray((tile_in_seq == seq_tiles - 1).astype(np.int32))
    tile_mod = jnp.asarray(tile_mod)

    cond = jnp.concatenate([c_ctx[None, :], c], axis=0)
    pad_rows = -cond.shape[0] % 8
    cond = jnp.pad(cond, ((0, pad_rows), (0, 0)))
    mod_all = _ada(cond, ada_w, ada_b).reshape(depth, cond.shape[0], 6, d)

    rope_tabs = _rope_tables(dec_seq)
    h = jnp.concatenate([x_prompt.reshape(n_ctx, d), x_sample.reshape(n_lat, d)], axis=0)
    ks, vs, ssd_states, rwkv_states = [], [], [], []
    for l in range(depth):
        lp = dict(rwkv_w0=rwkv_w0[l], rwkv_w_up=rwkv_w_up[l], rwkv_a0=rwkv_a0[l],
                  rwkv_a_up=rwkv_a_up[l], rwkv_g_up=rwkv_g_up[l], rwkv_k_k=rwkv_k_k[l],
                  rwkv_k_a=rwkv_k_a[l], rwkv_r_k=rwkv_r_k[l], rwkv_ln_w=rwkv_ln_w[l],
                  rwkv_ln_b=rwkv_ln_b[l])
        mod = mod_all[l]
        proj = _in_proj(tile_mod, h, mod, norm1_w[l], _permute_w_in(w_in[l]))

        q_c, k_c, v_c, k_norm = _attn_prep(proj, 0, n_ctx, q_norm_w[l], k_norm_w[l], None, 1)
        q_l, k_l, v_l = _attn_prep(proj, ctx_tiles, n_lat, q_norm_w[l], k_norm_w[l], rope_tabs,
                                   dec_seq // tm)
        v_c = v_c.reshape(KV_WIDTH, batch, seq).transpose(1, 0, 2)
        attn_c = _attention(q_c, k_c.reshape(batch, seq, -1), v_c, seq)
        k_full = jnp.concatenate([cache_attn_k[:, l].reshape(dec_batch, past, -1).astype(BF16),
                                  k_l.reshape(dec_batch, dec_seq, -1)], axis=1)
        v_past = cache_attn_v[:, l].reshape(dec_batch, past, -1).astype(BF16).transpose(0, 2, 1)
        v_full = jnp.concatenate(
            [v_past, v_l.reshape(KV_WIDTH, dec_batch, dec_seq).transpose(1, 0, 2)], axis=2)
        attn_l = _attention(q_l, k_full, v_full, dec_seq)
        attn = jnp.concatenate([attn_c, attn_l], axis=1)
        ks.append(k_norm.reshape(batch, seq, ATTN_KV_HEADS, HEAD_DIM))
        vs.append(proj[:n_ctx, COL_V:COL_V + KV_WIDTH].reshape(batch, seq, ATTN_KV_HEADS, HEAD_DIM))

        xbc = _ssd_conv(first, last, proj, ssd_conv_w[l], ssd_conv_b[l])
        ssd_c, ssd_fin = _ssd(proj, xbc, 0, batch, seq, ssd_dt_bias[l], ssd_a_log[l], ssd_d[l],
                              ssd_norm_w[l], None)
        ssd_l, _ = _ssd(proj, xbc, ctx_tiles, dec_batch, dec_seq, ssd_dt_bias[l], ssd_a_log[l],
                        ssd_d[l], ssd_norm_w[l], state_ssd[:, l])
        ssd = jnp.concatenate([ssd_c, ssd_l], axis=0)
        ssd_states.append(ssd_fin)

        rwkv_c, rwkv_fin = _rwkv(proj, 0, batch, seq, lp, None)
        rwkv_l, _ = _rwkv(proj, ctx_tiles, dec_batch, dec_seq, lp, state_rwkv[:, l])
        rwkv = jnp.concatenate([rwkv_c, rwkv_l], axis=0)
        rwkv_states.append(rwkv_fin)

        h, u, logits = _out_proj(tile_mod, attn, ssd, rwkv, h, mod, norm2_w[l],
                                 w_out[l].astype(BF16), router_w[l], router_b[l])

        pos, gates, block_expert, n_valid, n_rows = _route(logits[:, :N_EXPERTS])
        x_rows = _moe_dispatch(u, _slot_major(pos, DISPATCH_TILE), n_rows)
        y_rows = _moe_experts(block_expert, n_valid, x_rows, moe_w_gate_up[l], moe_b_gate_up[l],
                              moe_w_down[l], moe_b_down[l])
        h = _moe_combine(tile_mod, pos, gates, h, mod, final_norm_w, y_rows, l == depth - 1)

    y_prompt = h[:n_ctx].reshape(batch, seq, d)
    y_sample = h[n_ctx:].reshape(dec_batch, dec_seq, d)
    return (y_prompt, y_sample, jnp.stack(ks, axis=1), jnp.stack(vs, axis=1),
            jnp.stack(ssd_states, axis=1), jnp.stack(rwkv_states, axis=1))
```

```python
import functools
import math

import numpy as np
import jax
import jax.numpy as jnp
from jax import lax
from jax.experimental import pallas as pl
from jax.experimental.pallas import tpu as pltpu

F32 = jnp.float32
BF16 = jnp.bfloat16
HI = lax.Precision.HIGHEST

D_MODEL = 1024
GRID_W = 64
HEAD_DIM = 64
ATTN_HEADS = 6
ATTN_KV_HEADS = 2
ROPE_THETA = 10000.0
SSD_HEADS = 6
SSD_GROUPS = 2
SSD_STATE = 64
SSD_CONV = 5
SSD_CHUNK = 128
RWKV_HEADS = 4
RWKV_CHUNK = 64
N_EXPERTS = 32
TOP_K = 4
D_EXPERT = 1024
SWIGLU_LIMIT = 7.0
SWIGLU_ALPHA = 1.702
RMS_EPS = 1e-6
GN_EPS = 64e-5

ATTN_WIDTH = ATTN_HEADS * HEAD_DIM
KV_WIDTH = ATTN_KV_HEADS * HEAD_DIM
SSD_WIDTH = SSD_HEADS * HEAD_DIM
XBC_WIDTH = SSD_WIDTH + 2 * SSD_GROUPS * SSD_STATE
RWKV_WIDTH = RWKV_HEADS * HEAD_DIM
LANES = 128

COL_Q, COL_Z, COL_R, COL_KR = 0, 384, 768, 1024
COL_K, COL_V, COL_DT, COL_WALO, COL_GLO = 1280, 1408, 1536, 1664, 1792
COL_XBC, COL_VR = 1920, 2560
PROJ_WIDTH = 2816

TOKEN_TILE = 256
MOE_ROWS = 512
MOE_TOKEN_TILE = 256
RUN_CHUNK = 8
ATTN_Q_TILE = 256
ATTN_K_TILE_MAX = 1536
RWKV_SEQS = 2
VMEM_LIMIT = 56 * 1024 * 1024


def _cparams(*sem):
    return pltpu.CompilerParams(dimension_semantics=sem, vmem_limit_bytes=VMEM_LIMIT)


def _nt(a, b, precision=None):
    return lax.dot_general(a, b, (((1,), (1,)), ((), ())), precision=precision,
                           preferred_element_type=F32)


def _tn(a, b, precision=None):
    return lax.dot_general(a, b, (((0,), (0,)), ((), ())), precision=precision,
                           preferred_element_type=F32)


def _mm(a, b, precision=None):
    return jnp.dot(a, b, precision=precision, preferred_element_type=F32)


def _split2(x):
    hi = x.astype(BF16)
    return hi, (x - hi.astype(F32)).astype(BF16)


def _split3(x):
    hi = x.astype(BF16)
    rest = x - hi.astype(F32)
    mid = rest.astype(BF16)
    return hi, mid, (rest - mid.astype(F32)).astype(BF16)


def _mm_3pass(a, b_hi, b_lo):
    a_hi, a_lo = _split2(a)
    return _mm(a_hi, b_hi) + (_mm(a_lo, b_hi) + _mm(a_hi, b_lo))


def _mm_exact_lhs(m_bf16, x):
    hi, mid, lo = _split3(x)
    return _mm(m_bf16, hi) + (_mm(m_bf16, mid) + _mm(m_bf16, lo))


def _mm_exact_rhs(x, m_bf16):
    hi, mid, lo = _split3(x)
    return _mm(hi, m_bf16) + (_mm(mid, m_bf16) + _mm(lo, m_bf16))


def _sigmoid(x):
    return 1.0 / (1.0 + jnp.exp(-x))


def _rms(x, w):
    ms = jnp.mean(x * x, axis=-1, keepdims=True)
    return x * lax.rsqrt(ms + RMS_EPS) * w


def _ada_body(c_ref, w_ref, b_ref, o_ref):
    x = c_ref[...]
    x = x * _sigmoid(x)
    o_ref[0] = _mm(x, w_ref[0], HI) + b_ref[0]


def _ada(cond, ada_w, ada_b):
    depth, d, n = ada_w.shape
    rows = cond.shape[0]
    tn = 512
    return pl.pallas_call(
        _ada_body,
        grid=(depth, n // tn),
        in_specs=[pl.BlockSpec((rows, d), lambda l, j: (0, 0)),
                  pl.BlockSpec((1, d, tn), lambda l, j: (l, 0, j)),
                  pl.BlockSpec((1, 1, tn), lambda l, j: (l, 0, j))],
        out_specs=pl.BlockSpec((1, rows, tn), lambda l, j: (l, 0, j)),
        out_shape=jax.ShapeDtypeStruct((depth, rows, n), F32),
        compiler_params=_cparams("parallel", "parallel"),
        name="ada_mod",
    )(cond, ada_w, ada_b.reshape(depth, 1, n))


def _in_proj_body(tmod_ref, h_ref, mod_ref, nw_ref, w_ref, o_ref):
    del tmod_ref
    mod = mod_ref[0]
    u = _rms(h_ref[...], nw_ref[...]) * (1.0 + mod[1:2]) + mod[0:1]
    o_ref[...] = _mm(u.astype(BF16), w_ref[...])


def _in_proj(tile_mod, h, mod, norm_w, w):
    n_tok, d = h.shape
    tm = TOKEN_TILE
    grid_spec = pltpu.PrefetchScalarGridSpec(
        num_scalar_prefetch=1,
        grid=(n_tok // tm,),
        in_specs=[pl.BlockSpec((tm, d), lambda i, t: (i, 0)),
                  pl.BlockSpec((1, 6, d), lambda i, t: (t[i], 0, 0)),
                  pl.BlockSpec((1, d), lambda i, t: (0, 0)),
                  pl.BlockSpec((d, PROJ_WIDTH), lambda i, t: (0, 0))],
        out_specs=pl.BlockSpec((tm, PROJ_WIDTH), lambda i, t: (i, 0)))
    return pl.pallas_call(
        _in_proj_body, grid_spec=grid_spec,
        out_shape=jax.ShapeDtypeStruct((n_tok, PROJ_WIDTH), F32),
        compiler_params=_cparams("parallel"),
        name="in_proj",
    )(tile_mod, h, mod, norm_w.reshape(1, d), w)


def _rope_apply(x, cos, sin_signed):
    width = x.shape[-1]
    quarter = HEAD_DIM // 4
    up = pltpu.roll(x, width - quarter, 1)
    down = pltpu.roll(x, quarter, 1)
    lane = lax.broadcasted_iota(jnp.int32, x.shape, 1)
    first = (lane % (2 * quarter)) < quarter
    return x * cos + jnp.where(first, up, down) * sin_signed


def _prep_body(*refs, rope):
    if rope:
        (q_ref, k_ref, v_ref, qnw_ref, knw_ref, bd_ref, cq_ref, sq_ref, ck_ref, sk_ref,
         qo_ref, ko_ref, vo_ref) = refs
    else:
        q_ref, k_ref, v_ref, qnw_ref, knw_ref, bd_ref, qo_ref, ko_ref, vo_ref, kn_ref = refs
    bd = bd_ref[...]
    q = q_ref[...]
    k = k_ref[...]
    qn = q * lax.rsqrt(_mm(q * q, bd, HI) + RMS_EPS) * qnw_ref[...]
    kn = k * lax.rsqrt(_mm(k * k, bd[:KV_WIDTH, :KV_WIDTH], HI) + RMS_EPS) * knw_ref[...]
    if rope:
        qn = _rope_apply(qn, cq_ref[...], sq_ref[...])
        kr = _rope_apply(kn, ck_ref[...], sk_ref[...])
    else:
        kn_ref[...] = kn
        kr = kn
    qo_ref[...] = (qn * (HEAD_DIM ** -0.5 * math.log2(math.e))).T.astype(BF16)
    ko_ref[...] = kr.astype(BF16)
    vo_ref[...] = v_ref[...].T.astype(BF16)


def _attn_prep(proj, tile_off, n_tok, q_norm_w, k_norm_w, rope_tabs, tiles_per_seq):
    tm = TOKEN_TILE
    rope = rope_tabs is not None
    head_of = np.arange(ATTN_WIDTH) // HEAD_DIM
    bd = jnp.asarray((head_of[:, None] == head_of[None, :]).astype(np.float32) / HEAD_DIM)
    qnw = jnp.tile(q_norm_w, ATTN_HEADS).reshape(1, ATTN_WIDTH)
    knw = jnp.tile(k_norm_w, ATTN_KV_HEADS).reshape(1, KV_WIDTH)
    in_specs = [pl.BlockSpec((tm, ATTN_WIDTH), lambda i: (i + tile_off, COL_Q // ATTN_WIDTH)),
                pl.BlockSpec((tm, KV_WIDTH), lambda i: (i + tile_off, COL_K // KV_WIDTH)),
                pl.BlockSpec((tm, KV_WIDTH), lambda i: (i + tile_off, COL_V // KV_WIDTH)),
                pl.BlockSpec((1, ATTN_WIDTH), lambda i: (0, 0)),
                pl.BlockSpec((1, KV_WIDTH), lambda i: (0, 0)),
                pl.BlockSpec((ATTN_WIDTH, ATTN_WIDTH), lambda i: (0, 0))]
    args = [proj, proj, proj, qnw, knw, bd]
    out_specs = [pl.BlockSpec((ATTN_WIDTH, tm), lambda i: (0, i)),
                 pl.BlockSpec((tm, KV_WIDTH), lambda i: (i, 0)),
                 pl.BlockSpec((KV_WIDTH, tm), lambda i: (0, i))]
    out_shape = [jax.ShapeDtypeStruct((ATTN_WIDTH, n_tok), BF16),
                 jax.ShapeDtypeStruct((n_tok, KV_WIDTH), BF16),
                 jax.ShapeDtypeStruct((KV_WIDTH, n_tok), BF16)]
    if rope:
        cq, sq, ck, sk = rope_tabs
        in_specs += [pl.BlockSpec((tm, ATTN_WIDTH), lambda i: (i % tiles_per_seq, 0)),
                     pl.BlockSpec((tm, ATTN_WIDTH), lambda i: (i % tiles_per_seq, 0)),
                     pl.BlockSpec((tm, KV_WIDTH), lambda i: (i % tiles_per_seq, 0)),
                     pl.BlockSpec((tm, KV_WIDTH), lambda i: (i % tiles_per_seq, 0))]
        args += [cq, sq, ck, sk]
    else:
        out_specs.append(pl.BlockSpec((tm, KV_WIDTH), lambda i: (i, 0)))
        out_shape.append(jax.ShapeDtypeStruct((n_tok, KV_WIDTH), F32))
    return pl.pallas_call(
        functools.partial(_prep_body, rope=rope),
        grid=(n_tok // tm,), in_specs=in_specs, out_specs=out_specs, out_shape=out_shape,
        compiler_params=_cparams("parallel"),
        name="attn_prep_rope" if rope else "attn_prep",
    )(*args)


def _rope_tables(seq_len):
    quarter = HEAD_DIM // 4
    half = HEAD_DIM // 2
    t = np.arange(seq_len)
    pos = np.stack([t // GRID_W, t % GRID_W], axis=1).astype(np.float32)
    inv_freq = jnp.power(ROPE_THETA, -jnp.arange(0, half, 2, dtype=F32) / half)
    d = np.arange(HEAD_DIM)
    which = d // half
    freq = inv_freq[d % quarter]
    ang = jnp.asarray(pos)[:, which] * freq[None, :]
    sign = np.where((d % half) < quarter, -1.0, 1.0).astype(np.float32)
    cos, sin = jnp.cos(ang), jnp.sin(ang) * sign[None, :]
    return (jnp.tile(cos, (1, ATTN_HEADS)), jnp.tile(sin, (1, ATTN_HEADS)),
            jnp.tile(cos, (1, ATTN_KV_HEADS)), jnp.tile(sin, (1, ATTN_KV_HEADS)))


def _attn_body(qt_ref, k_ref, vt_ref, o_ref, m_ref, l_ref, acc_ref, *, nk):
    ki = pl.program_id(2)

    @pl.when(ki == 0)
    def _():
        m_ref[...] = jnp.full(m_ref.shape, -jnp.inf, F32)
        l_ref[...] = jnp.zeros(l_ref.shape, F32)
        acc_ref[...] = jnp.zeros(acc_ref.shape, F32)

    groups = ATTN_HEADS // ATTN_KV_HEADS
    m_all = m_ref[...]
    l_all = l_ref[...]
    k_all = k_ref[0]
    vt_all = vt_ref[0]
    scores, probs, new_m, new_l, alphas, pvs = {}, {}, {}, {}, {}, {}

    def stage_scores(h):
        kv = h // groups
        qt = qt_ref[h * HEAD_DIM:(h + 1) * HEAD_DIM, :]
        scores[h] = _mm(k_all[:, kv * HEAD_DIM:(kv + 1) * HEAD_DIM], qt)

    def stage_softmax(h):
        st = scores.pop(h)
        m_prev = m_all[h]
        m_new = jnp.maximum(m_prev, jnp.max(st, axis=0, keepdims=True))
        alphas[h] = jnp.exp2(m_prev - m_new)
        pt = jnp.exp2(st - m_new)
        new_l[h] = alphas[h] * l_all[h] + jnp.sum(pt, axis=0, keepdims=True)
        new_m[h] = m_new
        probs[h] = pt.astype(BF16)

    def stage_values(h):
        kv = h // groups
        pvs[h] = _mm(vt_all[kv * HEAD_DIM:(kv + 1) * HEAD_DIM, :], probs.pop(h))

    for step in range(ATTN_HEADS + 2):
        if step < ATTN_HEADS:
            stage_scores(step)
        if 0 <= step - 1 < ATTN_HEADS:
            stage_softmax(step - 1)
        if 0 <= step - 2 < ATTN_HEADS:
            stage_values(step - 2)
    for h in range(ATTN_HEADS):
        acc_ref[h] = alphas[h] * acc_ref[h] + pvs[h]
        m_ref[h] = new_m[h]
        l_ref[h] = new_l[h]

    @pl.when(ki == nk - 1)
    def _():
        out = [acc_ref[h] / l_ref[h] for h in range(ATTN_HEADS)]
        o_ref[...] = jnp.concatenate(out, axis=0).astype(o_ref.dtype)


def _key_tile(n_keys):
    tk = min(n_keys, ATTN_K_TILE_MAX)
    while n_keys % tk:
        tk -= LANES
    return tk


def _attention(qt, k, vt, seq_len):
    b, s, _ = k.shape
    tq = ATTN_Q_TILE
    tk = _key_tile(s)
    nq = seq_len // tq
    nk = s // tk
    return pl.pallas_call(
        functools.partial(_attn_body, nk=nk),
        grid=(b, nq, nk),
        in_specs=[pl.BlockSpec((ATTN_WIDTH, tq), lambda bi, qi, ki: (0, bi * nq + qi)),
                  pl.BlockSpec((1, tk, KV_WIDTH), lambda bi, qi, ki: (bi, ki, 0)),
                  pl.BlockSpec((1, KV_WIDTH, tk), lambda bi, qi, ki: (bi, 0, ki))],
        out_specs=pl.BlockSpec((ATTN_WIDTH, tq), lambda bi, qi, ki: (0, bi * nq + qi)),
        out_shape=jax.ShapeDtypeStruct((ATTN_WIDTH, b * seq_len), BF16),
        scratch_shapes=[pltpu.VMEM((ATTN_HEADS, 1, tq), F32),
                        pltpu.VMEM((ATTN_HEADS, 1, tq), F32),
                        pltpu.VMEM((ATTN_HEADS, HEAD_DIM, tq), F32)],
        compiler_params=_cparams("parallel", "parallel", "arbitrary"),
        name="attention",
    )(qt, k, vt)


def _conv_body(first_ref, last_ref, cur_ref, prev_ref, next_ref, w_ref, b_ref, o_ref):
    i = pl.program_id(0)
    x = cur_ref[...]
    tm = x.shape[0]
    prev = jnp.where(first_ref[i] == 1, 0.0, prev_ref[...])
    nxt = jnp.where(last_ref[i] == 1, 0.0, next_ref[...])
    row8 = lax.broadcasted_iota(jnp.int32, prev.shape, 0)
    w = w_ref[...]
    half = SSD_CONV // 2
    acc = b_ref[...] + w[half:half + 1] * x
    for s in range(1, half + 1):
        rolled = pltpu.roll(x, s, 0)
        top = jnp.where(row8 < s, pltpu.roll(prev, s, 0), rolled[:8])
        acc = acc + w[half - s:half - s + 1] * jnp.concatenate([top, rolled[8:]], axis=0)
        rolled = pltpu.roll(x, tm - s, 0)
        bot = jnp.where(row8 >= 8 - s, pltpu.roll(nxt, 8 - s, 0), rolled[tm - 8:])
        acc = acc + w[half + s:half + s + 1] * jnp.concatenate([rolled[:tm - 8], bot], axis=0)
    o_ref[...] = acc * _sigmoid(acc)


def _ssd_conv(first, last, proj, conv_w, conv_b):
    n_tok = proj.shape[0]
    tm = TOKEN_TILE
    sub = tm // 8
    col = COL_XBC // XBC_WIDTH
    w8 = jnp.concatenate([conv_w, jnp.zeros((8 - SSD_CONV, XBC_WIDTH), F32)], axis=0)
    grid_spec = pltpu.PrefetchScalarGridSpec(
        num_scalar_prefetch=2,
        grid=(n_tok // tm,),
        in_specs=[pl.BlockSpec((tm, XBC_WIDTH), lambda i, f, l: (i, col)),
                  pl.BlockSpec((8, XBC_WIDTH), lambda i, f, l: (jnp.maximum(i * sub - 1, 0), col)),
                  pl.BlockSpec((8, XBC_WIDTH),
                               lambda i, f, l: (jnp.minimum((i + 1) * sub, n_tok // 8 - 1), col)),
                  pl.BlockSpec((8, XBC_WIDTH), lambda i, f, l: (0, 0)),
                  pl.BlockSpec((1, XBC_WIDTH), lambda i, f, l: (0, 0))],
        out_specs=pl.BlockSpec((tm, XBC_WIDTH), lambda i, f, l: (i, 0)))
    return pl.pallas_call(
        _conv_body, grid_spec=grid_spec,
        out_shape=jax.ShapeDtypeStruct((n_tok, XBC_WIDTH), F32),
        compiler_params=_cparams("parallel"),
        name="ssd_conv",
    )(first, last, proj, proj, proj, w8, conv_b.reshape(1, XBC_WIDTH))


def _softplus(x):
    return jnp.maximum(x, 0.0) + jnp.log1p(jnp.exp(-jnp.abs(x)))


def _ssd_body(*refs, nc, has_init):
    if has_init:
        (xs_ref, bm_ref, cm_ref, dt_ref, z_ref, dtb_ref, an_ref, drow_ref, nw_ref, h0_ref,
         y_ref, fin_ref, y_scr, h_scr) = refs
    else:
        (xs_ref, bm_ref, cm_ref, dt_ref, z_ref, dtb_ref, an_ref, drow_ref, nw_ref,
         y_ref, fin_ref, y_scr, h_scr) = refs
    sweep_id = pl.program_id(1)
    c = pl.program_id(2)
    chunk_len = SSD_CHUNK
    rep = SSD_HEADS // SSD_GROUPS

    @pl.when(c == 0)
    def _():
        if has_init:
            h_scr[...] = h0_ref[0, 0]
        else:
            h_scr[...] = jnp.zeros(h_scr.shape, F32)

    def sweep(direction):
        chunk = c if direction == 0 else nc - 1 - c
        row0 = pl.multiple_of(chunk * chunk_len, chunk_len)
        xs = xs_ref[...]
        bm = bm_ref[...]
        cm = cm_ref[...]
        dt_all = _softplus(dt_ref[...] + dtb_ref[...])
        la_all = dt_all * an_ref[...]
        ti = lax.broadcasted_iota(jnp.int32, (chunk_len, chunk_len), 0)
        si = lax.broadcasted_iota(jnp.int32, (chunk_len, chunk_len), 1)
        before_eq = (si <= ti) if direction == 0 else (si >= ti)
        cs_all = _mm_exact_lhs(jnp.where(before_eq, 1.0, 0.0).astype(BF16), la_all)
        cs_t = cs_all.T
        last = chunk_len - 1 if direction == 0 else 0
        b_g = [bm[:, g * SSD_STATE:(g + 1) * SSD_STATE].astype(BF16) for g in range(SSD_GROUPS)]
        c_g = [cm[:, g * SSD_STATE:(g + 1) * SSD_STATE].astype(BF16) for g in range(SSD_GROUPS)]
        cb = [_nt(c_g[g], b_g[g]) for g in range(SSD_GROUPS)]
        states = [h_scr[h] for h in range(SSD_HEADS)]
        y_off = [_nt(c_g[h // rep], states[h].astype(BF16)) for h in range(SSD_HEADS)]
        cs_c, xd, scores = [], [], []
        for h in range(SSD_HEADS):
            j = direction * SSD_HEADS + h
            cs_c.append(cs_all[:, j:j + 1])
            decay = jnp.exp(jnp.where(before_eq, cs_c[h] - cs_t[j:j + 1, :], -jnp.inf))
            xd.append(xs[:, h * HEAD_DIM:(h + 1) * HEAD_DIM] * dt_all[:, j:j + 1])
            scores.append((cb[h // rep] * decay).astype(BF16))
        y_diag = [_mm(scores[h], xd[h].astype(BF16)) for h in range(SSD_HEADS)]
        heads, tots, to_state = [], [], []
        for h in range(SSD_HEADS):
            j = direction * SSD_HEADS + h
            tots.append(cs_all[last:last + 1, j:j + 1])
            to_end = jnp.exp(tots[h] - cs_c[h])
            to_state.append(_tn((xd[h] * to_end).astype(BF16), b_g[h // rep]))
            heads.append(y_diag[h] + y_off[h] * jnp.exp(cs_c[h]))
        for h in range(SSD_HEADS):
            h_scr[h] = jnp.exp(tots[h]) * states[h] + to_state[h]
        y_dir = jnp.concatenate(heads, axis=-1)
        if direction == 0:
            y_scr[pl.ds(row0, chunk_len), :] = y_dir + drow_ref[...] * xs
        else:
            y = y_scr[pl.ds(row0, chunk_len), :] + y_dir
            z = z_ref[...]
            y = y * (z * _sigmoid(z))
            y_ref[...] = _rms(y, nw_ref[...]).astype(y_ref.dtype)

    @pl.when(sweep_id == 0)
    def _():
        sweep(0)

    @pl.when(sweep_id == 1)
    def _():
        sweep(1)

    @pl.when(c == nc - 1)
    def _():
        fin_ref[0, 0] = h_scr[...]


def _ssd(proj, xbc, tile_off, batch, seq_len, dt_bias, a_log, d_skip, norm_w, init):
    chunk_len = SSD_CHUNK
    nc = seq_len // chunk_len
    base = tile_off * (TOKEN_TILE // chunk_len)
    has_init = init is not None

    def row(b, s, c):
        return base + b * nc + c + s * (nc - 1 - 2 * c)

    def z_row(b, s, c):
        return base + b * nc + s * (nc - 1 - c) + (1 - s) * (nc - 1)

    def y_row(b, s, c):
        return b * nc + s * (nc - 1 - c) + (1 - s) * (nc - 1)

    dtb = jnp.zeros((1, LANES), F32).at[0, :2 * SSD_HEADS].set(dt_bias.reshape(-1))
    a_neg = jnp.zeros((1, LANES), F32).at[0, :2 * SSD_HEADS].set(-jnp.exp(a_log.reshape(-1)))
    d_row = jnp.repeat(d_skip, HEAD_DIM).reshape(1, SSD_WIDTH)
    in_specs = [
        pl.BlockSpec((chunk_len, SSD_WIDTH), lambda b, s, c: (row(b, s, c), 0)),
        pl.BlockSpec((chunk_len, LANES), lambda b, s, c: (row(b, s, c), SSD_WIDTH // LANES)),
        pl.BlockSpec((chunk_len, LANES), lambda b, s, c: (row(b, s, c), SSD_WIDTH // LANES + 1)),
        pl.BlockSpec((chunk_len, LANES), lambda b, s, c: (row(b, s, c), COL_DT // LANES)),
        pl.BlockSpec((chunk_len, SSD_WIDTH), lambda b, s, c: (z_row(b, s, c), COL_Z // SSD_WIDTH)),
        pl.BlockSpec((1, LANES), lambda b, s, c: (0, 0)),
        pl.BlockSpec((1, LANES), lambda b, s, c: (0, 0)),
        pl.BlockSpec((1, SSD_WIDTH), lambda b, s, c: (0, 0)),
        pl.BlockSpec((1, SSD_WIDTH), lambda b, s, c: (0, 0)),
    ]
    args = [xbc, xbc, xbc, proj, proj, dtb, a_neg, d_row, norm_w.reshape(1, SSD_WIDTH)]
    state_block = (1, 1, SSD_HEADS, HEAD_DIM, SSD_STATE)
    if has_init:
        in_specs.append(pl.BlockSpec(state_block, lambda b, s, c: (b, s, 0, 0, 0)))
        args.append(init)
    return pl.pallas_call(
        functools.partial(_ssd_body, nc=nc, has_init=has_init),
        grid=(batch, 2, nc),
        in_specs=in_specs,
        out_specs=[pl.BlockSpec((chunk_len, SSD_WIDTH), lambda b, s, c: (y_row(b, s, c), 0)),
                   pl.BlockSpec(state_block, lambda b, s, c: (b, s, 0, 0, 0))],
        out_shape=[jax.ShapeDtypeStruct((batch * seq_len, SSD_WIDTH), BF16),
                   jax.ShapeDtypeStruct((batch, 2, SSD_HEADS, HEAD_DIM, SSD_STATE), F32)],
        scratch_shapes=[pltpu.VMEM((seq_len, SSD_WIDTH), F32),
                        pltpu.VMEM((SSD_HEADS, HEAD_DIM, SSD_STATE), F32)],
        compiler_params=_cparams("parallel", "arbitrary", "arbitrary"),
        name="ssd",
    )(*args)


def _rwkv_body(*refs, nc, seq_len, has_init):
    n_io = 10 * RWKV_SEQS
    io_refs = [refs[5 * i:5 * i + 5] for i in range(2 * RWKV_SEQS)]
    (w0_ref, wuph_ref, wupl_ref, a0_ref, auph_ref, aupl_ref, gup_ref, kk_ref, ka_ref, rk_ref,
     lnw_ref, lnb_ref, heads_ref) = refs[n_io:n_io + 13]
    if has_init:
        s0_ref, y_ref, fin_ref, y_scr, s_scr = refs[n_io + 13:]
    else:
        y_ref, fin_ref, y_scr, s_scr = refs[n_io + 13:]

    def head_sum(x):
        return _mm_exact_rhs(x, heads_ref[...])
    c = pl.program_id(1)
    cl = RWKV_CHUNK
    n = HEAD_DIM
    lora = LANES // 2

    @pl.when(c == 0)
    def _():
        if has_init:
            s_scr[...] = s0_ref[...]
        else:
            s_scr[...] = jnp.zeros(s_scr.shape, F32)

    ti = lax.broadcasted_iota(jnp.int32, (cl, cl), 0)
    si = lax.broadcasted_iota(jnp.int32, (cl, cl), 1)
    lane2 = lax.broadcasted_iota(jnp.int32, (cl, 2 * cl), 1)
    left = lane2 < cl
    sign2 = jnp.where(left, 1.0, -1.0)
    t2 = lax.broadcasted_iota(jnp.int32, (cl, 2 * cl), 0)
    s2 = jnp.where(left, lane2, lane2 - cl)

    chains = []
    for idx, (r_ref, k_ref, v_ref, walo_ref, _) in enumerate(io_refs):
        seq, direction = divmod(idx, 2)
        r = r_ref[...]
        k = k_ref[...]
        v = v_ref[...]
        walo = walo_ref[...]
        zw = w0_ref[direction:direction + 1] + _mm_3pass(
            jnp.tanh(walo[:, :lora]), wuph_ref[direction], wupl_ref[direction])
        lw = -math.exp(-0.5) * _sigmoid(zw)
        a = _sigmoid(a0_ref[direction:direction + 1] + _mm_3pass(
            walo[:, lora:], auph_ref[direction], aupl_ref[direction]))
        kd = k * (1.0 + (a - 1.0) * ka_ref[...])
        kk_raw = k * kk_ref[...]
        kk_all = kk_raw * lax.rsqrt(head_sum(kk_raw * kk_raw) + 1e-12)
        akk_all = a * kk_all
        before_eq = (si <= ti) if direction == 0 else (si >= ti)
        before = (si < ti) if direction == 0 else (si > ti)
        before_eq2 = (s2 <= t2) if direction == 0 else (s2 >= t2)
        lp_all = _mm_exact_lhs(jnp.where(before_eq, 1.0, 0.0).astype(BF16), lw)
        last = cl - 1 if direction == 0 else 0
        lp_last = lp_all[last:last + 1]
        dec_in = jnp.exp(lp_all)
        dec_out = jnp.exp(-lp_all)
        dec_before = jnp.exp(lp_all - lw)
        to_end = jnp.exp(lp_last - lp_all)
        end_decay = jnp.exp(lp_last)
        kap_all = (kk_all * dec_before).astype(BF16)
        rt_all = (r * dec_in).astype(BF16)
        kt_all = (kd * dec_out).astype(BF16)
        bt_all = (akk_all * dec_out).astype(BF16)
        kend_all = (kd * to_end).astype(BF16)
        bend_all = (akk_all * to_end).astype(BF16)
        for h in range(RWKV_HEADS):
            sl = slice(h * n, (h + 1) * n)
            chains.append(dict(
                where=(seq, direction, h), v=v[:, sl],
                kap_rt=jnp.concatenate([kap_all[:, sl], rt_all[:, sl]], axis=0),
                kt_bt=jnp.concatenate([kt_all[:, sl], bt_all[:, sl]], axis=0),
                state=s_scr[seq, direction, h], before=before, before_eq2=before_eq2,
                end_decay=end_decay[:, sl],
                kend=jnp.concatenate([kend_all[:, sl], bend_all[:, sl]], axis=0)))

    for ch in chains:
        ch['g'] = _nt(ch['kap_rt'], ch['kt_bt'])
    for ch in chains:
        ch['fs'] = _nt(ch['kap_rt'], ch['state'].astype(BF16))
    for ch in chains:
        g = ch.pop('g')
        ch['n_mat'] = jnp.where(ch['before'], g[:cl, :cl], 0.0).astype(BF16)
        ch['l_mat'] = jnp.where(ch['before'], g[:cl, cl:], 0.0)
        ch['rkb'] = (jnp.where(ch['before_eq2'], g[cl:], 0.0) * sign2).astype(BF16)
    for ch in chains:
        x0 = ch['fs'][:cl] + _mm(ch.pop('n_mat'), ch['v'].astype(BF16))
        ch['z'] = jnp.concatenate([-ch.pop('l_mat'), x0], axis=1)

    for _ in range(int(math.log2(cl))):
        for ch in chains:
            z = ch['z']
            pz = _mm(z[:, :cl].astype(BF16), z.astype(BF16))
            ch['z'] = jnp.where(left, pz, z + pz)

    for ch in chains:
        u = ch['z'][:, cl:]
        ch['y'] = ch['fs'][cl:] + _mm(ch['rkb'],
                                      jnp.concatenate([ch['v'], u], axis=0).astype(BF16))
    for ch in chains:
        seq, direction, h = ch['where']
        v_nu = jnp.concatenate([ch['v'], -ch['z'][:, cl:]], axis=0).astype(BF16)
        s_scr[seq, direction, h] = ch['state'] * ch['end_decay'] + _tn(v_nu, ch['kend'])

    def finish(y, refs_):
        r_ref, k_ref, v_ref, _, glo_ref = refs_
        r = r_ref[...]
        v = v_ref[...]
        gate = _mm(_sigmoid(glo_ref[...]).astype(BF16), gup_ref[...])
        bonus = head_sum(r * k_ref[...] * rk_ref[...]) * v
        centred = y - head_sum(y) * (1.0 / n)
        var = head_sum(centred * centred) * (1.0 / n)
        y_n = centred * lax.rsqrt(var + GN_EPS)
        return ((y_n * lnw_ref[...] + lnb_ref[...] + bonus) * gate).astype(y_ref.dtype)

    ys, rows = [], []
    for idx in range(2 * RWKV_SEQS):
        seq, direction = divmod(idx, 2)
        ys.append(jnp.concatenate(
            [ch['y'] for ch in chains[idx * RWKV_HEADS:(idx + 1) * RWKV_HEADS]], axis=-1))
        chunk = c if direction == 0 else nc - 1 - c
        rows.append(pl.multiple_of(seq * seq_len + chunk * cl, cl))

    @pl.when(c < nc // 2)
    def _():
        for idx in range(2 * RWKV_SEQS):
            y_scr[pl.ds(rows[idx], cl), :] = ys[idx]

    @pl.when(c >= nc // 2)
    def _():
        for idx in range(2 * RWKV_SEQS):
            y_ref[pl.ds(rows[idx], cl), :] = finish(y_scr[pl.ds(rows[idx], cl), :] + ys[idx],
                                                    io_refs[idx])

    @pl.when(c == nc - 1)
    def _():
        fin_ref[...] = s_scr[...]


def _rwkv(proj, tile_off, batch, seq_len, lp, init):
    cl = RWKV_CHUNK
    nc = seq_len // cl
    assert nc % 2 == 0 and batch % RWKV_SEQS == 0
    base = tile_off * (TOKEN_TILE // cl)
    has_init = init is not None
    w = RWKV_WIDTH

    def const(shape):
        return pl.BlockSpec(shape, lambda b, c: (0,) * len(shape))

    def chunk_specs(seq, direction):
        def row(b, c):
            first = base + (b * RWKV_SEQS + seq) * nc
            return first + (c if direction == 0 else nc - 1 - c)
        return [pl.BlockSpec((cl, w), lambda b, c: (row(b, c), COL_R // w)),
                pl.BlockSpec((cl, w), lambda b, c: (row(b, c), COL_KR // w)),
                pl.BlockSpec((cl, w), lambda b, c: (row(b, c), COL_VR // w)),
                pl.BlockSpec((cl, LANES), lambda b, c: (row(b, c), COL_WALO // LANES)),
                pl.BlockSpec((cl, LANES), lambda b, c: (row(b, c), COL_GLO // LANES))]

    in_specs = []
    for seq in range(RWKV_SEQS):
        for direction in range(2):
            in_specs += chunk_specs(seq, direction)
    lora_shape = (2, LANES // 2, w)
    in_specs += [const((2, w)), const(lora_shape), const(lora_shape), const((2, w)),
                 const(lora_shape), const(lora_shape), const((LANES, w)),
                 const((1, w)), const((1, w)), const((1, w)), const((1, w)), const((1, w)),
                 const((w, w))]
    wup_hi, wup_lo = _split2(lp['rwkv_w_up'])
    aup_hi, aup_lo = _split2(lp['rwkv_a_up'])
    head_of = np.arange(w) // HEAD_DIM
    same_head = jnp.asarray((head_of[:, None] == head_of[None, :]).astype(np.float32), BF16)
    args = [proj] * (10 * RWKV_SEQS) + [
        lp['rwkv_w0'], wup_hi, wup_lo, lp['rwkv_a0'], aup_hi, aup_lo,
        lp['rwkv_g_up'].astype(BF16), lp['rwkv_k_k'].reshape(1, w), lp['rwkv_k_a'].reshape(1, w),
        lp['rwkv_r_k'].reshape(1, w), lp['rwkv_ln_w'].reshape(1, w),
        lp['rwkv_ln_b'].reshape(1, w), same_head]
    state_block = (RWKV_SEQS, 2, RWKV_HEADS, HEAD_DIM, HEAD_DIM)
    if has_init:
        in_specs.append(pl.BlockSpec(state_block, lambda b, c: (b, 0, 0, 0, 0)))
        args.append(init)
    return pl.pallas_call(
        functools.partial(_rwkv_body, nc=nc, seq_len=seq_len, has_init=has_init),
        grid=(batch // RWKV_SEQS, nc),
        in_specs=in_specs,
        out_specs=[pl.BlockSpec((RWKV_SEQS * seq_len, w), lambda b, c: (b, 0)),
                   pl.BlockSpec(state_block, lambda b, c: (b, 0, 0, 0, 0))],
        out_shape=[jax.ShapeDtypeStruct((batch * seq_len, w), BF16),
                   jax.ShapeDtypeStruct((batch, 2, RWKV_HEADS, HEAD_DIM, HEAD_DIM), F32)],
        scratch_shapes=[pltpu.VMEM((RWKV_SEQS * seq_len, w), F32),
                        pltpu.VMEM(state_block, F32)],
        compiler_params=_cparams("parallel", "arbitrary"),
        name="rwkv",
    )(*args)


def _out_proj_body(tmod_ref, attn_ref, ssd_ref, rwkv_ref, h_ref, mod_ref, nw_ref, w_ref, rwh_ref,
                   rwl_ref, rb_ref, h_out, u_out, lg_out):
    del tmod_ref
    mixed = (_tn(attn_ref[...], w_ref[:ATTN_WIDTH])
             + _mm(ssd_ref[...], w_ref[ATTN_WIDTH:ATTN_WIDTH + SSD_WIDTH])
             + _mm(rwkv_ref[...], w_ref[ATTN_WIDTH + SSD_WIDTH:]))
    mod = mod_ref[0]
    h = h_ref[...] + mod[2:3] * mixed
    h_out[...] = h
    u = _rms(h, nw_ref[...]) * (1.0 + mod[4:5]) + mod[3:4]
    u_out[...] = u
    lg_out[...] = _mm_3pass(u, rwh_ref[...], rwl_ref[...]) + rb_ref[...]


def _out_proj(tile_mod, attn_t, ssd, rwkv, h, mod, norm_w, w, router_w, router_b):
    n_tok, d = h.shape
    tm = TOKEN_TILE
    rw = jnp.zeros((d, LANES), F32).at[:, :N_EXPERTS].set(router_w)
    rw_hi, rw_lo = _split2(rw)
    rb = jnp.zeros((1, LANES), F32).at[0, :N_EXPERTS].set(router_b)
    grid_spec = pltpu.PrefetchScalarGridSpec(
        num_scalar_prefetch=1,
        grid=(n_tok // tm,),
        in_specs=[pl.BlockSpec((ATTN_WIDTH, tm), lambda i, t: (0, i)),
                  pl.BlockSpec((tm, SSD_WIDTH), lambda i, t: (i, 0)),
                  pl.BlockSpec((tm, RWKV_WIDTH), lambda i, t: (i, 0)),
                  pl.BlockSpec((tm, d), lambda i, t: (i, 0)),
                  pl.BlockSpec((1, 6, d), lambda i, t: (t[i], 0, 0)),
                  pl.BlockSpec((1, d), lambda i, t: (0, 0)),
                  pl.BlockSpec((d, d), lambda i, t: (0, 0)),
                  pl.BlockSpec((d, LANES), lambda i, t: (0, 0)),
                  pl.BlockSpec((d, LANES), lambda i, t: (0, 0)),
                  pl.BlockSpec((1, LANES), lambda i, t: (0, 0))],
        out_specs=[pl.BlockSpec((tm, d), lambda i, t: (i, 0)),
                   pl.BlockSpec((tm, d), lambda i, t: (i, 0)),
                   pl.BlockSpec((tm, LANES), lambda i, t: (i, 0))])
    return pl.pallas_call(
        _out_proj_body, grid_spec=grid_spec,
        out_shape=[jax.ShapeDtypeStruct((n_tok, d), F32),
                   jax.ShapeDtypeStruct((n_tok, d), F32),
                   jax.ShapeDtypeStruct((n_tok, LANES), F32)],
        compiler_params=_cparams("parallel"),
        name="out_proj",
    )(tile_mod, attn_t, ssd, rwkv, h, mod, norm_w.reshape(1, d), w, rw_hi, rw_lo, rb)


def _run_copies(i, start_ref, nch_ref, dst_ref, tot_ref, make_copy):
    def per_expert(e, carry):
        idx = i * N_EXPERTS + e
        start = start_ref[idx]
        dst = dst_ref[idx]

        def per_chunk(j, c2):
            make_copy(pl.multiple_of(start + j * RUN_CHUNK, RUN_CHUNK),
                      pl.multiple_of(dst + j * RUN_CHUNK, RUN_CHUNK)).start()
            return c2

        lax.fori_loop(0, nch_ref[idx], per_chunk, 0)
        return carry

    lax.fori_loop(0, N_EXPERTS, per_expert, 0)

    def wait_one(j, carry):
        make_copy(0, 0).wait()
        return carry

    lax.fori_loop(0, tot_ref[i], wait_one, 0)


def _dispatch_body(start_ref, nch_ref, dst_ref, tot_ref, colt_ref, u_ref, init_hbm, o_hbm,
                   buf, sem):
    del init_hbm
    i = pl.program_id(0)
    colt = colt_ref[0]
    stage_row = lax.broadcasted_iota(jnp.int32, (buf.shape[0], colt.shape[1]), 0)
    sel = jnp.where(stage_row == colt[0:1], 1.0, 0.0)
    for kk in range(1, TOP_K):
        sel = sel + jnp.where(stage_row == colt[kk:kk + 1], 1.0, 0.0)
    buf[...] = _mm(sel.astype(BF16), u_ref[...].astype(BF16))

    def make_copy(hbm_row, stage):
        return pltpu.make_async_copy(buf.at[pl.ds(stage, RUN_CHUNK)],
                                     o_hbm.at[pl.ds(hbm_row, RUN_CHUNK)], sem)

    _run_copies(i, start_ref, nch_ref, dst_ref, tot_ref, make_copy)


def _stage_rows(tm):
    return TOP_K * tm + N_EXPERTS * RUN_CHUNK


def _moe_dispatch(u, tables, n_rows):
    n_tok, d = u.shape
    tm = MOE_TOKEN_TILE
    start, nch, dst, tot, col = tables
    col_t = col.reshape(n_tok // tm, tm, TOP_K).transpose(0, 2, 1)
    grid_spec = pltpu.PrefetchScalarGridSpec(
        num_scalar_prefetch=4,
        grid=(n_tok // tm,),
        in_specs=[pl.BlockSpec((1, TOP_K, tm), lambda i, *_: (i, 0, 0)),
                  pl.BlockSpec((tm, d), lambda i, *_: (i, 0)),
                  pl.BlockSpec(memory_space=pl.ANY)],
        out_specs=pl.BlockSpec(memory_space=pl.ANY),
        scratch_shapes=[pltpu.VMEM((_stage_rows(tm), d), F32), pltpu.SemaphoreType.DMA])
    return pl.pallas_call(
        _dispatch_body, grid_spec=grid_spec,
        out_shape=jax.ShapeDtypeStruct((n_rows, d), u.dtype),
        input_output_aliases={6: 0},
        compiler_params=_cparams("arbitrary"),
        name="moe_dispatch",
    )(start, nch, dst, tot, col_t, u, jnp.zeros((n_rows, d), u.dtype))


def _moe_body(be_ref, nv_ref, new_ref, x_ref, wgu_ref, bgu_ref, wdn_ref, bdn_ref, o_ref,
              wgu_bf, wdn_bf):
    del be_ref
    i = pl.program_id(0)

    @pl.when(new_ref[i] == 1)
    def _():
        wgu_bf[...] = wgu_ref[0].astype(BF16)
        wdn_bf[...] = wdn_ref[0].astype(BF16)

    @pl.when(i < nv_ref[0])
    def _():
        gu = _mm(x_ref[...].astype(BF16), wgu_bf[...]) + bgu_ref[0]
        gate = jnp.minimum(gu[:, :D_EXPERT], SWIGLU_LIMIT)
        up = jnp.clip(gu[:, D_EXPERT:], -SWIGLU_LIMIT, SWIGLU_LIMIT)
        act = (up + 1.0) * gate * _sigmoid(SWIGLU_ALPHA * gate)
        o_ref[...] = _mm(act.astype(BF16), wdn_bf[...]) + bdn_ref[0]

    @pl.when(i >= nv_ref[0])
    def _():
        o_ref[...] = jnp.zeros(o_ref.shape, o_ref.dtype)


def _moe_experts(block_expert, n_valid, x_rows, w_gu, b_gu, w_dn, b_dn):
    n_rows, d = x_rows.shape
    rows = MOE_ROWS
    new_expert = jnp.concatenate([jnp.ones((1,), jnp.int32),
                                  (block_expert[1:] != block_expert[:-1]).astype(jnp.int32)])
    grid_spec = pltpu.PrefetchScalarGridSpec(
        num_scalar_prefetch=3,
        grid=(n_rows // rows,),
        in_specs=[pl.BlockSpec((rows, d), lambda i, be, nv, ne: (i, 0)),
                  pl.BlockSpec((1, d, 2 * D_EXPERT), lambda i, be, nv, ne: (be[i], 0, 0)),
                  pl.BlockSpec((1, 1, 2 * D_EXPERT), lambda i, be, nv, ne: (be[i], 0, 0)),
                  pl.BlockSpec((1, D_EXPERT, d), lambda i, be, nv, ne: (be[i], 0, 0)),
                  pl.BlockSpec((1, 1, d), lambda i, be, nv, ne: (be[i], 0, 0))],
        out_specs=pl.BlockSpec((rows, d), lambda i, be, nv, ne: (i, 0)),
        scratch_shapes=[pltpu.VMEM((d, 2 * D_EXPERT), BF16), pltpu.VMEM((D_EXPERT, d), BF16)])
    return pl.pallas_call(
        _moe_body, grid_spec=grid_spec,
        out_shape=jax.ShapeDtypeStruct((n_rows, d), F32),
        compiler_params=_cparams("arbitrary"),
        name="moe_experts",
    )(block_expert, n_valid, new_expert, x_rows, w_gu, b_gu.reshape(N_EXPERTS, 1, -1), w_dn,
      b_dn.reshape(N_EXPERTS, 1, -1))


def _combine_body(tmod_ref, start_ref, nch_ref, dst_ref, tot_ref, col_ref, gates_ref, h_ref,
                  mod_ref, fw_ref, y_hbm, o_ref, buf, sem, *, final):
    del tmod_ref
    i = pl.program_id(0)

    @pl.when(i == 0)
    def _():
        buf[...] = jnp.zeros(buf.shape, buf.dtype)

    def make_copy(hbm_row, stage):
        return pltpu.make_async_copy(y_hbm.at[pl.ds(hbm_row, RUN_CHUNK)],
                                     buf.at[pl.ds(stage, RUN_CHUNK)], sem)

    _run_copies(i, start_ref, nch_ref, dst_ref, tot_ref, make_copy)
    col = col_ref[...]
    gates = gates_ref[...]
    stage_row = lax.broadcasted_iota(jnp.int32, (col.shape[0], buf.shape[0]), 1)
    sel = jnp.where(stage_row == col[:, 0:1], gates[:, 0:1], 0.0)
    for kk in range(1, TOP_K):
        sel = sel + jnp.where(stage_row == col[:, kk:kk + 1], gates[:, kk:kk + 1], 0.0)
    sel_hi, sel_lo = _split2(sel)
    rows_bf = buf[...].astype(BF16)
    y = _mm(sel_hi, rows_bf) + _mm(sel_lo, rows_bf)
    h = h_ref[...] + mod_ref[0][5:6] * y
    if final:
        o_ref[...] = _rms(h, fw_ref[...])
    else:
        o_ref[...] = h


def _moe_combine(tile_mod, tables, gates, h, mod, final_w, y_rows, final):
    n_tok, d = h.shape
    tm = MOE_TOKEN_TILE
    ratio = TOKEN_TILE // tm
    start, nch, dst, tot, col = tables
    grid_spec = pltpu.PrefetchScalarGridSpec(
        num_scalar_prefetch=5,
        grid=(n_tok // tm,),
        in_specs=[pl.BlockSpec((tm, TOP_K), lambda i, *_: (i, 0)),
                  pl.BlockSpec((tm, TOP_K), lambda i, *_: (i, 0)),
                  pl.BlockSpec((tm, d), lambda i, *_: (i, 0)),
                  pl.BlockSpec((1, 6, d), lambda i, t, *_: (t[i // ratio], 0, 0)),
                  pl.BlockSpec((1, d), lambda i, *_: (0, 0)),
                  pl.BlockSpec(memory_space=pl.ANY)],
        out_specs=pl.BlockSpec((tm, d), lambda i, *_: (i, 0)),
        scratch_shapes=[pltpu.VMEM((_stage_rows(tm), d), F32), pltpu.SemaphoreType.DMA])
    return pl.pallas_call(
        functools.partial(_combine_body, final=final), grid_spec=grid_spec,
        out_shape=jax.ShapeDtypeStruct((n_tok, d), F32),
        compiler_params=_cparams("arbitrary"),
        name="moe_combine",
    )(tile_mod, start, nch, dst, tot, col, gates, h, mod, final_w.reshape(1, d), y_rows)


def _route(logits):
    n_tok = logits.shape[0]
    rows = MOE_ROWS
    tm = MOE_TOKEN_TILE
    tiles = n_tok // tm
    slack = tiles * N_EXPERTS * (RUN_CHUNK - 1)
    n_blocks = n_tok * TOP_K // rows + N_EXPERTS + (slack + rows - 1) // rows
    top_logit, top_idx = lax.top_k(logits, TOP_K)
    gates = jax.nn.softmax(top_logit, axis=-1)
    hit = top_idx[:, :, None] == jnp.arange(N_EXPERTS, dtype=top_idx.dtype)[None, None, :]
    per_tok = jnp.sum(hit, axis=1, dtype=jnp.int32)
    before = jnp.cumsum(per_tok, axis=0) - per_tok
    run_len = per_tok.reshape(tiles, tm, N_EXPERTS).sum(axis=1)
    n_chunks = (run_len + RUN_CHUNK - 1) // RUN_CHUNK
    run_rows = n_chunks * RUN_CHUNK
    rows_before = jnp.cumsum(run_rows, axis=0) - run_rows
    counts = rows_before[-1] + run_rows[-1]
    padded = (counts + rows - 1) // rows * rows
    padded_end = jnp.cumsum(padded)
    padded_start = padded_end - padded
    block_start = jnp.arange(n_blocks, dtype=jnp.int32) * rows
    block_expert = jnp.minimum(jnp.searchsorted(padded_end, block_start, side='right'),
                               N_EXPERTS - 1).astype(jnp.int32)
    n_valid = (padded_end[-1] // rows).astype(jnp.int32).reshape(1)
    run_start = padded_start[None, :] + rows_before
    stage_off = (jnp.cumsum(n_chunks, axis=1) - n_chunks) * RUN_CHUNK
    total_chunks = jnp.sum(n_chunks, axis=1)
    in_tile_rank = before.reshape(tiles, tm, N_EXPERTS) - before[::tm][:, None, :]
    stage_row = (stage_off[:, None, :] + in_tile_rank).reshape(n_tok, 1, N_EXPERTS)
    col = jnp.sum(jnp.where(hit, stage_row, 0), axis=-1)
    tables = (run_start.reshape(-1).astype(jnp.int32), n_chunks.reshape(-1).astype(jnp.int32),
              stage_off.reshape(-1).astype(jnp.int32), total_chunks.astype(jnp.int32),
              col.astype(jnp.int32))
    return tables, gates, block_expert, n_valid, n_blocks * rows


def _permute_w_in(w_in):
    def cols(a, b):
        return w_in[:, a:b]
    q, k, v, z = cols(0, 384), cols(384, 512), cols(512, 640), cols(640, 1024)
    xbc, dt = cols(1024, 1664), cols(1664, 1676)
    r, kr, vr = cols(1676, 1932), cols(1932, 2188), cols(2188, 2444)
    walo, glo = cols(2444, 2572), cols(2572, 2700)
    dt = jnp.pad(dt, ((0, 0), (0, LANES - dt.shape[1])))
    return jnp.concatenate([q, z, r, kr, k, v, dt, walo, glo, xbc, vr], axis=1).astype(BF16)


def kernel(x_prompt, x_sample, cache_attn_k, cache_attn_v, state_ssd, state_rwkv, c, c_ctx,
           norm1_w, norm2_w, ada_w, ada_b, w_in, q_norm_w, k_norm_w,
           ssd_conv_w, ssd_conv_b, ssd_dt_bias, ssd_a_log, ssd_d, ssd_norm_w,
           rwkv_w0, rwkv_w_up, rwkv_a0, rwkv_a_up, rwkv_g_up, rwkv_k_k, rwkv_k_a, rwkv_r_k,
           rwkv_ln_w, rwkv_ln_b, w_out, router_w, router_b,
           moe_w_gate_up, moe_b_gate_up, moe_w_down, moe_b_down, final_norm_w):
    batch, seq, d = x_prompt.shape
    dec_batch, dec_seq, _ = x_sample.shape
    depth = norm1_w.shape[0]
    past = cache_attn_k.shape[2]
    n_ctx, n_lat = batch * seq, dec_batch * dec_seq
    tm = TOKEN_TILE
    ctx_tiles, lat_tiles = n_ctx // tm, n_lat // tm

    tile_mod = np.concatenate([np.zeros(ctx_tiles, np.int32),
                               1 + np.arange(lat_tiles, dtype=np.int32) // (dec_seq // tm)])
    seq_tiles = np.concatenate([np.full(ctx_tiles, seq // tm), np.full(lat_tiles, dec_seq // tm)])
    tile_in_seq = np.concatenate([np.arange(ctx_tiles) % (seq // tm),
                                  np.arange(lat_tiles) % (dec_seq // tm)])
    first = jnp.asarray((tile_in_seq == 0).astype(np.int32))
    last = jnp.asarray((tile_in_seq == seq_tiles - 1).astype(np.int32))
    tile_mod = jnp.asarray(tile_mod)

    cond = jnp.concatenate([c_ctx[None, :], c], axis=0)
    pad_rows = -cond.shape[0] % 8
    cond = jnp.pad(cond, ((0, pad_rows), (0, 0)))
    mod_all = _ada(cond, ada_w, ada_b).reshape(depth, cond.shape[0], 6, d)

    rope_tabs = _rope_tables(dec_seq)
    h = jnp.concatenate([x_prompt.reshape(n_ctx, d), x_sample.reshape(n_lat, d)], axis=0)
    ks, vs, ssd_states, rwkv_states = [], [], [], []
    for l in range(depth):
        lp = dict(rwkv_w0=rwkv_w0[l], rwkv_w_up=rwkv_w_up[l], rwkv_a0=rwkv_a0[l],
                  rwkv_a_up=rwkv_a_up[l], rwkv_g_up=rwkv_g_up[l], rwkv_k_k=rwkv_k_k[l],
                  rwkv_k_a=rwkv_k_a[l], rwkv_r_k=rwkv_r_k[l], rwkv_ln_w=rwkv_ln_w[l],
                  rwkv_ln_b=rwkv_ln_b[l])
        mod = mod_all[l]
        proj = _in_proj(tile_mod, h, mod, norm1_w[l], _permute_w_in(w_in[l]))

        q_c, k_c, v_c, k_norm = _attn_prep(proj, 0, n_ctx, q_norm_w[l], k_norm_w[l], None, 1)
        q_l, k_l, v_l = _attn_prep(proj, ctx_tiles, n_lat, q_norm_w[l], k_norm_w[l], rope_tabs,
                                   dec_seq // tm)
        v_c = v_c.reshape(KV_WIDTH, batch, seq).transpose(1, 0, 2)
        attn_c = _attention(q_c, k_c.reshape(batch, seq, -1), v_c, seq)
        k_full = jnp.concatenate([cache_attn_k[:, l].reshape(dec_batch, past, -1).astype(BF16),
                                  k_l.reshape(dec_batch, dec_seq, -1)], axis=1)
        v_past = cache_attn_v[:, l].reshape(dec_batch, past, -1).astype(BF16).transpose(0, 2, 1)
        v_full = jnp.concatenate(
            [v_past, v_l.reshape(KV_WIDTH, dec_batch, dec_seq).transpose(1, 0, 2)], axis=2)
        attn_l = _attention(q_l, k_full, v_full, dec_seq)
        attn = jnp.concatenate([attn_c, attn_l], axis=1)
        ks.append(k_norm.reshape(batch, seq, ATTN_KV_HEADS, HEAD_DIM))
        vs.append(proj[:n_ctx, COL_V:COL_V + KV_WIDTH].reshape(batch, seq, ATTN_KV_HEADS, HEAD_DIM))

        xbc = _ssd_conv(first, last, proj, ssd_conv_w[l], ssd_conv_b[l])
        ssd_c, ssd_fin = _ssd(proj, xbc, 0, batch, seq, ssd_dt_bias[l], ssd_a_log[l], ssd_d[l],
                              ssd_norm_w[l], None)
        ssd_l, _ = _ssd(proj, xbc, ctx_tiles, dec_batch, dec_seq, ssd_dt_bias[l], ssd_a_log[l],
                        ssd_d[l], ssd_norm_w[l], state_ssd[:, l])
        ssd = jnp.concatenate([ssd_c, ssd_l], axis=0)
        ssd_states.append(ssd_fin)

        rwkv_c, rwkv_fin = _rwkv(proj, 0, batch, seq, lp, None)
        rwkv_l, _ = _rwkv(proj, ctx_tiles, dec_batch, dec_seq, lp, state_rwkv[:, l])
        rwkv = jnp.concatenate([rwkv_c, rwkv_l], axis=0)
        rwkv_states.append(rwkv_fin)

        h, u, logits = _out_proj(tile_mod, attn, ssd, rwkv, h, mod, norm2_w[l],
                                 w_out[l].astype(BF16), router_w[l], router_b[l])

        tables, gates, block_expert, n_valid, n_rows = _route(logits[:, :N_EXPERTS])
        x_rows = _moe_dispatch(u, tables, n_rows)
        y_rows = _moe_experts(block_expert, n_valid, x_rows, moe_w_gate_up[l], moe_b_gate_up[l],
                              moe_w_down[l], moe_b_down[l])
        h = _moe_combine(tile_mod, tables, gates, h, mod, final_norm_w, y_rows, l == depth - 1)

    y_prompt = h[:n_ctx].reshape(batch, seq, d)
    y_sample = h[n_ctx:].reshape(dec_batch, dec_seq, d)
    return (y_prompt, y_sample, jnp.stack(ks, axis=1), jnp.stack(vs, axis=1),
            jnp.stack(ssd_states, axis=1), jnp.stack(rwkv_states, axis=1))
```

```python
import functools
import math

import numpy as np
import jax
import jax.numpy as jnp
from jax import lax
from jax.experimental import pallas as pl
from jax.experimental.pallas import tpu as pltpu

F32 = jnp.float32
BF16 = jnp.bfloat16
HI = lax.Precision.HIGHEST

D_MODEL = 1024
GRID_W = 64
HEAD_DIM = 64
ATTN_HEADS = 6
ATTN_KV_HEADS = 2
ROPE_THETA = 10000.0
SSD_HEADS = 6
SSD_GROUPS = 2
SSD_STATE = 64
SSD_CONV = 5
SSD_CHUNK = 128
RWKV_HEADS = 4
RWKV_CHUNK = 64
N_EXPERTS = 32
TOP_K = 4
D_EXPERT = 1024
SWIGLU_LIMIT = 7.0
SWIGLU_ALPHA = 1.702
RMS_EPS = 1e-6
GN_EPS = 64e-5

ATTN_WIDTH = ATTN_HEADS * HEAD_DIM
KV_WIDTH = ATTN_KV_HEADS * HEAD_DIM
SSD_WIDTH = SSD_HEADS * HEAD_DIM
XBC_WIDTH = SSD_WIDTH + 2 * SSD_GROUPS * SSD_STATE
RWKV_WIDTH = RWKV_HEADS * HEAD_DIM
LANES = 128

COL_Q, COL_Z, COL_R, COL_KR = 0, 384, 768, 1024
COL_K, COL_V, COL_DT, COL_WALO, COL_GLO = 1280, 1408, 1536, 1664, 1792
COL_XBC, COL_VR = 1920, 2560
PROJ_WIDTH = 2816

TOKEN_TILE = 256
MOE_ROWS = 512
MOE_TOKEN_TILE = 256
RUN_CHUNK = 8
ATTN_Q_TILE = 256
ATTN_K_TILE_MAX = 1536
RWKV_SEQS = 2
VMEM_LIMIT = 56 * 1024 * 1024


def _cparams(*sem):
    return pltpu.CompilerParams(dimension_semantics=sem, vmem_limit_bytes=VMEM_LIMIT)


def _nt(a, b, precision=None):
    return lax.dot_general(a, b, (((1,), (1,)), ((), ())), precision=precision,
                           preferred_element_type=F32)


def _tn(a, b, precision=None):
    return lax.dot_general(a, b, (((0,), (0,)), ((), ())), precision=precision,
                           preferred_element_type=F32)


def _mm(a, b, precision=None):
    return jnp.dot(a, b, precision=precision, preferred_element_type=F32)


def _split2(x):
    hi = x.astype(BF16)
    return hi, (x - hi.astype(F32)).astype(BF16)


def _split3(x):
    hi = x.astype(BF16)
    rest = x - hi.astype(F32)
    mid = rest.astype(BF16)
    return hi, mid, (rest - mid.astype(F32)).astype(BF16)


def _mm_3pass(a, b_hi, b_lo):
    a_hi, a_lo = _split2(a)
    return _mm(a_hi, b_hi) + (_mm(a_lo, b_hi) + _mm(a_hi, b_lo))


def _mm_exact_lhs(m_bf16, x):
    hi, mid, lo = _split3(x)
    return _mm(m_bf16, hi) + (_mm(m_bf16, mid) + _mm(m_bf16, lo))


def _mm_exact_rhs(x, m_bf16):
    hi, mid, lo = _split3(x)
    return _mm(hi, m_bf16) + (_mm(mid, m_bf16) + _mm(lo, m_bf16))


def _sigmoid(x):
    return 1.0 / (1.0 + jnp.exp(-x))


def _rms(x, w):
    ms = jnp.mean(x * x, axis=-1, keepdims=True)
    return x * lax.rsqrt(ms + RMS_EPS) * w


def _ada_body(c_ref, w_ref, b_ref, o_ref):
    x = c_ref[...]
    x = x * _sigmoid(x)
    o_ref[0] = _mm(x, w_ref[0], HI) + b_ref[0]


def _ada(cond, ada_w, ada_b):
    depth, d, n = ada_w.shape
    rows = cond.shape[0]
    tn = 512
    return pl.pallas_call(
        _ada_body,
        grid=(depth, n // tn),
        in_specs=[pl.BlockSpec((rows, d), lambda l, j: (0, 0)),
                  pl.BlockSpec((1, d, tn), lambda l, j: (l, 0, j)),
                  pl.BlockSpec((1, 1, tn), lambda l, j: (l, 0, j))],
        out_specs=pl.BlockSpec((1, rows, tn), lambda l, j: (l, 0, j)),
        out_shape=jax.ShapeDtypeStruct((depth, rows, n), F32),
        compiler_params=_cparams("parallel", "parallel"),
        name="ada_mod",
    )(cond, ada_w, ada_b.reshape(depth, 1, n))


def _in_proj_body(tmod_ref, h_ref, mod_ref, nw_ref, w_ref, o_ref):
    del tmod_ref
    mod = mod_ref[0]
    u = _rms(h_ref[...], nw_ref[...]) * (1.0 + mod[1:2]) + mod[0:1]
    o_ref[...] = _mm(u.astype(BF16), w_ref[...])


def _in_proj_first_body(tmod_ref, hc_ref, hl_ref, mod_ref, nw_ref, w_ref, o_ref, h_ref, *,
                        ctx_tiles):
    del tmod_ref
    x = jnp.where(pl.program_id(0) < ctx_tiles, hc_ref[...], hl_ref[...])
    h_ref[...] = x
    mod = mod_ref[0]
    u = _rms(x, nw_ref[...]) * (1.0 + mod[1:2]) + mod[0:1]
    o_ref[...] = _mm(u.astype(BF16), w_ref[...])


def _in_proj(tile_mod, h, mod, norm_w, w):
    tm = TOKEN_TILE
    first = isinstance(h, tuple)
    if first:
        h_ctx, h_lat = h
        d = h_ctx.shape[1]
        ctx_tiles = h_ctx.shape[0] // tm
        n_tok = h_ctx.shape[0] + h_lat.shape[0]
        h_specs = [pl.BlockSpec((tm, d), lambda i, t: (jnp.minimum(i, ctx_tiles - 1), 0)),
                   pl.BlockSpec((tm, d), lambda i, t: (jnp.maximum(i - ctx_tiles, 0), 0))]
        h_args = [h_ctx, h_lat]
        body = functools.partial(_in_proj_first_body, ctx_tiles=ctx_tiles)
    else:
        n_tok, d = h.shape
        h_specs = [pl.BlockSpec((tm, d), lambda i, t: (i, 0))]
        h_args = [h]
        body = _in_proj_body
    out_specs = [pl.BlockSpec((tm, PROJ_WIDTH), lambda i, t: (i, 0))]
    out_shape = [jax.ShapeDtypeStruct((n_tok, PROJ_WIDTH), F32)]
    if first:
        out_specs.append(pl.BlockSpec((tm, d), lambda i, t: (i, 0)))
        out_shape.append(jax.ShapeDtypeStruct((n_tok, d), F32))
    grid_spec = pltpu.PrefetchScalarGridSpec(
        num_scalar_prefetch=1,
        grid=(n_tok // tm,),
        in_specs=h_specs + [pl.BlockSpec((1, 6, d), lambda i, t: (t[i], 0, 0)),
                            pl.BlockSpec((1, d), lambda i, t: (0, 0)),
                            pl.BlockSpec((d, PROJ_WIDTH), lambda i, t: (0, 0))],
        out_specs=out_specs)
    out = pl.pallas_call(
        body, grid_spec=grid_spec, out_shape=out_shape,
        compiler_params=_cparams("parallel"),
        name="in_proj_first" if first else "in_proj",
    )(tile_mod, *h_args, mod, norm_w.reshape(1, d), w)
    return (out[0], out[1]) if first else (out[0], h)


def _rope_apply(x, cos, sin_signed):
    width = x.shape[-1]
    quarter = HEAD_DIM // 4
    up = pltpu.roll(x, width - quarter, 1)
    down = pltpu.roll(x, quarter, 1)
    lane = lax.broadcasted_iota(jnp.int32, x.shape, 1)
    first = (lane % (2 * quarter)) < quarter
    return x * cos + jnp.where(first, up, down) * sin_signed


def _prep_body(*refs, rope):
    if rope:
        (q_ref, k_ref, v_ref, qnw_ref, knw_ref, bd_ref, cq_ref, sq_ref, ck_ref, sk_ref,
         qo_ref, ko_ref, vo_ref) = refs
    else:
        q_ref, k_ref, v_ref, qnw_ref, knw_ref, bd_ref, qo_ref, ko_ref, vo_ref, kn_ref = refs
    bd = bd_ref[...]
    q = q_ref[...]
    k = k_ref[...]
    qn = q * lax.rsqrt(_mm(q * q, bd, HI) + RMS_EPS) * qnw_ref[...]
    kn = k * lax.rsqrt(_mm(k * k, bd[:KV_WIDTH, :KV_WIDTH], HI) + RMS_EPS) * knw_ref[...]
    if rope:
        qn = _rope_apply(qn, cq_ref[...], sq_ref[...])
        kr = _rope_apply(kn, ck_ref[...], sk_ref[...])
    else:
        kn_ref[...] = kn
        kr = kn
    qo_ref[...] = (qn * (HEAD_DIM ** -0.5 * math.log2(math.e))).T.astype(BF16)
    ko_ref[...] = kr.astype(BF16)
    vo_ref[...] = v_ref[...].T.astype(BF16)


def _attn_prep(proj, tile_off, n_tok, q_norm_w, k_norm_w, rope_tabs, tiles_per_seq):
    tm = TOKEN_TILE
    rope = rope_tabs is not None
    head_of = np.arange(ATTN_WIDTH) // HEAD_DIM
    bd = jnp.asarray((head_of[:, None] == head_of[None, :]).astype(np.float32) / HEAD_DIM)
    qnw = jnp.tile(q_norm_w, ATTN_HEADS).reshape(1, ATTN_WIDTH)
    knw = jnp.tile(k_norm_w, ATTN_KV_HEADS).reshape(1, KV_WIDTH)
    in_specs = [pl.BlockSpec((tm, ATTN_WIDTH), lambda i: (i + tile_off, COL_Q // ATTN_WIDTH)),
                pl.BlockSpec((tm, KV_WIDTH), lambda i: (i + tile_off, COL_K // KV_WIDTH)),
                pl.BlockSpec((tm, KV_WIDTH), lambda i: (i + tile_off, COL_V // KV_WIDTH)),
                pl.BlockSpec((1, ATTN_WIDTH), lambda i: (0, 0)),
                pl.BlockSpec((1, KV_WIDTH), lambda i: (0, 0)),
                pl.BlockSpec((ATTN_WIDTH, ATTN_WIDTH), lambda i: (0, 0))]
    args = [proj, proj, proj, qnw, knw, bd]
    out_specs = [pl.BlockSpec((ATTN_WIDTH, tm), lambda i: (0, i)),
                 pl.BlockSpec((tm, KV_WIDTH), lambda i: (i, 0)),
                 pl.BlockSpec((KV_WIDTH, tm), lambda i: (0, i))]
    out_shape = [jax.ShapeDtypeStruct((ATTN_WIDTH, n_tok), BF16),
                 jax.ShapeDtypeStruct((n_tok, KV_WIDTH), BF16),
                 jax.ShapeDtypeStruct((KV_WIDTH, n_tok), BF16)]
    if rope:
        cq, sq, ck, sk = rope_tabs
        in_specs += [pl.BlockSpec((tm, ATTN_WIDTH), lambda i: (i % tiles_per_seq, 0)),
                     pl.BlockSpec((tm, ATTN_WIDTH), lambda i: (i % tiles_per_seq, 0)),
                     pl.BlockSpec((tm, KV_WIDTH), lambda i: (i % tiles_per_seq, 0)),
                     pl.BlockSpec((tm, KV_WIDTH), lambda i: (i % tiles_per_seq, 0))]
        args += [cq, sq, ck, sk]
    else:
        out_specs.append(pl.BlockSpec((tm, KV_WIDTH), lambda i: (i, 0)))
        out_shape.append(jax.ShapeDtypeStruct((n_tok, KV_WIDTH), F32))
    return pl.pallas_call(
        functools.partial(_prep_body, rope=rope),
        grid=(n_tok // tm,), in_specs=in_specs, out_specs=out_specs, out_shape=out_shape,
        compiler_params=_cparams("parallel"),
        name="attn_prep_rope" if rope else "attn_prep",
    )(*args)


def _rope_tables(seq_len):
    quarter = HEAD_DIM // 4
    half = HEAD_DIM // 2
    t = np.arange(seq_len)
    pos = np.stack([t // GRID_W, t % GRID_W], axis=1).astype(np.float32)
    inv_freq = jnp.power(ROPE_THETA, -jnp.arange(0, half, 2, dtype=F32) / half)
    d = np.arange(HEAD_DIM)
    which = d // half
    freq = inv_freq[d % quarter]
    ang = jnp.asarray(pos)[:, which] * freq[None, :]
    sign = np.where((d % half) < quarter, -1.0, 1.0).astype(np.float32)
    cos, sin = jnp.cos(ang), jnp.sin(ang) * sign[None, :]
    return (jnp.tile(cos, (1, ATTN_HEADS)), jnp.tile(sin, (1, ATTN_HEADS)),
            jnp.tile(cos, (1, ATTN_KV_HEADS)), jnp.tile(sin, (1, ATTN_KV_HEADS)))


def _attn_body(qt_ref, k_ref, vt_ref, o_ref, m_ref, l_ref, acc_ref, *, nk):
    ki = pl.program_id(2)

    @pl.when(ki == 0)
    def _():
        m_ref[...] = jnp.full(m_ref.shape, -jnp.inf, F32)
        l_ref[...] = jnp.zeros(l_ref.shape, F32)
        acc_ref[...] = jnp.zeros(acc_ref.shape, F32)

    groups = ATTN_HEADS // ATTN_KV_HEADS
    m_all = m_ref[...]
    l_all = l_ref[...]
    k_all = k_ref[0]
    vt_all = vt_ref[0]
    scores, probs, new_m, new_l, alphas, pvs = {}, {}, {}, {}, {}, {}

    def stage_scores(h):
        kv = h // groups
        qt = qt_ref[h * HEAD_DIM:(h + 1) * HEAD_DIM, :]
        scores[h] = _mm(k_all[:, kv * HEAD_DIM:(kv + 1) * HEAD_DIM], qt)

    def stage_softmax(h):
        st = scores.pop(h)
        m_prev = m_all[h]
        m_new = jnp.maximum(m_prev, jnp.max(st, axis=0, keepdims=True))
        alphas[h] = jnp.exp2(m_prev - m_new)
        pt = jnp.exp2(st - m_new)
        new_l[h] = alphas[h] * l_all[h] + jnp.sum(pt, axis=0, keepdims=True)
        new_m[h] = m_new
        probs[h] = pt.astype(BF16)

    def stage_values(h):
        kv = h // groups
        pvs[h] = _mm(vt_all[kv * HEAD_DIM:(kv + 1) * HEAD_DIM, :], probs.pop(h))

    for step in range(ATTN_HEADS + 2):
        if step < ATTN_HEADS:
            stage_scores(step)
        if 0 <= step - 1 < ATTN_HEADS:
            stage_softmax(step - 1)
        if 0 <= step - 2 < ATTN_HEADS:
            stage_values(step - 2)
    for h in range(ATTN_HEADS):
        acc_ref[h] = alphas[h] * acc_ref[h] + pvs[h]
        m_ref[h] = new_m[h]
        l_ref[h] = new_l[h]

    @pl.when(ki == nk - 1)
    def _():
        out = [acc_ref[h] / l_ref[h] for h in range(ATTN_HEADS)]
        o_ref[...] = jnp.concatenate(out, axis=0).astype(o_ref.dtype)


def _key_tile(n_keys):
    tk = min(n_keys, ATTN_K_TILE_MAX)
    while n_keys % tk:
        tk -= LANES
    return tk


def _attention(qt, k, vt, seq_len):
    b, s, _ = k.shape
    tq = ATTN_Q_TILE
    tk = _key_tile(s)
    nq = seq_len // tq
    nk = s // tk
    return pl.pallas_call(
        functools.partial(_attn_body, nk=nk),
        grid=(b, nq, nk),
        in_specs=[pl.BlockSpec((ATTN_WIDTH, tq), lambda bi, qi, ki: (0, bi * nq + qi)),
                  pl.BlockSpec((1, tk, KV_WIDTH), lambda bi, qi, ki: (bi, ki, 0)),
                  pl.BlockSpec((1, KV_WIDTH, tk), lambda bi, qi, ki: (bi, 0, ki))],
        out_specs=pl.BlockSpec((ATTN_WIDTH, tq), lambda bi, qi, ki: (0, bi * nq + qi)),
        out_shape=jax.ShapeDtypeStruct((ATTN_WIDTH, b * seq_len), BF16),
        scratch_shapes=[pltpu.VMEM((ATTN_HEADS, 1, tq), F32),
                        pltpu.VMEM((ATTN_HEADS, 1, tq), F32),
                        pltpu.VMEM((ATTN_HEADS, HEAD_DIM, tq), F32)],
        compiler_params=_cparams("parallel", "parallel", "arbitrary"),
        name="attention",
    )(qt, k, vt)


def _conv_body(first_ref, last_ref, cur_ref, prev_ref, next_ref, w_ref, b_ref, o_ref):
    i = pl.program_id(0)
    x = cur_ref[...]
    tm = x.shape[0]
    prev = jnp.where(first_ref[i] == 1, 0.0, prev_ref[...])
    nxt = jnp.where(last_ref[i] == 1, 0.0, next_ref[...])
    row8 = lax.broadcasted_iota(jnp.int32, prev.shape, 0)
    w = w_ref[...]
    half = SSD_CONV // 2
    acc = b_ref[...] + w[half:half + 1] * x
    for s in range(1, half + 1):
        rolled = pltpu.roll(x, s, 0)
        top = jnp.where(row8 < s, pltpu.roll(prev, s, 0), rolled[:8])
        acc = acc + w[half - s:half - s + 1] * jnp.concatenate([top, rolled[8:]], axis=0)
        rolled = pltpu.roll(x, tm - s, 0)
        bot = jnp.where(row8 >= 8 - s, pltpu.roll(nxt, 8 - s, 0), rolled[tm - 8:])
        acc = acc + w[half + s:half + s + 1] * jnp.concatenate([rolled[:tm - 8], bot], axis=0)
    o_ref[...] = acc * _sigmoid(acc)


def _ssd_conv(first, last, proj, conv_w, conv_b):
    n_tok = proj.shape[0]
    tm = TOKEN_TILE
    sub = tm // 8
    col = COL_XBC // XBC_WIDTH
    w8 = jnp.concatenate([conv_w, jnp.zeros((8 - SSD_CONV, XBC_WIDTH), F32)], axis=0)
    grid_spec = pltpu.PrefetchScalarGridSpec(
        num_scalar_prefetch=2,
        grid=(n_tok // tm,),
        in_specs=[pl.BlockSpec((tm, XBC_WIDTH), lambda i, f, l: (i, col)),
                  pl.BlockSpec((8, XBC_WIDTH), lambda i, f, l: (jnp.maximum(i * sub - 1, 0), col)),
                  pl.BlockSpec((8, XBC_WIDTH),
                               lambda i, f, l: (jnp.minimum((i + 1) * sub, n_tok // 8 - 1), col)),
                  pl.BlockSpec((8, XBC_WIDTH), lambda i, f, l: (0, 0)),
                  pl.BlockSpec((1, XBC_WIDTH), lambda i, f, l: (0, 0))],
        out_specs=pl.BlockSpec((tm, XBC_WIDTH), lambda i, f, l: (i, 0)))
    return pl.pallas_call(
        _conv_body, grid_spec=grid_spec,
        out_shape=jax.ShapeDtypeStruct((n_tok, XBC_WIDTH), F32),
        compiler_params=_cparams("parallel"),
        name="ssd_conv",
    )(first, last, proj, proj, proj, w8, conv_b.reshape(1, XBC_WIDTH))


def _softplus(x):
    return jnp.maximum(x, 0.0) + jnp.log1p(jnp.exp(-jnp.abs(x)))


def _ssd_body(*refs, nc, has_init):
    if has_init:
        (xs_ref, bm_ref, cm_ref, dt_ref, z_ref, dtb_ref, an_ref, drow_ref, nw_ref, h0_ref,
         y_ref, fin_ref, y_scr, h_scr) = refs
    else:
        (xs_ref, bm_ref, cm_ref, dt_ref, z_ref, dtb_ref, an_ref, drow_ref, nw_ref,
         y_ref, fin_ref, y_scr, h_scr) = refs
    sweep_id = pl.program_id(1)
    c = pl.program_id(2)
    chunk_len = SSD_CHUNK
    rep = SSD_HEADS // SSD_GROUPS

    @pl.when(c == 0)
    def _():
        if has_init:
            h_scr[...] = h0_ref[0, 0]
        else:
            h_scr[...] = jnp.zeros(h_scr.shape, F32)

    def sweep(direction):
        chunk = c if direction == 0 else nc - 1 - c
        row0 = pl.multiple_of(chunk * chunk_len, chunk_len)
        xs = xs_ref[...]
        bm = bm_ref[...]
        cm = cm_ref[...]
        dt_all = _softplus(dt_ref[...] + dtb_ref[...])
        la_all = dt_all * an_ref[...]
        ti = lax.broadcasted_iota(jnp.int32, (chunk_len, chunk_len), 0)
        si = lax.broadcasted_iota(jnp.int32, (chunk_len, chunk_len), 1)
        before_eq = (si <= ti) if direction == 0 else (si >= ti)
        cs_all = _mm_exact_lhs(jnp.where(before_eq, 1.0, 0.0).astype(BF16), la_all)
        cs_t = cs_all.T
        last = chunk_len - 1 if direction == 0 else 0
        b_g = [bm[:, g * SSD_STATE:(g + 1) * SSD_STATE].astype(BF16) for g in range(SSD_GROUPS)]
        c_g = [cm[:, g * SSD_STATE:(g + 1) * SSD_STATE].astype(BF16) for g in range(SSD_GROUPS)]
        cb = [_nt(c_g[g], b_g[g]) for g in range(SSD_GROUPS)]
        states = [h_scr[h] for h in range(SSD_HEADS)]
        y_off = [_nt(c_g[h // rep], states[h].astype(BF16)) for h in range(SSD_HEADS)]
        cs_c, xd, scores = [], [], []
        for h in range(SSD_HEADS):
            j = direction * SSD_HEADS + h
            cs_c.append(cs_all[:, j:j + 1])
            decay = jnp.exp(jnp.where(before_eq, cs_c[h] - cs_t[j:j + 1, :], -jnp.inf))
            xd.append(xs[:, h * HEAD_DIM:(h + 1) * HEAD_DIM] * dt_all[:, j:j + 1])
            scores.append((cb[h // rep] * decay).astype(BF16))
        y_diag = [_mm(scores[h], xd[h].astype(BF16)) for h in range(SSD_HEADS)]
        heads, tots, to_state = [], [], []
        for h in range(SSD_HEADS):
            j = direction * SSD_HEADS + h
            tots.append(cs_all[last:last + 1, j:j + 1])
            to_end = jnp.exp(tots[h] - cs_c[h])
            to_state.append(_tn((xd[h] * to_end).astype(BF16), b_g[h // rep]))
            heads.append(y_diag[h] + y_off[h] * jnp.exp(cs_c[h]))
        for h in range(SSD_HEADS):
            h_scr[h] = jnp.exp(tots[h]) * states[h] + to_state[h]
        y_dir = jnp.concatenate(heads, axis=-1)
        if direction == 0:
            y_scr[pl.ds(row0, chunk_len), :] = y_dir + drow_ref[...] * xs
        else:
            y = y_scr[pl.ds(row0, chunk_len), :] + y_dir
            z = z_ref[...]
            y = y * (z * _sigmoid(z))
            y_ref[...] = _rms(y, nw_ref[...]).astype(y_ref.dtype)

    @pl.when(sweep_id == 0)
    def _():
        sweep(0)

    @pl.when(sweep_id == 1)
    def _():
        sweep(1)

    @pl.when(c == nc - 1)
    def _():
        fin_ref[0, 0] = h_scr[...]


def _ssd(proj, xbc, tile_off, batch, seq_len, dt_bias, a_log, d_skip, norm_w, init):
    chunk_len = SSD_CHUNK
    nc = seq_len // chunk_len
    base = tile_off * (TOKEN_TILE // chunk_len)
    has_init = init is not None

    def row(b, s, c):
        return base + b * nc + c + s * (nc - 1 - 2 * c)

    def z_row(b, s, c):
        return base + b * nc + s * (nc - 1 - c) + (1 - s) * (nc - 1)

    def y_row(b, s, c):
        return b * nc + s * (nc - 1 - c) + (1 - s) * (nc - 1)

    dtb = jnp.zeros((1, LANES), F32).at[0, :2 * SSD_HEADS].set(dt_bias.reshape(-1))
    a_neg = jnp.zeros((1, LANES), F32).at[0, :2 * SSD_HEADS].set(-jnp.exp(a_log.reshape(-1)))
    d_row = jnp.repeat(d_skip, HEAD_DIM).reshape(1, SSD_WIDTH)
    in_specs = [
        pl.BlockSpec((chunk_len, SSD_WIDTH), lambda b, s, c: (row(b, s, c), 0)),
        pl.BlockSpec((chunk_len, LANES), lambda b, s, c: (row(b, s, c), SSD_WIDTH // LANES)),
        pl.BlockSpec((chunk_len, LANES), lambda b, s, c: (row(b, s, c), SSD_WIDTH // LANES + 1)),
        pl.BlockSpec((chunk_len, LANES), lambda b, s, c: (row(b, s, c), COL_DT // LANES)),
        pl.BlockSpec((chunk_len, SSD_WIDTH), lambda b, s, c: (z_row(b, s, c), COL_Z // SSD_WIDTH)),
        pl.BlockSpec((1, LANES), lambda b, s, c: (0, 0)),
        pl.BlockSpec((1, LANES), lambda b, s, c: (0, 0)),
        pl.BlockSpec((1, SSD_WIDTH), lambda b, s, c: (0, 0)),
        pl.BlockSpec((1, SSD_WIDTH), lambda b, s, c: (0, 0)),
    ]
    args = [xbc, xbc, xbc, proj, proj, dtb, a_neg, d_row, norm_w.reshape(1, SSD_WIDTH)]
    state_block = (1, 1, SSD_HEADS, HEAD_DIM, SSD_STATE)
    if has_init:
        in_specs.append(pl.BlockSpec(state_block, lambda b, s, c: (b, s, 0, 0, 0)))
        args.append(init)
    return pl.pallas_call(
        functools.partial(_ssd_body, nc=nc, has_init=has_init),
        grid=(batch, 2, nc),
        in_specs=in_specs,
        out_specs=[pl.BlockSpec((chunk_len, SSD_WIDTH), lambda b, s, c: (y_row(b, s, c), 0)),
                   pl.BlockSpec(state_block, lambda b, s, c: (b, s, 0, 0, 0))],
        out_shape=[jax.ShapeDtypeStruct((batch * seq_len, SSD_WIDTH), BF16),
                   jax.ShapeDtypeStruct((batch, 2, SSD_HEADS, HEAD_DIM, SSD_STATE), F32)],
        scratch_shapes=[pltpu.VMEM((seq_len, SSD_WIDTH), F32),
                        pltpu.VMEM((SSD_HEADS, HEAD_DIM, SSD_STATE), F32)],
        compiler_params=_cparams("parallel", "arbitrary", "arbitrary"),
        name="ssd",
    )(*args)


def _rwkv_body(*refs, nc, seq_len, has_init):
    n_io = 10 * RWKV_SEQS
    io_refs = [refs[5 * i:5 * i + 5] for i in range(2 * RWKV_SEQS)]
    (w0_ref, wuph_ref, wupl_ref, a0_ref, auph_ref, aupl_ref, gup_ref, kk_ref, ka_ref, rk_ref,
     lnw_ref, lnb_ref, heads_ref) = refs[n_io:n_io + 13]
    if has_init:
        s0_ref, y_ref, fin_ref, y_scr, s_scr = refs[n_io + 13:]
    else:
        y_ref, fin_ref, y_scr, s_scr = refs[n_io + 13:]

    def head_sum(x):
        return _mm_exact_rhs(x, heads_ref[...])
    c = pl.program_id(1)
    cl = RWKV_CHUNK
    n = HEAD_DIM
    lora = LANES // 2

    @pl.when(c == 0)
    def _():
        if has_init:
            s_scr[...] = s0_ref[...]
        else:
            s_scr[...] = jnp.zeros(s_scr.shape, F32)

    ti = lax.broadcasted_iota(jnp.int32, (cl, cl), 0)
    si = lax.broadcasted_iota(jnp.int32, (cl, cl), 1)
    lane2 = lax.broadcasted_iota(jnp.int32, (cl, 2 * cl), 1)
    left = lane2 < cl
    sign2 = jnp.where(left, 1.0, -1.0)
    t2 = lax.broadcasted_iota(jnp.int32, (cl, 2 * cl), 0)
    s2 = jnp.where(left, lane2, lane2 - cl)

    chains = []
    for idx, (r_ref, k_ref, v_ref, walo_ref, _) in enumerate(io_refs):
        seq, direction = divmod(idx, 2)
        r = r_ref[...]
        k = k_ref[...]
        v = v_ref[...]
        walo = walo_ref[...]
        zw = w0_ref[direction:direction + 1] + _mm_3pass(
            jnp.tanh(walo[:, :lora]), wuph_ref[direction], wupl_ref[direction])
        lw = -math.exp(-0.5) * _sigmoid(zw)
        a = _sigmoid(a0_ref[direction:direction + 1] + _mm_3pass(
            walo[:, lora:], auph_ref[direction], aupl_ref[direction]))
        kd = k * (1.0 + (a - 1.0) * ka_ref[...])
        kk_raw = k * kk_ref[...]
        kk_all = kk_raw * lax.rsqrt(head_sum(kk_raw * kk_raw) + 1e-12)
        akk_all = a * kk_all
        before_eq = (si <= ti) if direction == 0 else (si >= ti)
        before = (si < ti) if direction == 0 else (si > ti)
        before_eq2 = (s2 <= t2) if direction == 0 else (s2 >= t2)
        lp_all = _mm_exact_lhs(jnp.where(before_eq, 1.0, 0.0).astype(BF16), lw)
        last = cl - 1 if direction == 0 else 0
        lp_last = lp_all[last:last + 1]
        dec_in = jnp.exp(lp_all)
        dec_out = jnp.exp(-lp_all)
        dec_before = jnp.exp(lp_all - lw)
        to_end = jnp.exp(lp_last - lp_all)
        end_decay = jnp.exp(lp_last)
        kap_all = (kk_all * dec_before).astype(BF16)
        rt_all = (r * dec_in).astype(BF16)
        kt_all = (kd * dec_out).astype(BF16)
        bt_all = (akk_all * dec_out).astype(BF16)
        kend_all = (kd * to_end).astype(BF16)
        bend_all = (akk_all * to_end).astype(BF16)
        for h in range(RWKV_HEADS):
            sl = slice(h * n, (h + 1) * n)
            chains.append(dict(
                where=(seq, direction, h), v=v[:, sl],
                kap_rt=jnp.concatenate([kap_all[:, sl], rt_all[:, sl]], axis=0),
                kt_bt=jnp.concatenate([kt_all[:, sl], bt_all[:, sl]], axis=0),
                state=s_scr[seq, direction, h], before=before, before_eq2=before_eq2,
                end_decay=end_decay[:, sl],
                kend=jnp.concatenate([kend_all[:, sl], bend_all[:, sl]], axis=0)))

    for ch in chains:
        ch['g'] = _nt(ch['kap_rt'], ch['kt_bt'])
    for ch in chains:
        ch['fs'] = _nt(ch['kap_rt'], ch['state'].astype(BF16))
    for ch in chains:
        g = ch.pop('g')
        ch['n_mat'] = jnp.where(ch['before'], g[:cl, :cl], 0.0).astype(BF16)
        ch['l_mat'] = jnp.where(ch['before'], g[:cl, cl:], 0.0)
        ch['rkb'] = (jnp.where(ch['before_eq2'], g[cl:], 0.0) * sign2).astype(BF16)
    for ch in chains:
        x0 = ch['fs'][:cl] + _mm(ch.pop('n_mat'), ch['v'].astype(BF16))
        ch['z'] = jnp.concatenate([-ch.pop('l_mat'), x0], axis=1)

    for _ in range(int(math.log2(cl))):
        for ch in chains:
            z = ch['z']
            pz = _mm(z[:, :cl].astype(BF16), z.astype(BF16))
            ch['z'] = jnp.where(left, pz, z + pz)

    for ch in chains:
        u = ch['z'][:, cl:]
        ch['y'] = ch['fs'][cl:] + _mm(ch['rkb'],
                                      jnp.concatenate([ch['v'], u], axis=0).astype(BF16))
    for ch in chains:
        seq, direction, h = ch['where']
        v_nu = jnp.concatenate([ch['v'], -ch['z'][:, cl:]], axis=0).astype(BF16)
        s_scr[seq, direction, h] = ch['state'] * ch['end_decay'] + _tn(v_nu, ch['kend'])

    def finish(y, refs_):
        r_ref, k_ref, v_ref, _, glo_ref = refs_
        r = r_ref[...]
        v = v_ref[...]
        gate = _mm(_sigmoid(glo_ref[...]).astype(BF16), gup_ref[...])
        bonus = head_sum(r * k_ref[...] * rk_ref[...]) * v
        centred = y - head_sum(y) * (1.0 / n)
        var = head_sum(centred * centred) * (1.0 / n)
        y_n = centred * lax.rsqrt(var + GN_EPS)
        return ((y_n * lnw_ref[...] + lnb_ref[...] + bonus) * gate).astype(y_ref.dtype)

    ys, rows = [], []
    for idx in range(2 * RWKV_SEQS):
        seq, direction = divmod(idx, 2)
        ys.append(jnp.concatenate(
            [ch['y'] for ch in chains[idx * RWKV_HEADS:(idx + 1) * RWKV_HEADS]], axis=-1))
        chunk = c if direction == 0 else nc - 1 - c
        rows.append(pl.multiple_of(seq * seq_len + chunk * cl, cl))

    @pl.when(c < nc // 2)
    def _():
        for idx in range(2 * RWKV_SEQS):
            y_scr[pl.ds(rows[idx], cl), :] = ys[idx]

    @pl.when(c >= nc // 2)
    def _():
        for idx in range(2 * RWKV_SEQS):
            y_ref[pl.ds(rows[idx], cl), :] = finish(y_scr[pl.ds(rows[idx], cl), :] + ys[idx],
                                                    io_refs[idx])

    @pl.when(c == nc - 1)
    def _():
        fin_ref[...] = s_scr[...]


def _rwkv(proj, tile_off, batch, seq_len, lp, init):
    cl = RWKV_CHUNK
    nc = seq_len // cl
    assert nc % 2 == 0 and batch % RWKV_SEQS == 0
    base = tile_off * (TOKEN_TILE // cl)
    has_init = init is not None
    w = RWKV_WIDTH

    def const(shape):
        return pl.BlockSpec(shape, lambda b, c: (0,) * len(shape))

    def chunk_specs(seq, direction):
        def row(b, c):
            first = base + (b * RWKV_SEQS + seq) * nc
            return first + (c if direction == 0 else nc - 1 - c)
        return [pl.BlockSpec((cl, w), lambda b, c: (row(b, c), COL_R // w)),
                pl.BlockSpec((cl, w), lambda b, c: (row(b, c), COL_KR // w)),
                pl.BlockSpec((cl, w), lambda b, c: (row(b, c), COL_VR // w)),
                pl.BlockSpec((cl, LANES), lambda b, c: (row(b, c), COL_WALO // LANES)),
                pl.BlockSpec((cl, LANES), lambda b, c: (row(b, c), COL_GLO // LANES))]

    in_specs = []
    for seq in range(RWKV_SEQS):
        for direction in range(2):
            in_specs += chunk_specs(seq, direction)
    lora_shape = (2, LANES // 2, w)
    in_specs += [const((2, w)), const(lora_shape), const(lora_shape), const((2, w)),
                 const(lora_shape), const(lora_shape), const((LANES, w)),
                 const((1, w)), const((1, w)), const((1, w)), const((1, w)), const((1, w)),
                 const((w, w))]
    wup_hi, wup_lo = _split2(lp['rwkv_w_up'])
    aup_hi, aup_lo = _split2(lp['rwkv_a_up'])
    head_of = np.arange(w) // HEAD_DIM
    same_head = jnp.asarray((head_of[:, None] == head_of[None, :]).astype(np.float32), BF16)
    args = [proj] * (10 * RWKV_SEQS) + [
        lp['rwkv_w0'], wup_hi, wup_lo, lp['rwkv_a0'], aup_hi, aup_lo,
        lp['rwkv_g_up'].astype(BF16), lp['rwkv_k_k'].reshape(1, w), lp['rwkv_k_a'].reshape(1, w),
        lp['rwkv_r_k'].reshape(1, w), lp['rwkv_ln_w'].reshape(1, w),
        lp['rwkv_ln_b'].reshape(1, w), same_head]
    state_block = (RWKV_SEQS, 2, RWKV_HEADS, HEAD_DIM, HEAD_DIM)
    if has_init:
        in_specs.append(pl.BlockSpec(state_block, lambda b, c: (b, 0, 0, 0, 0)))
        args.append(init)
    return pl.pallas_call(
        functools.partial(_rwkv_body, nc=nc, seq_len=seq_len, has_init=has_init),
        grid=(batch // RWKV_SEQS, nc),
        in_specs=in_specs,
        out_specs=[pl.BlockSpec((RWKV_SEQS * seq_len, w), lambda b, c: (b, 0)),
                   pl.BlockSpec(state_block, lambda b, c: (b, 0, 0, 0, 0))],
        out_shape=[jax.ShapeDtypeStruct((batch * seq_len, w), BF16),
                   jax.ShapeDtypeStruct((batch, 2, RWKV_HEADS, HEAD_DIM, HEAD_DIM), F32)],
        scratch_shapes=[pltpu.VMEM((RWKV_SEQS * seq_len, w), F32),
                        pltpu.VMEM(state_block, F32)],
        compiler_params=_cparams("parallel", "arbitrary"),
        name="rwkv",
    )(*args)


def _out_proj_body(tmod_ref, attn_ref, ssd_ref, rwkv_ref, h_ref, mod_ref, nw_ref, w_ref, rwh_ref,
                   rwl_ref, rb_ref, h_out, u_out, lg_out):
    del tmod_ref
    mixed = (_tn(attn_ref[...], w_ref[:ATTN_WIDTH])
             + _mm(ssd_ref[...], w_ref[ATTN_WIDTH:ATTN_WIDTH + SSD_WIDTH])
             + _mm(rwkv_ref[...], w_ref[ATTN_WIDTH + SSD_WIDTH:]))
    mod = mod_ref[0]
    h = h_ref[...] + mod[2:3] * mixed
    h_out[...] = h
    u = _rms(h, nw_ref[...]) * (1.0 + mod[4:5]) + mod[3:4]
    u_out[...] = u
    lg_out[...] = _mm_3pass(u, rwh_ref[...], rwl_ref[...]) + rb_ref[...]


def _out_proj(tile_mod, attn_t, ssd, rwkv, h, mod, norm_w, w, router_w, router_b):
    n_tok, d = h.shape
    tm = TOKEN_TILE
    rw = jnp.zeros((d, LANES), F32).at[:, :N_EXPERTS].set(router_w)
    rw_hi, rw_lo = _split2(rw)
    rb = jnp.zeros((1, LANES), F32).at[0, :N_EXPERTS].set(router_b)
    grid_spec = pltpu.PrefetchScalarGridSpec(
        num_scalar_prefetch=1,
        grid=(n_tok // tm,),
        in_specs=[pl.BlockSpec((ATTN_WIDTH, tm), lambda i, t: (0, i)),
                  pl.BlockSpec((tm, SSD_WIDTH), lambda i, t: (i, 0)),
                  pl.BlockSpec((tm, RWKV_WIDTH), lambda i, t: (i, 0)),
                  pl.BlockSpec((tm, d), lambda i, t: (i, 0)),
                  pl.BlockSpec((1, 6, d), lambda i, t: (t[i], 0, 0)),
                  pl.BlockSpec((1, d), lambda i, t: (0, 0)),
                  pl.BlockSpec((d, d), lambda i, t: (0, 0)),
                  pl.BlockSpec((d, LANES), lambda i, t: (0, 0)),
                  pl.BlockSpec((d, LANES), lambda i, t: (0, 0)),
                  pl.BlockSpec((1, LANES), lambda i, t: (0, 0))],
        out_specs=[pl.BlockSpec((tm, d), lambda i, t: (i, 0)),
                   pl.BlockSpec((tm, d), lambda i, t: (i, 0)),
                   pl.BlockSpec((tm, LANES), lambda i, t: (i, 0))])
    return pl.pallas_call(
        _out_proj_body, grid_spec=grid_spec,
        out_shape=[jax.ShapeDtypeStruct((n_tok, d), F32),
                   jax.ShapeDtypeStruct((n_tok, d), F32),
                   jax.ShapeDtypeStruct((n_tok, LANES), F32)],
        compiler_params=_cparams("parallel"),
        name="out_proj",
    )(tile_mod, attn_t, ssd, rwkv, h, mod, norm_w.reshape(1, d), w, rw_hi, rw_lo, rb)


def _run_copies(i, start_ref, nch_ref, dst_ref, tot_ref, make_copy):
    def per_expert(e, carry):
        idx = i * N_EXPERTS + e
        start = start_ref[idx]
        dst = dst_ref[idx]

        def per_chunk(j, c2):
            make_copy(pl.multiple_of(start + j * RUN_CHUNK, RUN_CHUNK),
                      pl.multiple_of(dst + j * RUN_CHUNK, RUN_CHUNK)).start()
            return c2

        lax.fori_loop(0, nch_ref[idx], per_chunk, 0)
        return carry

    lax.fori_loop(0, N_EXPERTS, per_expert, 0)

    def wait_one(j, carry):
        make_copy(0, 0).wait()
        return carry

    lax.fori_loop(0, tot_ref[i], wait_one, 0)


def _dispatch_body(start_ref, nch_ref, dst_ref, tot_ref, colt_ref, u_ref, init_hbm, o_hbm,
                   buf, sem):
    del init_hbm
    i = pl.program_id(0)
    colt = colt_ref[0]
    stage_row = lax.broadcasted_iota(jnp.int32, (buf.shape[0], colt.shape[1]), 0)
    sel = jnp.where(stage_row == colt[0:1], 1.0, 0.0)
    for kk in range(1, TOP_K):
        sel = sel + jnp.where(stage_row == colt[kk:kk + 1], 1.0, 0.0)
    buf[...] = _mm(sel.astype(BF16), u_ref[...].astype(BF16))

    def make_copy(hbm_row, stage):
        return pltpu.make_async_copy(buf.at[pl.ds(stage, RUN_CHUNK)],
                                     o_hbm.at[pl.ds(hbm_row, RUN_CHUNK)], sem)

    _run_copies(i, start_ref, nch_ref, dst_ref, tot_ref, make_copy)


def _stage_rows(tm):
    return TOP_K * tm + N_EXPERTS * RUN_CHUNK


def _moe_dispatch(u, tables, n_rows):
    n_tok, d = u.shape
    tm = MOE_TOKEN_TILE
    start, nch, dst, tot, col = tables
    col_t = col.reshape(n_tok // tm, tm, TOP_K).transpose(0, 2, 1)
    grid_spec = pltpu.PrefetchScalarGridSpec(
        num_scalar_prefetch=4,
        grid=(n_tok // tm,),
        in_specs=[pl.BlockSpec((1, TOP_K, tm), lambda i, *_: (i, 0, 0)),
                  pl.BlockSpec((tm, d), lambda i, *_: (i, 0)),
                  pl.BlockSpec(memory_space=pl.ANY)],
        out_specs=pl.BlockSpec(memory_space=pl.ANY),
        scratch_shapes=[pltpu.VMEM((_stage_rows(tm), d), F32), pltpu.SemaphoreType.DMA])
    return pl.pallas_call(
        _dispatch_body, grid_spec=grid_spec,
        out_shape=jax.ShapeDtypeStruct((n_rows, d), u.dtype),
        input_output_aliases={6: 0},
        compiler_params=_cparams("arbitrary"),
        name="moe_dispatch",
    )(start, nch, dst, tot, col_t, u, jnp.zeros((n_rows, d), u.dtype))


def _moe_body(be_ref, nv_ref, new_ref, x_ref, wgu_ref, bgu_ref, wdn_ref, bdn_ref, o_ref,
              wgu_bf, wdn_bf):
    del be_ref
    i = pl.program_id(0)

    @pl.when(new_ref[i] == 1)
    def _():
        wgu_bf[...] = wgu_ref[0].astype(BF16)
        wdn_bf[...] = wdn_ref[0].astype(BF16)

    @pl.when(i < nv_ref[0])
    def _():
        gu = _mm(x_ref[...].astype(BF16), wgu_bf[...]) + bgu_ref[0]
        gate = jnp.minimum(gu[:, :D_EXPERT], SWIGLU_LIMIT)
        up = jnp.clip(gu[:, D_EXPERT:], -SWIGLU_LIMIT, SWIGLU_LIMIT)
        act = (up + 1.0) * gate * _sigmoid(SWIGLU_ALPHA * gate)
        o_ref[...] = _mm(act.astype(BF16), wdn_bf[...]) + bdn_ref[0]

    @pl.when(i >= nv_ref[0])
    def _():
        o_ref[...] = jnp.zeros(o_ref.shape, o_ref.dtype)


def _moe_experts(block_expert, n_valid, x_rows, w_gu, b_gu, w_dn, b_dn):
    n_rows, d = x_rows.shape
    rows = MOE_ROWS
    new_expert = jnp.concatenate([jnp.ones((1,), jnp.int32),
                                  (block_expert[1:] != block_expert[:-1]).astype(jnp.int32)])
    grid_spec = pltpu.PrefetchScalarGridSpec(
        num_scalar_prefetch=3,
        grid=(n_rows // rows,),
        in_specs=[pl.BlockSpec((rows, d), lambda i, be, nv, ne: (i, 0)),
                  pl.BlockSpec((1, d, 2 * D_EXPERT), lambda i, be, nv, ne: (be[i], 0, 0)),
                  pl.BlockSpec((1, 1, 2 * D_EXPERT), lambda i, be, nv, ne: (be[i], 0, 0)),
                  pl.BlockSpec((1, D_EXPERT, d), lambda i, be, nv, ne: (be[i], 0, 0)),
                  pl.BlockSpec((1, 1, d), lambda i, be, nv, ne: (be[i], 0, 0))],
        out_specs=pl.BlockSpec((rows, d), lambda i, be, nv, ne: (i, 0)),
        scratch_shapes=[pltpu.VMEM((d, 2 * D_EXPERT), BF16), pltpu.VMEM((D_EXPERT, d), BF16)])
    return pl.pallas_call(
        _moe_body, grid_spec=grid_spec,
        out_shape=jax.ShapeDtypeStruct((n_rows, d), F32),
        compiler_params=_cparams("arbitrary"),
        name="moe_experts",
    )(block_expert, n_valid, new_expert, x_rows, w_gu, b_gu.reshape(N_EXPERTS, 1, -1), w_dn,
      b_dn.reshape(N_EXPERTS, 1, -1))


def _combine_body(tmod_ref, start_ref, nch_ref, dst_ref, tot_ref, col_ref, gates_ref, h_ref,
                  mod_ref, fw_ref, y_hbm, o_ref, *rest, final_ctx_tiles):
    del tmod_ref
    o2_ref, (buf, sem) = rest[:-2], rest[-2:]
    i = pl.program_id(0)

    @pl.when(i == 0)
    def _():
        buf[...] = jnp.zeros(buf.shape, buf.dtype)

    def make_copy(hbm_row, stage):
        return pltpu.make_async_copy(y_hbm.at[pl.ds(hbm_row, RUN_CHUNK)],
                                     buf.at[pl.ds(stage, RUN_CHUNK)], sem)

    _run_copies(i, start_ref, nch_ref, dst_ref, tot_ref, make_copy)
    col = col_ref[...]
    gates = gates_ref[...]
    stage_row = lax.broadcasted_iota(jnp.int32, (col.shape[0], buf.shape[0]), 1)
    sel = jnp.where(stage_row == col[:, 0:1], gates[:, 0:1], 0.0)
    for kk in range(1, TOP_K):
        sel = sel + jnp.where(stage_row == col[:, kk:kk + 1], gates[:, kk:kk + 1], 0.0)
    sel_hi, sel_lo = _split2(sel)
    rows_bf = buf[...].astype(BF16)
    y = _mm(sel_hi, rows_bf) + _mm(sel_lo, rows_bf)
    h = h_ref[...] + mod_ref[0][5:6] * y
    if final_ctx_tiles is None:
        o_ref[...] = h
    else:
        oc_ref, ol_ref = o_ref, o2_ref[0]
        out = _rms(h, fw_ref[...])

        @pl.when(i < final_ctx_tiles)
        def _():
            oc_ref[...] = out

        @pl.when(i >= final_ctx_tiles)
        def _():
            ol_ref[...] = out


def _moe_combine(tile_mod, tables, gates, h, mod, final_w, y_rows, final_ctx_tiles):
    n_tok, d = h.shape
    tm = MOE_TOKEN_TILE
    ratio = TOKEN_TILE // tm
    start, nch, dst, tot, col = tables
    if final_ctx_tiles is None:
        out_specs = pl.BlockSpec((tm, d), lambda i, *_: (i, 0))
        out_shape = jax.ShapeDtypeStruct((n_tok, d), F32)
    else:
        c = final_ctx_tiles
        out_specs = [pl.BlockSpec((tm, d), lambda i, *_: (jnp.minimum(i, c - 1), 0)),
                     pl.BlockSpec((tm, d), lambda i, *_: (jnp.maximum(i - c, 0), 0))]
        out_shape = [jax.ShapeDtypeStruct((c * tm, d), F32),
                     jax.ShapeDtypeStruct((n_tok - c * tm, d), F32)]
    grid_spec = pltpu.PrefetchScalarGridSpec(
        num_scalar_prefetch=5,
        grid=(n_tok // tm,),
        in_specs=[pl.BlockSpec((tm, TOP_K), lambda i, *_: (i, 0)),
                  pl.BlockSpec((tm, TOP_K), lambda i, *_: (i, 0)),
                  pl.BlockSpec((tm, d), lambda i, *_: (i, 0)),
                  pl.BlockSpec((1, 6, d), lambda i, t, *_: (t[i // ratio], 0, 0)),
                  pl.BlockSpec((1, d), lambda i, *_: (0, 0)),
                  pl.BlockSpec(memory_space=pl.ANY)],
        out_specs=out_specs,
        scratch_shapes=[pltpu.VMEM((_stage_rows(tm), d), F32), pltpu.SemaphoreType.DMA])
    return pl.pallas_call(
        functools.partial(_combine_body, final_ctx_tiles=final_ctx_tiles), grid_spec=grid_spec,
        out_shape=out_shape,
        compiler_params=_cparams("arbitrary"),
        name="moe_combine",
    )(tile_mod, start, nch, dst, tot, col, gates, h, mod, final_w.reshape(1, d), y_rows)


def _route(logits):
    n_tok = logits.shape[0]
    rows = MOE_ROWS
    tm = MOE_TOKEN_TILE
    tiles = n_tok // tm
    slack = tiles * N_EXPERTS * (RUN_CHUNK - 1)
    n_blocks = n_tok * TOP_K // rows + N_EXPERTS + (slack + rows - 1) // rows
    top_logit, top_idx = lax.top_k(logits, TOP_K)
    gates = jax.nn.softmax(top_logit, axis=-1)
    hit = top_idx[:, :, None] == jnp.arange(N_EXPERTS, dtype=top_idx.dtype)[None, None, :]
    per_tok = jnp.sum(hit, axis=1, dtype=jnp.int32).reshape(tiles, tm, N_EXPERTS)
    earlier = jnp.tril(jnp.ones((tm, tm), F32), k=-1)
    in_tile_rank = jnp.einsum('ts,nse->nte', earlier, per_tok.astype(F32),
                              precision=HI).astype(jnp.int32)
    run_len = per_tok.sum(axis=1)
    n_chunks = (run_len + RUN_CHUNK - 1) // RUN_CHUNK
    run_rows = n_chunks * RUN_CHUNK
    rows_before = jnp.cumsum(run_rows, axis=0) - run_rows
    counts = rows_before[-1] + run_rows[-1]
    padded = (counts + rows - 1) // rows * rows
    padded_end = jnp.cumsum(padded)
    padded_start = padded_end - padded
    block_start = jnp.arange(n_blocks, dtype=jnp.int32) * rows
    block_expert = jnp.minimum(
        jnp.sum(block_start[:, None] >= padded_end[None, :], axis=1, dtype=jnp.int32),
        N_EXPERTS - 1)
    n_valid = (padded_end[-1] // rows).astype(jnp.int32).reshape(1)
    run_start = padded_start[None, :] + rows_before
    stage_off = (jnp.cumsum(n_chunks, axis=1) - n_chunks) * RUN_CHUNK
    total_chunks = jnp.sum(n_chunks, axis=1)
    stage_row = (stage_off[:, None, :] + in_tile_rank).reshape(n_tok, 1, N_EXPERTS)
    col = jnp.sum(jnp.where(hit, stage_row, 0), axis=-1)
    tables = (run_start.reshape(-1).astype(jnp.int32), n_chunks.reshape(-1).astype(jnp.int32),
              stage_off.reshape(-1).astype(jnp.int32), total_chunks.astype(jnp.int32),
              col.astype(jnp.int32))
    return tables, gates, block_expert, n_valid, n_blocks * rows


def _permute_w_in(w_in):
    def cols(a, b):
        return w_in[:, a:b]
    q, k, v, z = cols(0, 384), cols(384, 512), cols(512, 640), cols(640, 1024)
    xbc, dt = cols(1024, 1664), cols(1664, 1676)
    r, kr, vr = cols(1676, 1932), cols(1932, 2188), cols(2188, 2444)
    walo, glo = cols(2444, 2572), cols(2572, 2700)
    dt = jnp.pad(dt, ((0, 0), (0, LANES - dt.shape[1])))
    return jnp.concatenate([q, z, r, kr, k, v, dt, walo, glo, xbc, vr], axis=1).astype(BF16)


def kernel(x_prompt, x_sample, cache_attn_k, cache_attn_v, state_ssd, state_rwkv, c, c_ctx,
           norm1_w, norm2_w, ada_w, ada_b, w_in, q_norm_w, k_norm_w,
           ssd_conv_w, ssd_conv_b, ssd_dt_bias, ssd_a_log, ssd_d, ssd_norm_w,
           rwkv_w0, rwkv_w_up, rwkv_a0, rwkv_a_up, rwkv_g_up, rwkv_k_k, rwkv_k_a, rwkv_r_k,
           rwkv_ln_w, rwkv_ln_b, w_out, router_w, router_b,
           moe_w_gate_up, moe_b_gate_up, moe_w_down, moe_b_down, final_norm_w):
    batch, seq, d = x_prompt.shape
    dec_batch, dec_seq, _ = x_sample.shape
    depth = norm1_w.shape[0]
    past = cache_attn_k.shape[2]
    n_ctx, n_lat = batch * seq, dec_batch * dec_seq
    tm = TOKEN_TILE
    ctx_tiles, lat_tiles = n_ctx // tm, n_lat // tm

    tile_mod = np.concatenate([np.zeros(ctx_tiles, np.int32),
                               1 + np.arange(lat_tiles, dtype=np.int32) // (dec_seq // tm)])
    seq_tiles = np.concatenate([np.full(ctx_tiles, seq // tm), np.full(lat_tiles, dec_seq // tm)])
    tile_in_seq = np.concatenate([np.arange(ctx_tiles) % (seq // tm),
                                  np.arange(lat_tiles) % (dec_seq // tm)])
    first = jnp.asarray((tile_in_seq == 0).astype(np.int32))
    last = jnp.asarray((tile_in_seq == seq_tiles - 1).astype(np.int32))
    tile_mod = jnp.asarray(tile_mod)

    cond = jnp.concatenate([c_ctx[None, :], c], axis=0)
    pad_rows = -cond.shape[0] % 8
    cond = jnp.pad(cond, ((0, pad_rows), (0, 0)))
    mod_all = _ada(cond, ada_w, ada_b).reshape(depth, cond.shape[0], 6, d)

    rope_tabs = _rope_tables(dec_seq)
    h = (x_prompt.reshape(n_ctx, d), x_sample.reshape(n_lat, d))
    ks, vs, ssd_states, rwkv_states = [], [], [], []
    for l in range(depth):
        lp = dict(rwkv_w0=rwkv_w0[l], rwkv_w_up=rwkv_w_up[l], rwkv_a0=rwkv_a0[l],
                  rwkv_a_up=rwkv_a_up[l], rwkv_g_up=rwkv_g_up[l], rwkv_k_k=rwkv_k_k[l],
                  rwkv_k_a=rwkv_k_a[l], rwkv_r_k=rwkv_r_k[l], rwkv_ln_w=rwkv_ln_w[l],
                  rwkv_ln_b=rwkv_ln_b[l])
        mod = mod_all[l]
        proj, h = _in_proj(tile_mod, h, mod, norm1_w[l], _permute_w_in(w_in[l]))

        q_c, k_c, v_c, k_norm = _attn_prep(proj, 0, n_ctx, q_norm_w[l], k_norm_w[l], None, 1)
        q_l, k_l, v_l = _attn_prep(proj, ctx_tiles, n_lat, q_norm_w[l], k_norm_w[l], rope_tabs,
                                   dec_seq // tm)
        v_c = v_c.reshape(KV_WIDTH, batch, seq).transpose(1, 0, 2)
        attn_c = _attention(q_c, k_c.reshape(batch, seq, -1), v_c, seq)
        k_full = jnp.concatenate([cache_attn_k[:, l].reshape(dec_batch, past, -1).astype(BF16),
                                  k_l.reshape(dec_batch, dec_seq, -1)], axis=1)
        v_past = cache_attn_v[:, l].reshape(dec_batch, past, -1).astype(BF16).transpose(0, 2, 1)
        v_full = jnp.concatenate(
            [v_past, v_l.reshape(KV_WIDTH, dec_batch, dec_seq).transpose(1, 0, 2)], axis=2)
        attn_l = _attention(q_l, k_full, v_full, dec_seq)
        attn = jnp.concatenate([attn_c, attn_l], axis=1)
        ks.append(k_norm.reshape(batch, seq, ATTN_KV_HEADS, HEAD_DIM))
        vs.append(proj[:n_ctx, COL_V:COL_V + KV_WIDTH].reshape(batch, seq, ATTN_KV_HEADS, HEAD_DIM))

        xbc = _ssd_conv(first, last, proj, ssd_conv_w[l], ssd_conv_b[l])
        ssd_c, ssd_fin = _ssd(proj, xbc, 0, batch, seq, ssd_dt_bias[l], ssd_a_log[l], ssd_d[l],
                              ssd_norm_w[l], None)
        ssd_l, _ = _ssd(proj, xbc, ctx_tiles, dec_batch, dec_seq, ssd_dt_bias[l], ssd_a_log[l],
                        ssd_d[l], ssd_norm_w[l], state_ssd[:, l])
        ssd = jnp.concatenate([ssd_c, ssd_l], axis=0)
        ssd_states.append(ssd_fin)

        rwkv_c, rwkv_fin = _rwkv(proj, 0, batch, seq, lp, None)
        rwkv_l, _ = _rwkv(proj, ctx_tiles, dec_batch, dec_seq, lp, state_rwkv[:, l])
        rwkv = jnp.concatenate([rwkv_c, rwkv_l], axis=0)
        rwkv_states.append(rwkv_fin)

        h, u, logits = _out_proj(tile_mod, attn, ssd, rwkv, h, mod, norm2_w[l],
                                 w_out[l].astype(BF16), router_w[l], router_b[l])

        tables, gates, block_expert, n_valid, n_rows = _route(logits[:, :N_EXPERTS])
        x_rows = _moe_dispatch(u, tables, n_rows)
        y_rows = _moe_experts(block_expert, n_valid, x_rows, moe_w_gate_up[l], moe_b_gate_up[l],
                              moe_w_down[l], moe_b_down[l])
        h = _moe_combine(tile_mod, tables, gates, h, mod, final_norm_w, y_rows,
                         n_ctx // MOE_TOKEN_TILE if l == depth - 1 else None)

    y_prompt = h[0].reshape(batch, seq, d)
    y_sample = h[1].reshape(dec_batch, dec_seq, d)
    return (y_prompt, y_sample, jnp.stack(ks, axis=1), jnp.stack(vs, axis=1),
            jnp.stack(ssd_states, axis=1), jnp.stack(rwkv_states, axis=1))
```

```python
import functools
import math

import numpy as np
import jax
import jax.numpy as jnp
from jax import lax
from jax.experimental import pallas as pl
from jax.experimental.pallas import tpu as pltpu

F32 = jnp.float32
BF16 = jnp.bfloat16
HI = lax.Precision.HIGHEST

D_MODEL = 1024
GRID_W = 64
HEAD_DIM = 64
ATTN_HEADS = 6
ATTN_KV_HEADS = 2
ROPE_THETA = 10000.0
SSD_HEADS = 6
SSD_GROUPS = 2
SSD_STATE = 64
SSD_CONV = 5
SSD_CHUNK = 128
RWKV_HEADS = 4
RWKV_CHUNK = 64
N_EXPERTS = 32
TOP_K = 4
D_EXPERT = 1024
SWIGLU_LIMIT = 7.0
SWIGLU_ALPHA = 1.702
RMS_EPS = 1e-6
GN_EPS = 64e-5

ATTN_WIDTH = ATTN_HEADS * HEAD_DIM
KV_WIDTH = ATTN_KV_HEADS * HEAD_DIM
SSD_WIDTH = SSD_HEADS * HEAD_DIM
XBC_WIDTH = SSD_WIDTH + 2 * SSD_GROUPS * SSD_STATE
RWKV_WIDTH = RWKV_HEADS * HEAD_DIM
LANES = 128

COL_Q, COL_Z, COL_R, COL_KR = 0, 384, 768, 1024
COL_K, COL_V, COL_DT, COL_WALO, COL_GLO = 1280, 1408, 1536, 1664, 1792
COL_XBC, COL_VR = 1920, 2560
PROJ_WIDTH = 2816

TOKEN_TILE = 256
MOE_ROWS = 512
MOE_TOKEN_TILE = 256
RUN_CHUNK = 8
ATTN_Q_TILE = 256
ATTN_K_TILE_MAX = 1536
RWKV_SEQS = 2
VMEM_LIMIT = 56 * 1024 * 1024


def _cparams(*sem):
    return pltpu.CompilerParams(dimension_semantics=sem, vmem_limit_bytes=VMEM_LIMIT)


def _nt(a, b, precision=None):
    return lax.dot_general(a, b, (((1,), (1,)), ((), ())), precision=precision,
                           preferred_element_type=F32)


def _tn(a, b, precision=None):
    return lax.dot_general(a, b, (((0,), (0,)), ((), ())), precision=precision,
                           preferred_element_type=F32)


def _mm(a, b, precision=None):
    return jnp.dot(a, b, precision=precision, preferred_element_type=F32)


def _split2(x):
    hi = x.astype(BF16)
    return hi, (x - hi.astype(F32)).astype(BF16)


def _split3(x):
    hi = x.astype(BF16)
    rest = x - hi.astype(F32)
    mid = rest.astype(BF16)
    return hi, mid, (rest - mid.astype(F32)).astype(BF16)


def _mm_3pass(a, b_hi, b_lo):
    a_hi, a_lo = _split2(a)
    return _mm(a_hi, b_hi) + (_mm(a_lo, b_hi) + _mm(a_hi, b_lo))


def _mm_exact_lhs(m_bf16, x):
    hi, mid, lo = _split3(x)
    return _mm(m_bf16, hi) + (_mm(m_bf16, mid) + _mm(m_bf16, lo))


def _mm_exact_rhs(x, m_bf16):
    hi, mid, lo = _split3(x)
    return _mm(hi, m_bf16) + (_mm(mid, m_bf16) + _mm(lo, m_bf16))


def _sigmoid(x):
    return 1.0 / (1.0 + jnp.exp(-x))


def _rms(x, w):
    ms = jnp.mean(x * x, axis=-1, keepdims=True)
    return x * lax.rsqrt(ms + RMS_EPS) * w


def _ada_body(c_ref, w_ref, b_ref, o_ref):
    x = c_ref[...]
    x = x * _sigmoid(x)
    o_ref[0] = _mm(x, w_ref[0], HI) + b_ref[0]


def _ada(cond, ada_w, ada_b):
    depth, d, n = ada_w.shape
    rows = cond.shape[0]
    tn = 512
    return pl.pallas_call(
        _ada_body,
        grid=(depth, n // tn),
        in_specs=[pl.BlockSpec((rows, d), lambda l, j: (0, 0)),
                  pl.BlockSpec((1, d, tn), lambda l, j: (l, 0, j)),
                  pl.BlockSpec((1, 1, tn), lambda l, j: (l, 0, j))],
        out_specs=pl.BlockSpec((1, rows, tn), lambda l, j: (l, 0, j)),
        out_shape=jax.ShapeDtypeStruct((depth, rows, n), F32),
        compiler_params=_cparams("parallel", "parallel"),
        name="ada_mod",
    )(cond, ada_w, ada_b.reshape(depth, 1, n))


def _in_proj_body(tmod_ref, h_ref, mod_ref, nw_ref, w_ref, o_ref):
    del tmod_ref
    mod = mod_ref[0]
    u = _rms(h_ref[...], nw_ref[...]) * (1.0 + mod[1:2]) + mod[0:1]
    o_ref[...] = _mm(u.astype(BF16), w_ref[...])


def _in_proj_first_body(tmod_ref, hc_ref, hl_ref, mod_ref, nw_ref, w_ref, o_ref, h_ref, *,
                        ctx_tiles):
    del tmod_ref
    x = jnp.where(pl.program_id(0) < ctx_tiles, hc_ref[...], hl_ref[...])
    h_ref[...] = x
    mod = mod_ref[0]
    u = _rms(x, nw_ref[...]) * (1.0 + mod[1:2]) + mod[0:1]
    o_ref[...] = _mm(u.astype(BF16), w_ref[...])


def _in_proj(tile_mod, h, mod, norm_w, w):
    tm = TOKEN_TILE
    first = isinstance(h, tuple)
    if first:
        h_ctx, h_lat = h
        d = h_ctx.shape[1]
        ctx_tiles = h_ctx.shape[0] // tm
        n_tok = h_ctx.shape[0] + h_lat.shape[0]
        h_specs = [pl.BlockSpec((tm, d), lambda i, t: (jnp.minimum(i, ctx_tiles - 1), 0)),
                   pl.BlockSpec((tm, d), lambda i, t: (jnp.maximum(i - ctx_tiles, 0), 0))]
        h_args = [h_ctx, h_lat]
        body = functools.partial(_in_proj_first_body, ctx_tiles=ctx_tiles)
    else:
        n_tok, d = h.shape
        h_specs = [pl.BlockSpec((tm, d), lambda i, t: (i, 0))]
        h_args = [h]
        body = _in_proj_body
    out_specs = [pl.BlockSpec((tm, PROJ_WIDTH), lambda i, t: (i, 0))]
    out_shape = [jax.ShapeDtypeStruct((n_tok, PROJ_WIDTH), F32)]
    if first:
        out_specs.append(pl.BlockSpec((tm, d), lambda i, t: (i, 0)))
        out_shape.append(jax.ShapeDtypeStruct((n_tok, d), F32))
    grid_spec = pltpu.PrefetchScalarGridSpec(
        num_scalar_prefetch=1,
        grid=(n_tok // tm,),
        in_specs=h_specs + [pl.BlockSpec((1, 6, d), lambda i, t: (t[i], 0, 0)),
                            pl.BlockSpec((1, d), lambda i, t: (0, 0)),
                            pl.BlockSpec((d, PROJ_WIDTH), lambda i, t: (0, 0))],
        out_specs=out_specs)
    out = pl.pallas_call(
        body, grid_spec=grid_spec, out_shape=out_shape,
        compiler_params=_cparams("parallel"),
        name="in_proj_first" if first else "in_proj",
    )(tile_mod, *h_args, mod, norm_w.reshape(1, d), w)
    return (out[0], out[1]) if first else (out[0], h)


def _rope_apply(x, cos, sin_signed):
    width = x.shape[-1]
    quarter = HEAD_DIM // 4
    up = pltpu.roll(x, width - quarter, 1)
    down = pltpu.roll(x, quarter, 1)
    lane = lax.broadcasted_iota(jnp.int32, x.shape, 1)
    first = (lane % (2 * quarter)) < quarter
    return x * cos + jnp.where(first, up, down) * sin_signed


def _prep_body(*refs, rope):
    if rope:
        (q_ref, k_ref, v_ref, qnw_ref, knw_ref, bd_ref, cq_ref, sq_ref, ck_ref, sk_ref,
         qo_ref, ko_ref, vo_ref) = refs
    else:
        q_ref, k_ref, v_ref, qnw_ref, knw_ref, bd_ref, qo_ref, ko_ref, vo_ref, kn_ref = refs
    bd = bd_ref[...]
    q = q_ref[...]
    k = k_ref[...]
    qn = q * lax.rsqrt(_mm(q * q, bd, HI) + RMS_EPS) * qnw_ref[...]
    kn = k * lax.rsqrt(_mm(k * k, bd[:KV_WIDTH, :KV_WIDTH], HI) + RMS_EPS) * knw_ref[...]
    if rope:
        qn = _rope_apply(qn, cq_ref[...], sq_ref[...])
        kr = _rope_apply(kn, ck_ref[...], sk_ref[...])
    else:
        kn_ref[...] = kn
        kr = kn
    qo_ref[...] = (qn * (HEAD_DIM ** -0.5 * math.log2(math.e))).T.astype(BF16)
    ko_ref[...] = kr.astype(BF16)
    vo_ref[...] = v_ref[...].T.astype(BF16)


def _attn_prep(proj, tile_off, n_tok, q_norm_w, k_norm_w, rope_tabs, tiles_per_seq):
    tm = TOKEN_TILE
    rope = rope_tabs is not None
    head_of = np.arange(ATTN_WIDTH) // HEAD_DIM
    bd = jnp.asarray((head_of[:, None] == head_of[None, :]).astype(np.float32) / HEAD_DIM)
    qnw = jnp.tile(q_norm_w, ATTN_HEADS).reshape(1, ATTN_WIDTH)
    knw = jnp.tile(k_norm_w, ATTN_KV_HEADS).reshape(1, KV_WIDTH)
    in_specs = [pl.BlockSpec((tm, ATTN_WIDTH), lambda i: (i + tile_off, COL_Q // ATTN_WIDTH)),
                pl.BlockSpec((tm, KV_WIDTH), lambda i: (i + tile_off, COL_K // KV_WIDTH)),
                pl.BlockSpec((tm, KV_WIDTH), lambda i: (i + tile_off, COL_V // KV_WIDTH)),
                pl.BlockSpec((1, ATTN_WIDTH), lambda i: (0, 0)),
                pl.BlockSpec((1, KV_WIDTH), lambda i: (0, 0)),
                pl.BlockSpec((ATTN_WIDTH, ATTN_WIDTH), lambda i: (0, 0))]
    args = [proj, proj, proj, qnw, knw, bd]
    out_specs = [pl.BlockSpec((ATTN_WIDTH, tm), lambda i: (0, i)),
                 pl.BlockSpec((tm, KV_WIDTH), lambda i: (i, 0)),
                 pl.BlockSpec((KV_WIDTH, tm), lambda i: (0, i))]
    out_shape = [jax.ShapeDtypeStruct((ATTN_WIDTH, n_tok), BF16),
                 jax.ShapeDtypeStruct((n_tok, KV_WIDTH), BF16),
                 jax.ShapeDtypeStruct((KV_WIDTH, n_tok), BF16)]
    if rope:
        cq, sq, ck, sk = rope_tabs
        in_specs += [pl.BlockSpec((tm, ATTN_WIDTH), lambda i: (i % tiles_per_seq, 0)),
                     pl.BlockSpec((tm, ATTN_WIDTH), lambda i: (i % tiles_per_seq, 0)),
                     pl.BlockSpec((tm, KV_WIDTH), lambda i: (i % tiles_per_seq, 0)),
                     pl.BlockSpec((tm, KV_WIDTH), lambda i: (i % tiles_per_seq, 0))]
        args += [cq, sq, ck, sk]
    else:
        out_specs.append(pl.BlockSpec((tm, KV_WIDTH), lambda i: (i, 0)))
        out_shape.append(jax.ShapeDtypeStruct((n_tok, KV_WIDTH), F32))
    return pl.pallas_call(
        functools.partial(_prep_body, rope=rope),
        grid=(n_tok // tm,), in_specs=in_specs, out_specs=out_specs, out_shape=out_shape,
        compiler_params=_cparams("parallel"),
        name="attn_prep_rope" if rope else "attn_prep",
    )(*args)


def _rope_tables(seq_len):
    quarter = HEAD_DIM // 4
    half = HEAD_DIM // 2
    t = np.arange(seq_len)
    pos = np.stack([t // GRID_W, t % GRID_W], axis=1).astype(np.float32)
    inv_freq = jnp.power(ROPE_THETA, -jnp.arange(0, half, 2, dtype=F32) / half)
    d = np.arange(HEAD_DIM)
    which = d // half
    freq = inv_freq[d % quarter]
    ang = jnp.asarray(pos)[:, which] * freq[None, :]
    sign = np.where((d % half) < quarter, -1.0, 1.0).astype(np.float32)
    cos, sin = jnp.cos(ang), jnp.sin(ang) * sign[None, :]
    return (jnp.tile(cos, (1, ATTN_HEADS)), jnp.tile(sin, (1, ATTN_HEADS)),
            jnp.tile(cos, (1, ATTN_KV_HEADS)), jnp.tile(sin, (1, ATTN_KV_HEADS)))


def _attn_body(qt_ref, k_ref, vt_ref, o_ref, m_ref, l_ref, acc_ref, *, nk):
    ki = pl.program_id(2)

    @pl.when(ki == 0)
    def _():
        m_ref[...] = jnp.full(m_ref.shape, -jnp.inf, F32)
        l_ref[...] = jnp.zeros(l_ref.shape, F32)
        acc_ref[...] = jnp.zeros(acc_ref.shape, F32)

    groups = ATTN_HEADS // ATTN_KV_HEADS
    m_all = m_ref[...]
    l_all = l_ref[...]
    k_all = k_ref[0]
    vt_all = vt_ref[0]
    scores, probs, new_m, new_l, alphas, pvs = {}, {}, {}, {}, {}, {}

    def stage_scores(h):
        kv = h // groups
        qt = qt_ref[h * HEAD_DIM:(h + 1) * HEAD_DIM, :]
        scores[h] = _mm(k_all[:, kv * HEAD_DIM:(kv + 1) * HEAD_DIM], qt)

    def stage_softmax(h):
        st = scores.pop(h)
        m_prev = m_all[h]
        m_new = jnp.maximum(m_prev, jnp.max(st, axis=0, keepdims=True))
        alphas[h] = jnp.exp2(m_prev - m_new)
        pt = jnp.exp2(st - m_new)
        new_l[h] = alphas[h] * l_all[h] + jnp.sum(pt, axis=0, keepdims=True)
        new_m[h] = m_new
        probs[h] = pt.astype(BF16)

    def stage_values(h):
        kv = h // groups
        pvs[h] = _mm(vt_all[kv * HEAD_DIM:(kv + 1) * HEAD_DIM, :], probs.pop(h))

    for step in range(ATTN_HEADS + 2):
        if step < ATTN_HEADS:
            stage_scores(step)
        if 0 <= step - 1 < ATTN_HEADS:
            stage_softmax(step - 1)
        if 0 <= step - 2 < ATTN_HEADS:
            stage_values(step - 2)
    for h in range(ATTN_HEADS):
        acc_ref[h] = alphas[h] * acc_ref[h] + pvs[h]
        m_ref[h] = new_m[h]
        l_ref[h] = new_l[h]

    @pl.when(ki == nk - 1)
    def _():
        out = [acc_ref[h] / l_ref[h] for h in range(ATTN_HEADS)]
        o_ref[...] = jnp.concatenate(out, axis=0).astype(o_ref.dtype)


def _key_tile(n_keys):
    tk = min(n_keys, ATTN_K_TILE_MAX)
    while n_keys % tk:
        tk -= LANES
    return tk


def _attention(qt, k, vt, seq_len):
    b, s, _ = k.shape
    tq = ATTN_Q_TILE
    tk = _key_tile(s)
    nq = seq_len // tq
    nk = s // tk
    return pl.pallas_call(
        functools.partial(_attn_body, nk=nk),
        grid=(b, nq, nk),
        in_specs=[pl.BlockSpec((ATTN_WIDTH, tq), lambda bi, qi, ki: (0, bi * nq + qi)),
                  pl.BlockSpec((1, tk, KV_WIDTH), lambda bi, qi, ki: (bi, ki, 0)),
                  pl.BlockSpec((1, KV_WIDTH, tk), lambda bi, qi, ki: (bi, 0, ki))],
        out_specs=pl.BlockSpec((ATTN_WIDTH, tq), lambda bi, qi, ki: (0, bi * nq + qi)),
        out_shape=jax.ShapeDtypeStruct((ATTN_WIDTH, b * seq_len), BF16),
        scratch_shapes=[pltpu.VMEM((ATTN_HEADS, 1, tq), F32),
                        pltpu.VMEM((ATTN_HEADS, 1, tq), F32),
                        pltpu.VMEM((ATTN_HEADS, HEAD_DIM, tq), F32)],
        compiler_params=_cparams("parallel", "parallel", "arbitrary"),
        name="attention",
    )(qt, k, vt)


def _conv_body(first_ref, last_ref, cur_ref, prev_ref, next_ref, w_ref, b_ref, o_ref):
    i = pl.program_id(0)
    x = cur_ref[...]
    tm = x.shape[0]
    prev = jnp.where(first_ref[i] == 1, 0.0, prev_ref[...])
    nxt = jnp.where(last_ref[i] == 1, 0.0, next_ref[...])
    row8 = lax.broadcasted_iota(jnp.int32, prev.shape, 0)
    w = w_ref[...]
    half = SSD_CONV // 2
    acc = b_ref[...] + w[half:half + 1] * x
    for s in range(1, half + 1):
        rolled = pltpu.roll(x, s, 0)
        top = jnp.where(row8 < s, pltpu.roll(prev, s, 0), rolled[:8])
        acc = acc + w[half - s:half - s + 1] * jnp.concatenate([top, rolled[8:]], axis=0)
        rolled = pltpu.roll(x, tm - s, 0)
        bot = jnp.where(row8 >= 8 - s, pltpu.roll(nxt, 8 - s, 0), rolled[tm - 8:])
        acc = acc + w[half + s:half + s + 1] * jnp.concatenate([rolled[:tm - 8], bot], axis=0)
    o_ref[...] = acc * _sigmoid(acc)


def _ssd_conv(first, last, proj, conv_w, conv_b):
    n_tok = proj.shape[0]
    tm = TOKEN_TILE
    sub = tm // 8
    col = COL_XBC // XBC_WIDTH
    w8 = jnp.concatenate([conv_w, jnp.zeros((8 - SSD_CONV, XBC_WIDTH), F32)], axis=0)
    grid_spec = pltpu.PrefetchScalarGridSpec(
        num_scalar_prefetch=2,
        grid=(n_tok // tm,),
        in_specs=[pl.BlockSpec((tm, XBC_WIDTH), lambda i, f, l: (i, col)),
                  pl.BlockSpec((8, XBC_WIDTH), lambda i, f, l: (jnp.maximum(i * sub - 1, 0), col)),
                  pl.BlockSpec((8, XBC_WIDTH),
                               lambda i, f, l: (jnp.minimum((i + 1) * sub, n_tok // 8 - 1), col)),
                  pl.BlockSpec((8, XBC_WIDTH), lambda i, f, l: (0, 0)),
                  pl.BlockSpec((1, XBC_WIDTH), lambda i, f, l: (0, 0))],
        out_specs=pl.BlockSpec((tm, XBC_WIDTH), lambda i, f, l: (i, 0)))
    return pl.pallas_call(
        _conv_body, grid_spec=grid_spec,
        out_shape=jax.ShapeDtypeStruct((n_tok, XBC_WIDTH), F32),
        compiler_params=_cparams("parallel"),
        name="ssd_conv",
    )(first, last, proj, proj, proj, w8, conv_b.reshape(1, XBC_WIDTH))


def _softplus(x):
    return jnp.maximum(x, 0.0) + jnp.log1p(jnp.exp(-jnp.abs(x)))


def _ssd_body(*refs, nc, has_init):
    if has_init:
        (xs_ref, bm_ref, cm_ref, dt_ref, z_ref, dtb_ref, an_ref, drow_ref, nw_ref, h0_ref,
         y_ref, fin_ref, y_scr, h_scr) = refs
    else:
        (xs_ref, bm_ref, cm_ref, dt_ref, z_ref, dtb_ref, an_ref, drow_ref, nw_ref,
         y_ref, fin_ref, y_scr, h_scr) = refs
    sweep_id = pl.program_id(1)
    c = pl.program_id(2)
    chunk_len = SSD_CHUNK
    rep = SSD_HEADS // SSD_GROUPS

    @pl.when(c == 0)
    def _():
        if has_init:
            h_scr[...] = h0_ref[0, 0]
        else:
            h_scr[...] = jnp.zeros(h_scr.shape, F32)

    def sweep(direction):
        chunk = c if direction == 0 else nc - 1 - c
        row0 = pl.multiple_of(chunk * chunk_len, chunk_len)
        xs = xs_ref[...]
        bm = bm_ref[...]
        cm = cm_ref[...]
        dt_all = _softplus(dt_ref[...] + dtb_ref[...])
        la_all = dt_all * an_ref[...]
        ti = lax.broadcasted_iota(jnp.int32, (chunk_len, chunk_len), 0)
        si = lax.broadcasted_iota(jnp.int32, (chunk_len, chunk_len), 1)
        before_eq = (si <= ti) if direction == 0 else (si >= ti)
        cs_all = _mm_exact_lhs(jnp.where(before_eq, 1.0, 0.0).astype(BF16), la_all)
        cs_t = cs_all.T
        last = chunk_len - 1 if direction == 0 else 0
        b_g = [bm[:, g * SSD_STATE:(g + 1) * SSD_STATE].astype(BF16) for g in range(SSD_GROUPS)]
        c_g = [cm[:, g * SSD_STATE:(g + 1) * SSD_STATE].astype(BF16) for g in range(SSD_GROUPS)]
        cb = [_nt(c_g[g], b_g[g]) for g in range(SSD_GROUPS)]
        states = [h_scr[h] for h in range(SSD_HEADS)]
        y_off = [_nt(c_g[h // rep], states[h].astype(BF16)) for h in range(SSD_HEADS)]
        cs_c, xd, scores = [], [], []
        for h in range(SSD_HEADS):
            j = direction * SSD_HEADS + h
            cs_c.append(cs_all[:, j:j + 1])
            decay = jnp.exp(jnp.where(before_eq, cs_c[h] - cs_t[j:j + 1, :], -jnp.inf))
            xd.append(xs[:, h * HEAD_DIM:(h + 1) * HEAD_DIM] * dt_all[:, j:j + 1])
            scores.append((cb[h // rep] * decay).astype(BF16))
        y_diag = [_mm(scores[h], xd[h].astype(BF16)) for h in range(SSD_HEADS)]
        heads, tots, to_state = [], [], []
        for h in range(SSD_HEADS):
            j = direction * SSD_HEADS + h
            tots.append(cs_all[last:last + 1, j:j + 1])
            to_end = jnp.exp(tots[h] - cs_c[h])
            to_state.append(_tn((xd[h] * to_end).astype(BF16), b_g[h // rep]))
            heads.append(y_diag[h] + y_off[h] * jnp.exp(cs_c[h]))
        for h in range(SSD_HEADS):
            h_scr[h] = jnp.exp(tots[h]) * states[h] + to_state[h]
        y_dir = jnp.concatenate(heads, axis=-1)
        if direction == 0:
            y_scr[pl.ds(row0, chunk_len), :] = y_dir + drow_ref[...] * xs
        else:
            y = y_scr[pl.ds(row0, chunk_len), :] + y_dir
            z = z_ref[...]
            y = y * (z * _sigmoid(z))
            y_ref[...] = _rms(y, nw_ref[...]).astype(y_ref.dtype)

    @pl.when(sweep_id == 0)
    def _():
        sweep(0)

    @pl.when(sweep_id == 1)
    def _():
        sweep(1)

    @pl.when(c == nc - 1)
    def _():
        fin_ref[0, 0] = h_scr[...]


def _ssd(proj, xbc, tile_off, batch, seq_len, dt_bias, a_log, d_skip, norm_w, init):
    chunk_len = SSD_CHUNK
    nc = seq_len // chunk_len
    base = tile_off * (TOKEN_TILE // chunk_len)
    has_init = init is not None

    def row(b, s, c):
        return base + b * nc + c + s * (nc - 1 - 2 * c)

    def z_row(b, s, c):
        return base + b * nc + s * (nc - 1 - c) + (1 - s) * (nc - 1)

    def y_row(b, s, c):
        return b * nc + s * (nc - 1 - c) + (1 - s) * (nc - 1)

    dtb = jnp.zeros((1, LANES), F32).at[0, :2 * SSD_HEADS].set(dt_bias.reshape(-1))
    a_neg = jnp.zeros((1, LANES), F32).at[0, :2 * SSD_HEADS].set(-jnp.exp(a_log.reshape(-1)))
    d_row = jnp.repeat(d_skip, HEAD_DIM).reshape(1, SSD_WIDTH)
    in_specs = [
        pl.BlockSpec((chunk_len, SSD_WIDTH), lambda b, s, c: (row(b, s, c), 0)),
        pl.BlockSpec((chunk_len, LANES), lambda b, s, c: (row(b, s, c), SSD_WIDTH // LANES)),
        pl.BlockSpec((chunk_len, LANES), lambda b, s, c: (row(b, s, c), SSD_WIDTH // LANES + 1)),
        pl.BlockSpec((chunk_len, LANES), lambda b, s, c: (row(b, s, c), COL_DT // LANES)),
        pl.BlockSpec((chunk_len, SSD_WIDTH), lambda b, s, c: (z_row(b, s, c), COL_Z // SSD_WIDTH)),
        pl.BlockSpec((1, LANES), lambda b, s, c: (0, 0)),
        pl.BlockSpec((1, LANES), lambda b, s, c: (0, 0)),
        pl.BlockSpec((1, SSD_WIDTH), lambda b, s, c: (0, 0)),
        pl.BlockSpec((1, SSD_WIDTH), lambda b, s, c: (0, 0)),
    ]
    args = [xbc, xbc, xbc, proj, proj, dtb, a_neg, d_row, norm_w.reshape(1, SSD_WIDTH)]
    state_block = (1, 1, SSD_HEADS, HEAD_DIM, SSD_STATE)
    if has_init:
        in_specs.append(pl.BlockSpec(state_block, lambda b, s, c: (b, s, 0, 0, 0)))
        args.append(init)
    return pl.pallas_call(
        functools.partial(_ssd_body, nc=nc, has_init=has_init),
        grid=(batch, 2, nc),
        in_specs=in_specs,
        out_specs=[pl.BlockSpec((chunk_len, SSD_WIDTH), lambda b, s, c: (y_row(b, s, c), 0)),
                   pl.BlockSpec(state_block, lambda b, s, c: (b, s, 0, 0, 0))],
        out_shape=[jax.ShapeDtypeStruct((batch * seq_len, SSD_WIDTH), BF16),
                   jax.ShapeDtypeStruct((batch, 2, SSD_HEADS, HEAD_DIM, SSD_STATE), F32)],
        scratch_shapes=[pltpu.VMEM((seq_len, SSD_WIDTH), F32),
                        pltpu.VMEM((SSD_HEADS, HEAD_DIM, SSD_STATE), F32)],
        compiler_params=_cparams("parallel", "arbitrary", "arbitrary"),
        name="ssd",
    )(*args)


def _rwkv_body(*refs, nc, seq_len, has_init):
    n_io = 10 * RWKV_SEQS
    io_refs = [refs[5 * i:5 * i + 5] for i in range(2 * RWKV_SEQS)]
    (w0_ref, wuph_ref, wupl_ref, a0_ref, auph_ref, aupl_ref, gup_ref, kk_ref, ka_ref, rk_ref,
     lnw_ref, lnb_ref, heads_ref) = refs[n_io:n_io + 13]
    if has_init:
        s0_ref, y_ref, fin_ref, y_scr, s_scr = refs[n_io + 13:]
    else:
        y_ref, fin_ref, y_scr, s_scr = refs[n_io + 13:]

    def head_sum(x):
        return _mm_exact_rhs(x, heads_ref[...])
    c = pl.program_id(1)
    cl = RWKV_CHUNK
    n = HEAD_DIM
    lora = LANES // 2

    @pl.when(c == 0)
    def _():
        if has_init:
            s_scr[...] = s0_ref[...]
        else:
            s_scr[...] = jnp.zeros(s_scr.shape, F32)

    ti = lax.broadcasted_iota(jnp.int32, (cl, cl), 0)
    si = lax.broadcasted_iota(jnp.int32, (cl, cl), 1)
    lane2 = lax.broadcasted_iota(jnp.int32, (cl, 2 * cl), 1)
    left = lane2 < cl
    sign2 = jnp.where(left, 1.0, -1.0)
    t2 = lax.broadcasted_iota(jnp.int32, (cl, 2 * cl), 0)
    s2 = jnp.where(left, lane2, lane2 - cl)

    chains = []
    for idx, (r_ref, k_ref, v_ref, walo_ref, _) in enumerate(io_refs):
        seq, direction = divmod(idx, 2)
        r = r_ref[...]
        k = k_ref[...]
        v = v_ref[...]
        walo = walo_ref[...]
        zw = w0_ref[direction:direction + 1] + _mm_3pass(
            jnp.tanh(walo[:, :lora]), wuph_ref[direction], wupl_ref[direction])
        lw = -math.exp(-0.5) * _sigmoid(zw)
        a = _sigmoid(a0_ref[direction:direction + 1] + _mm_3pass(
            walo[:, lora:], auph_ref[direction], aupl_ref[direction]))
        kd = k * (1.0 + (a - 1.0) * ka_ref[...])
        kk_raw = k * kk_ref[...]
        kk_all = kk_raw * lax.rsqrt(head_sum(kk_raw * kk_raw) + 1e-12)
        akk_all = a * kk_all
        before_eq = (si <= ti) if direction == 0 else (si >= ti)
        before = (si < ti) if direction == 0 else (si > ti)
        before_eq2 = (s2 <= t2) if direction == 0 else (s2 >= t2)
        lp_all = _mm_exact_lhs(jnp.where(before_eq, 1.0, 0.0).astype(BF16), lw)
        last = cl - 1 if direction == 0 else 0
        lp_last = lp_all[last:last + 1]
        dec_in = jnp.exp(lp_all)
        dec_out = jnp.exp(-lp_all)
        dec_before = jnp.exp(lp_all - lw)
        to_end = jnp.exp(lp_last - lp_all)
        end_decay = jnp.exp(lp_last)
        kap_all = (kk_all * dec_before).astype(BF16)
        rt_all = (r * dec_in).astype(BF16)
        kt_all = (kd * dec_out).astype(BF16)
        bt_all = (akk_all * dec_out).astype(BF16)
        kend_all = (kd * to_end).astype(BF16)
        bend_all = (akk_all * to_end).astype(BF16)
        for h in range(RWKV_HEADS):
            sl = slice(h * n, (h + 1) * n)
            chains.append(dict(
                where=(seq, direction, h), v=v[:, sl],
                kap_rt=jnp.concatenate([kap_all[:, sl], rt_all[:, sl]], axis=0),
                kt_bt=jnp.concatenate([kt_all[:, sl], bt_all[:, sl]], axis=0),
                state=s_scr[seq, direction, h], before=before, before_eq2=before_eq2,
                end_decay=end_decay[:, sl],
                kend=jnp.concatenate([kend_all[:, sl], bend_all[:, sl]], axis=0)))

    for ch in chains:
        ch['g'] = _nt(ch['kap_rt'], ch['kt_bt'])
    for ch in chains:
        ch['fs'] = _nt(ch['kap_rt'], ch['state'].astype(BF16))
    for ch in chains:
        g = ch.pop('g')
        ch['n_mat'] = jnp.where(ch['before'], g[:cl, :cl], 0.0).astype(BF16)
        ch['l_mat'] = jnp.where(ch['before'], g[:cl, cl:], 0.0)
        ch['rkb'] = (jnp.where(ch['before_eq2'], g[cl:], 0.0) * sign2).astype(BF16)
    for ch in chains:
        x0 = ch['fs'][:cl] + _mm(ch.pop('n_mat'), ch['v'].astype(BF16))
        ch['z'] = jnp.concatenate([-ch.pop('l_mat'), x0], axis=1)

    for _ in range(int(math.log2(cl))):
        for ch in chains:
            z = ch['z']
            pz = _mm(z[:, :cl].astype(BF16), z.astype(BF16))
            ch['z'] = jnp.where(left, pz, z + pz)

    for ch in chains:
        u = ch['z'][:, cl:]
        ch['y'] = ch['fs'][cl:] + _mm(ch['rkb'],
                                      jnp.concatenate([ch['v'], u], axis=0).astype(BF16))
    for ch in chains:
        seq, direction, h = ch['where']
        v_nu = jnp.concatenate([ch['v'], -ch['z'][:, cl:]], axis=0).astype(BF16)
        s_scr[seq, direction, h] = ch['state'] * ch['end_decay'] + _tn(v_nu, ch['kend'])

    def finish(y, refs_):
        r_ref, k_ref, v_ref, _, glo_ref = refs_
        r = r_ref[...]
        v = v_ref[...]
        gate = _mm(_sigmoid(glo_ref[...]).astype(BF16), gup_ref[...])
        bonus = head_sum(r * k_ref[...] * rk_ref[...]) * v
        centred = y - head_sum(y) * (1.0 / n)
        var = head_sum(centred * centred) * (1.0 / n)
        y_n = centred * lax.rsqrt(var + GN_EPS)
        return ((y_n * lnw_ref[...] + lnb_ref[...] + bonus) * gate).astype(y_ref.dtype)

    ys, rows = [], []
    for idx in range(2 * RWKV_SEQS):
        seq, direction = divmod(idx, 2)
        ys.append(jnp.concatenate(
            [ch['y'] for ch in chains[idx * RWKV_HEADS:(idx + 1) * RWKV_HEADS]], axis=-1))
        chunk = c if direction == 0 else nc - 1 - c
        rows.append(pl.multiple_of(seq * seq_len + chunk * cl, cl))

    @pl.when(c < nc // 2)
    def _():
        for idx in range(2 * RWKV_SEQS):
            y_scr[pl.ds(rows[idx], cl), :] = ys[idx]

    @pl.when(c >= nc // 2)
    def _():
        for idx in range(2 * RWKV_SEQS):
            y_ref[pl.ds(rows[idx], cl), :] = finish(y_scr[pl.ds(rows[idx], cl), :] + ys[idx],
                                                    io_refs[idx])

    @pl.when(c == nc - 1)
    def _():
        fin_ref[...] = s_scr[...]


def _rwkv(proj, tile_off, batch, seq_len, lp, init):
    cl = RWKV_CHUNK
    nc = seq_len // cl
    assert nc % 2 == 0 and batch % RWKV_SEQS == 0
    base = tile_off * (TOKEN_TILE // cl)
    has_init = init is not None
    w = RWKV_WIDTH

    def const(shape):
        return pl.BlockSpec(shape, lambda b, c: (0,) * len(shape))

    def chunk_specs(seq, direction):
        def row(b, c):
            first = base + (b * RWKV_SEQS + seq) * nc
            return first + (c if direction == 0 else nc - 1 - c)
        return [pl.BlockSpec((cl, w), lambda b, c: (row(b, c), COL_R // w)),
                pl.BlockSpec((cl, w), lambda b, c: (row(b, c), COL_KR // w)),
                pl.BlockSpec((cl, w), lambda b, c: (row(b, c), COL_VR // w)),
                pl.BlockSpec((cl, LANES), lambda b, c: (row(b, c), COL_WALO // LANES)),
                pl.BlockSpec((cl, LANES), lambda b, c: (row(b, c), COL_GLO // LANES))]

    in_specs = []
    for seq in range(RWKV_SEQS):
        for direction in range(2):
            in_specs += chunk_specs(seq, direction)
    lora_shape = (2, LANES // 2, w)
    in_specs += [const((2, w)), const(lora_shape), const(lora_shape), const((2, w)),
                 const(lora_shape), const(lora_shape), const((LANES, w)),
                 const((1, w)), const((1, w)), const((1, w)), const((1, w)), const((1, w)),
                 const((w, w))]
    wup_hi, wup_lo = _split2(lp['rwkv_w_up'])
    aup_hi, aup_lo = _split2(lp['rwkv_a_up'])
    head_of = np.arange(w) // HEAD_DIM
    same_head = jnp.asarray((head_of[:, None] == head_of[None, :]).astype(np.float32), BF16)
    args = [proj] * (10 * RWKV_SEQS) + [
        lp['rwkv_w0'], wup_hi, wup_lo, lp['rwkv_a0'], aup_hi, aup_lo,
        lp['rwkv_g_up'].astype(BF16), lp['rwkv_k_k'].reshape(1, w), lp['rwkv_k_a'].reshape(1, w),
        lp['rwkv_r_k'].reshape(1, w), lp['rwkv_ln_w'].reshape(1, w),
        lp['rwkv_ln_b'].reshape(1, w), same_head]
    state_block = (RWKV_SEQS, 2, RWKV_HEADS, HEAD_DIM, HEAD_DIM)
    if has_init:
        in_specs.append(pl.BlockSpec(state_block, lambda b, c: (b, 0, 0, 0, 0)))
        args.append(init)
    return pl.pallas_call(
        functools.partial(_rwkv_body, nc=nc, seq_len=seq_len, has_init=has_init),
        grid=(batch // RWKV_SEQS, nc),
        in_specs=in_specs,
        out_specs=[pl.BlockSpec((RWKV_SEQS * seq_len, w), lambda b, c: (b, 0)),
                   pl.BlockSpec(state_block, lambda b, c: (b, 0, 0, 0, 0))],
        out_shape=[jax.ShapeDtypeStruct((batch * seq_len, w), BF16),
                   jax.ShapeDtypeStruct((batch, 2, RWKV_HEADS, HEAD_DIM, HEAD_DIM), F32)],
        scratch_shapes=[pltpu.VMEM((RWKV_SEQS * seq_len, w), F32),
                        pltpu.VMEM(state_block, F32)],
        compiler_params=_cparams("parallel", "arbitrary"),
        name="rwkv",
    )(*args)


def _out_proj_body(tmod_ref, attn_ref, ssd_ref, rwkv_ref, h_ref, mod_ref, nw_ref, w_ref, rwh_ref,
                   rwl_ref, rb_ref, h_out, u_out, lg_out):
    del tmod_ref
    mixed = (_tn(attn_ref[...], w_ref[:ATTN_WIDTH])
             + _mm(ssd_ref[...], w_ref[ATTN_WIDTH:ATTN_WIDTH + SSD_WIDTH])
             + _mm(rwkv_ref[...], w_ref[ATTN_WIDTH + SSD_WIDTH:]))
    mod = mod_ref[0]
    h = h_ref[...] + mod[2:3] * mixed
    h_out[...] = h
    u = _rms(h, nw_ref[...]) * (1.0 + mod[4:5]) + mod[3:4]
    u_out[...] = u
    lg_out[...] = _mm_3pass(u, rwh_ref[...], rwl_ref[...]) + rb_ref[...]


def _out_proj(tile_mod, attn_t, ssd, rwkv, h, mod, norm_w, w, router_w, router_b):
    n_tok, d = h.shape
    tm = TOKEN_TILE
    rw = jnp.zeros((d, LANES), F32).at[:, :N_EXPERTS].set(router_w)
    rw_hi, rw_lo = _split2(rw)
    rb = jnp.zeros((1, LANES), F32).at[0, :N_EXPERTS].set(router_b)
    grid_spec = pltpu.PrefetchScalarGridSpec(
        num_scalar_prefetch=1,
        grid=(n_tok // tm,),
        in_specs=[pl.BlockSpec((ATTN_WIDTH, tm), lambda i, t: (0, i)),
                  pl.BlockSpec((tm, SSD_WIDTH), lambda i, t: (i, 0)),
                  pl.BlockSpec((tm, RWKV_WIDTH), lambda i, t: (i, 0)),
                  pl.BlockSpec((tm, d), lambda i, t: (i, 0)),
                  pl.BlockSpec((1, 6, d), lambda i, t: (t[i], 0, 0)),
                  pl.BlockSpec((1, d), lambda i, t: (0, 0)),
                  pl.BlockSpec((d, d), lambda i, t: (0, 0)),
                  pl.BlockSpec((d, LANES), lambda i, t: (0, 0)),
                  pl.BlockSpec((d, LANES), lambda i, t: (0, 0)),
                  pl.BlockSpec((1, LANES), lambda i, t: (0, 0))],
        out_specs=[pl.BlockSpec((tm, d), lambda i, t: (i, 0)),
                   pl.BlockSpec((tm, d), lambda i, t: (i, 0)),
                   pl.BlockSpec((tm, LANES), lambda i, t: (i, 0))])
    return pl.pallas_call(
        _out_proj_body, grid_spec=grid_spec,
        out_shape=[jax.ShapeDtypeStruct((n_tok, d), F32),
                   jax.ShapeDtypeStruct((n_tok, d), F32),
                   jax.ShapeDtypeStruct((n_tok, LANES), F32)],
        compiler_params=_cparams("parallel"),
        name="out_proj",
    )(tile_mod, attn_t, ssd, rwkv, h, mod, norm_w.reshape(1, d), w, rw_hi, rw_lo, rb)


def _run_copies(i, start_ref, nch_ref, dst_ref, tot_ref, make_copy):
    def per_expert(e, carry):
        idx = i * N_EXPERTS + e
        start = start_ref[idx]
        dst = dst_ref[idx]

        def per_chunk(j, c2):
            make_copy(pl.multiple_of(start + j * RUN_CHUNK, RUN_CHUNK),
                      pl.multiple_of(dst + j * RUN_CHUNK, RUN_CHUNK)).start()
            return c2

        lax.fori_loop(0, nch_ref[idx], per_chunk, 0)
        return carry

    lax.fori_loop(0, N_EXPERTS, per_expert, 0)

    def wait_one(j, carry):
        make_copy(0, 0).wait()
        return carry

    lax.fori_loop(0, tot_ref[i], wait_one, 0)


def _zero_unused_rows(tail_start_ref, tail_nch_ref, nv_ref, o_hbm, zeros, sem):
    zeros[...] = jnp.zeros(zeros.shape, zeros.dtype)
    n_blocks = o_hbm.shape[0] // MOE_ROWS

    def tail_copy(row):
        return pltpu.make_async_copy(zeros.at[pl.ds(0, RUN_CHUNK)],
                                     o_hbm.at[pl.ds(row, RUN_CHUNK)], sem)

    def block_copy(row):
        return pltpu.make_async_copy(zeros, o_hbm.at[pl.ds(row, MOE_ROWS)], sem)

    def per_expert(e, carry):
        def per_chunk(j, c2):
            tail_copy(pl.multiple_of(tail_start_ref[e] + j * RUN_CHUNK, RUN_CHUNK)).start()
            return c2
        lax.fori_loop(0, tail_nch_ref[e], per_chunk, 0)
        return carry

    lax.fori_loop(0, N_EXPERTS, per_expert, 0)

    def per_block(b, carry):
        block_copy(pl.multiple_of(b * MOE_ROWS, MOE_ROWS)).start()
        return carry

    lax.fori_loop(nv_ref[0], n_blocks, per_block, 0)

    def wait_tails(e, carry):
        def wait_chunk(j, c2):
            tail_copy(0).wait()
            return c2
        lax.fori_loop(0, tail_nch_ref[e], wait_chunk, 0)
        return carry

    lax.fori_loop(0, N_EXPERTS, wait_tails, 0)

    def wait_block(b, carry):
        block_copy(0).wait()
        return carry

    lax.fori_loop(nv_ref[0], n_blocks, wait_block, 0)


def _dispatch_body(start_ref, nch_ref, dst_ref, tot_ref, tail_start_ref, tail_nch_ref, nv_ref,
                   colt_ref, u_ref, o_hbm, buf, zeros, sem, zsem):
    i = pl.program_id(0)

    @pl.when(i == 0)
    def _():
        _zero_unused_rows(tail_start_ref, tail_nch_ref, nv_ref, o_hbm, zeros, zsem)

    colt = colt_ref[0]
    stage_row = lax.broadcasted_iota(jnp.int32, (buf.shape[0], colt.shape[1]), 0)
    sel = jnp.where(stage_row == colt[0:1], 1.0, 0.0)
    for kk in range(1, TOP_K):
        sel = sel + jnp.where(stage_row == colt[kk:kk + 1], 1.0, 0.0)
    buf[...] = _mm(sel.astype(BF16), u_ref[...].astype(BF16))

    def make_copy(hbm_row, stage):
        return pltpu.make_async_copy(buf.at[pl.ds(stage, RUN_CHUNK)],
                                     o_hbm.at[pl.ds(hbm_row, RUN_CHUNK)], sem)

    _run_copies(i, start_ref, nch_ref, dst_ref, tot_ref, make_copy)


def _stage_rows(tm):
    return TOP_K * tm + N_EXPERTS * RUN_CHUNK


def _moe_dispatch(u, tables, tails, n_valid, n_rows):
    n_tok, d = u.shape
    tm = MOE_TOKEN_TILE
    start, nch, dst, tot, col = tables
    tail_start, tail_nch = tails
    col_t = col.reshape(n_tok // tm, tm, TOP_K).transpose(0, 2, 1)
    grid_spec = pltpu.PrefetchScalarGridSpec(
        num_scalar_prefetch=7,
        grid=(n_tok // tm,),
        in_specs=[pl.BlockSpec((1, TOP_K, tm), lambda i, *_: (i, 0, 0)),
                  pl.BlockSpec((tm, d), lambda i, *_: (i, 0))],
        out_specs=pl.BlockSpec(memory_space=pl.ANY),
        scratch_shapes=[pltpu.VMEM((_stage_rows(tm), d), F32), pltpu.VMEM((MOE_ROWS, d), F32),
                        pltpu.SemaphoreType.DMA, pltpu.SemaphoreType.DMA])
    return pl.pallas_call(
        _dispatch_body, grid_spec=grid_spec,
        out_shape=jax.ShapeDtypeStruct((n_rows, d), u.dtype),
        compiler_params=_cparams("arbitrary"),
        name="moe_dispatch",
    )(start, nch, dst, tot, tail_start, tail_nch, n_valid, col_t, u)


def _moe_body(be_ref, nv_ref, new_ref, x_ref, wgu_ref, bgu_ref, wdn_ref, bdn_ref, o_ref,
              wgu_bf, wdn_bf):
    del be_ref
    i = pl.program_id(0)

    @pl.when(new_ref[i] == 1)
    def _():
        wgu_bf[...] = wgu_ref[0, 0].astype(BF16)
        wdn_bf[...] = wdn_ref[0, 0].astype(BF16)

    @pl.when(i < nv_ref[0])
    def _():
        gu = _mm(x_ref[...].astype(BF16), wgu_bf[...]) + bgu_ref[0, 0]
        gate = jnp.minimum(gu[:, :D_EXPERT], SWIGLU_LIMIT)
        up = jnp.clip(gu[:, D_EXPERT:], -SWIGLU_LIMIT, SWIGLU_LIMIT)
        act = (up + 1.0) * gate * _sigmoid(SWIGLU_ALPHA * gate)
        o_ref[...] = _mm(act.astype(BF16), wdn_bf[...]) + bdn_ref[0, 0]

    @pl.when(i >= nv_ref[0])
    def _():
        o_ref[...] = jnp.zeros(o_ref.shape, o_ref.dtype)


def _moe_experts(block_expert, n_valid, x_rows, layer, w_gu, b_gu, w_dn, b_dn):
    n_rows, d = x_rows.shape
    rows = MOE_ROWS
    depth = w_gu.shape[0]
    new_expert = jnp.concatenate([jnp.ones((1,), jnp.int32),
                                  (block_expert[1:] != block_expert[:-1]).astype(jnp.int32)])
    grid_spec = pltpu.PrefetchScalarGridSpec(
        num_scalar_prefetch=3,
        grid=(n_rows // rows,),
        in_specs=[pl.BlockSpec((rows, d), lambda i, be, nv, ne: (i, 0)),
                  pl.BlockSpec((1, 1, d, 2 * D_EXPERT), lambda i, be, nv, ne: (layer, be[i], 0, 0)),
                  pl.BlockSpec((1, 1, 1, 2 * D_EXPERT), lambda i, be, nv, ne: (layer, be[i], 0, 0)),
                  pl.BlockSpec((1, 1, D_EXPERT, d), lambda i, be, nv, ne: (layer, be[i], 0, 0)),
                  pl.BlockSpec((1, 1, 1, d), lambda i, be, nv, ne: (layer, be[i], 0, 0))],
        out_specs=pl.BlockSpec((rows, d), lambda i, be, nv, ne: (i, 0)),
        scratch_shapes=[pltpu.VMEM((d, 2 * D_EXPERT), BF16), pltpu.VMEM((D_EXPERT, d), BF16)])
    return pl.pallas_call(
        _moe_body, grid_spec=grid_spec,
        out_shape=jax.ShapeDtypeStruct((n_rows, d), F32),
        compiler_params=_cparams("arbitrary"),
        name="moe_experts",
    )(block_expert, n_valid, new_expert, x_rows, w_gu, b_gu.reshape(depth, N_EXPERTS, 1, -1),
      w_dn, b_dn.reshape(depth, N_EXPERTS, 1, -1))


def _combine_body(tmod_ref, start_ref, nch_ref, dst_ref, tot_ref, col_ref, gates_ref, h_ref,
                  mod_ref, fw_ref, y_hbm, o_ref, *rest, final_ctx_tiles):
    del tmod_ref
    o2_ref, (buf, sem) = rest[:-2], rest[-2:]
    i = pl.program_id(0)

    @pl.when(i == 0)
    def _():
        buf[...] = jnp.zeros(buf.shape, buf.dtype)

    def make_copy(hbm_row, stage):
        return pltpu.make_async_copy(y_hbm.at[pl.ds(hbm_row, RUN_CHUNK)],
                                     buf.at[pl.ds(stage, RUN_CHUNK)], sem)

    _run_copies(i, start_ref, nch_ref, dst_ref, tot_ref, make_copy)
    col = col_ref[...]
    gates = gates_ref[...]
    stage_row = lax.broadcasted_iota(jnp.int32, (col.shape[0], buf.shape[0]), 1)
    sel = jnp.where(stage_row == col[:, 0:1], gates[:, 0:1], 0.0)
    for kk in range(1, TOP_K):
        sel = sel + jnp.where(stage_row == col[:, kk:kk + 1], gates[:, kk:kk + 1], 0.0)
    sel_hi, sel_lo = _split2(sel)
    rows_bf = buf[...].astype(BF16)
    y = _mm(sel_hi, rows_bf) + _mm(sel_lo, rows_bf)
    h = h_ref[...] + mod_ref[0][5:6] * y
    if final_ctx_tiles is None:
        o_ref[...] = h
    else:
        oc_ref, ol_ref = o_ref, o2_ref[0]
        out = _rms(h, fw_ref[...])

        @pl.when(i < final_ctx_tiles)
        def _():
            oc_ref[...] = out

        @pl.when(i >= final_ctx_tiles)
        def _():
            ol_ref[...] = out


def _moe_combine(tile_mod, tables, gates, h, mod, final_w, y_rows, final_ctx_tiles):
    n_tok, d = h.shape
    tm = MOE_TOKEN_TILE
    ratio = TOKEN_TILE // tm
    start, nch, dst, tot, col = tables
    if final_ctx_tiles is None:
        out_specs = pl.BlockSpec((tm, d), lambda i, *_: (i, 0))
        out_shape = jax.ShapeDtypeStruct((n_tok, d), F32)
    else:
        c = final_ctx_tiles
        out_specs = [pl.BlockSpec((tm, d), lambda i, *_: (jnp.minimum(i, c - 1), 0)),
                     pl.BlockSpec((tm, d), lambda i, *_: (jnp.maximum(i - c, 0), 0))]
        out_shape = [jax.ShapeDtypeStruct((c * tm, d), F32),
                     jax.ShapeDtypeStruct((n_tok - c * tm, d), F32)]
    grid_spec = pltpu.PrefetchScalarGridSpec(
        num_scalar_prefetch=5,
        grid=(n_tok // tm,),
        in_specs=[pl.BlockSpec((tm, TOP_K), lambda i, *_: (i, 0)),
                  pl.BlockSpec((tm, TOP_K), lambda i, *_: (i, 0)),
                  pl.BlockSpec((tm, d), lambda i, *_: (i, 0)),
                  pl.BlockSpec((1, 6, d), lambda i, t, *_: (t[i // ratio], 0, 0)),
                  pl.BlockSpec((1, d), lambda i, *_: (0, 0)),
                  pl.BlockSpec(memory_space=pl.ANY)],
        out_specs=out_specs,
        scratch_shapes=[pltpu.VMEM((_stage_rows(tm), d), F32), pltpu.SemaphoreType.DMA])
    return pl.pallas_call(
        functools.partial(_combine_body, final_ctx_tiles=final_ctx_tiles), grid_spec=grid_spec,
        out_shape=out_shape,
        compiler_params=_cparams("arbitrary"),
        name="moe_combine",
    )(tile_mod, start, nch, dst, tot, col, gates, h, mod, final_w.reshape(1, d), y_rows)


def _route(logits):
    n_tok = logits.shape[0]
    rows = MOE_ROWS
    tm = MOE_TOKEN_TILE
    tiles = n_tok // tm
    slack = tiles * N_EXPERTS * (RUN_CHUNK - 1)
    n_blocks = n_tok * TOP_K // rows + N_EXPERTS + (slack + rows - 1) // rows
    top_logit, top_idx = lax.top_k(logits, TOP_K)
    gates = jax.nn.softmax(top_logit, axis=-1)
    hit = top_idx[:, :, None] == jnp.arange(N_EXPERTS, dtype=top_idx.dtype)[None, None, :]
    per_tok = jnp.sum(hit, axis=1, dtype=jnp.int32).reshape(tiles, tm, N_EXPERTS)
    earlier = jnp.tril(jnp.ones((tm, tm), F32), k=-1)
    in_tile_rank = jnp.einsum('ts,nse->nte', earlier, per_tok.astype(F32),
                              precision=HI).astype(jnp.int32)
    run_len = per_tok.sum(axis=1)
    n_chunks = (run_len + RUN_CHUNK - 1) // RUN_CHUNK
    run_rows = n_chunks * RUN_CHUNK
    rows_before = jnp.cumsum(run_rows, axis=0) - run_rows
    counts = rows_before[-1] + run_rows[-1]
    padded = (counts + rows - 1) // rows * rows
    padded_end = jnp.cumsum(padded)
    padded_start = padded_end - padded
    block_start = jnp.arange(n_blocks, dtype=jnp.int32) * rows
    block_expert = jnp.minimum(
        jnp.sum(block_start[:, None] >= padded_end[None, :], axis=1, dtype=jnp.int32),
        N_EXPERTS - 1)
    n_valid = (padded_end[-1] // rows).astype(jnp.int32).reshape(1)
    run_start = padded_start[None, :] + rows_before
    stage_off = (jnp.cumsum(n_chunks, axis=1) - n_chunks) * RUN_CHUNK
    total_chunks = jnp.sum(n_chunks, axis=1)
    stage_row = (stage_off[:, None, :] + in_tile_rank).reshape(n_tok, 1, N_EXPERTS)
    col = jnp.sum(jnp.where(hit, stage_row, 0), axis=-1)
    tables = (run_start.reshape(-1).astype(jnp.int32), n_chunks.reshape(-1).astype(jnp.int32),
              stage_off.reshape(-1).astype(jnp.int32), total_chunks.astype(jnp.int32),
              col.astype(jnp.int32))
    tails = ((padded_start + counts).astype(jnp.int32),
             ((padded - counts) // RUN_CHUNK).astype(jnp.int32))
    return tables, tails, gates, block_expert, n_valid, n_blocks * rows


def _permute_w_in(w_in):
    def cols(a, b):
        return w_in[:, a:b]
    q, k, v, z = cols(0, 384), cols(384, 512), cols(512, 640), cols(640, 1024)
    xbc, dt = cols(1024, 1664), cols(1664, 1676)
    r, kr, vr = cols(1676, 1932), cols(1932, 2188), cols(2188, 2444)
    walo, glo = cols(2444, 2572), cols(2572, 2700)
    dt = jnp.pad(dt, ((0, 0), (0, LANES - dt.shape[1])))
    return jnp.concatenate([q, z, r, kr, k, v, dt, walo, glo, xbc, vr], axis=1).astype(BF16)


def kernel(x_prompt, x_sample, cache_attn_k, cache_attn_v, state_ssd, state_rwkv, c, c_ctx,
           norm1_w, norm2_w, ada_w, ada_b, w_in, q_norm_w, k_norm_w,
           ssd_conv_w, ssd_conv_b, ssd_dt_bias, ssd_a_log, ssd_d, ssd_norm_w,
           rwkv_w0, rwkv_w_up, rwkv_a0, rwkv_a_up, rwkv_g_up, rwkv_k_k, rwkv_k_a, rwkv_r_k,
           rwkv_ln_w, rwkv_ln_b, w_out, router_w, router_b,
           moe_w_gate_up, moe_b_gate_up, moe_w_down, moe_b_down, final_norm_w):
    batch, seq, d = x_prompt.shape
    dec_batch, dec_seq, _ = x_sample.shape
    depth = norm1_w.shape[0]
    past = cache_attn_k.shape[2]
    n_ctx, n_lat = batch * seq, dec_batch * dec_seq
    tm = TOKEN_TILE
    ctx_tiles, lat_tiles = n_ctx // tm, n_lat // tm

    tile_mod = np.concatenate([np.zeros(ctx_tiles, np.int32),
                               1 + np.arange(lat_tiles, dtype=np.int32) // (dec_seq // tm)])
    seq_tiles = np.concatenate([np.full(ctx_tiles, seq // tm), np.full(lat_tiles, dec_seq // tm)])
    tile_in_seq = np.concatenate([np.arange(ctx_tiles) % (seq // tm),
                                  np.arange(lat_tiles) % (dec_seq // tm)])
    first = jnp.asarray((tile_in_seq == 0).astype(np.int32))
    last = jnp.asarray((tile_in_seq == seq_tiles - 1).astype(np.int32))
    tile_mod = jnp.asarray(tile_mod)

    cond = jnp.concatenate([c_ctx[None, :], c], axis=0)
    pad_rows = -cond.shape[0] % 8
    cond = jnp.pad(cond, ((0, pad_rows), (0, 0)))
    mod_all = _ada(cond, ada_w, ada_b).reshape(depth, cond.shape[0], 6, d)

    rope_tabs = _rope_tables(dec_seq)
    h = (x_prompt.reshape(n_ctx, d), x_sample.reshape(n_lat, d))
    ks, vs, ssd_states, rwkv_states = [], [], [], []
    for l in range(depth):
        lp = dict(rwkv_w0=rwkv_w0[l], rwkv_w_up=rwkv_w_up[l], rwkv_a0=rwkv_a0[l],
                  rwkv_a_up=rwkv_a_up[l], rwkv_g_up=rwkv_g_up[l], rwkv_k_k=rwkv_k_k[l],
                  rwkv_k_a=rwkv_k_a[l], rwkv_r_k=rwkv_r_k[l], rwkv_ln_w=rwkv_ln_w[l],
                  rwkv_ln_b=rwkv_ln_b[l])
        mod = mod_all[l]
        proj, h = _in_proj(tile_mod, h, mod, norm1_w[l], _permute_w_in(w_in[l]))

        q_c, k_c, v_c, k_norm = _attn_prep(proj, 0, n_ctx, q_norm_w[l], k_norm_w[l], None, 1)
        q_l, k_l, v_l = _attn_prep(proj, ctx_tiles, n_lat, q_norm_w[l], k_norm_w[l], rope_tabs,
                                   dec_seq // tm)
        v_c = v_c.reshape(KV_WIDTH, batch, seq).transpose(1, 0, 2)
        attn_c = _attention(q_c, k_c.reshape(batch, seq, -1), v_c, seq)
        k_full = jnp.concatenate([cache_attn_k[:, l].reshape(dec_batch, past, -1).astype(BF16),
                                  k_l.reshape(dec_batch, dec_seq, -1)], axis=1)
        v_past = cache_attn_v[:, l].reshape(dec_batch, past, -1).astype(BF16).transpose(0, 2, 1)
        v_full = jnp.concatenate(
            [v_past, v_l.reshape(KV_WIDTH, dec_batch, dec_seq).transpose(1, 0, 2)], axis=2)
        attn_l = _attention(q_l, k_full, v_full, dec_seq)
        attn = jnp.concatenate([attn_c, attn_l], axis=1)
        ks.append(k_norm.reshape(batch, seq, ATTN_KV_HEADS, HEAD_DIM))
        vs.append(proj[:n_ctx, COL_V:COL_V + KV_WIDTH].reshape(batch, seq, ATTN_KV_HEADS, HEAD_DIM))

        xbc = _ssd_conv(first, last, proj, ssd_conv_w[l], ssd_conv_b[l])
        ssd_c, ssd_fin = _ssd(proj, xbc, 0, batch, seq, ssd_dt_bias[l], ssd_a_log[l], ssd_d[l],
                              ssd_norm_w[l], None)
        ssd_l, _ = _ssd(proj, xbc, ctx_tiles, dec_batch, dec_seq, ssd_dt_bias[l], ssd_a_log[l],
                        ssd_d[l], ssd_norm_w[l], state_ssd[:, l])
        ssd = jnp.concatenate([ssd_c, ssd_l], axis=0)
        ssd_states.append(ssd_fin)

        rwkv_c, rwkv_fin = _rwkv(proj, 0, batch, seq, lp, None)
        rwkv_l, _ = _rwkv(proj, ctx_tiles, dec_batch, dec_seq, lp, state_rwkv[:, l])
        rwkv = jnp.concatenate([rwkv_c, rwkv_l], axis=0)
        rwkv_states.append(rwkv_fin)

        h, u, logits = _out_proj(tile_mod, attn, ssd, rwkv, h, mod, norm2_w[l],
                                 w_out[l].astype(BF16), router_w[l], router_b[l])

        tables, tails, gates, block_expert, n_valid, n_rows = _route(logits[:, :N_EXPERTS])
        x_rows = _moe_dispatch(u, tables, tails, n_valid, n_rows)
        y_rows = _moe_experts(block_expert, n_valid, x_rows, l, moe_w_gate_up, moe_b_gate_up,
                              moe_w_down, moe_b_down)
        h = _moe_combine(tile_mod, tables, gates, h, mod, final_norm_w, y_rows,
                         n_ctx // MOE_TOKEN_TILE if l == depth - 1 else None)

    y_prompt = h[0].reshape(batch, seq, d)
    y_sample = h[1].reshape(dec_batch, dec_seq, d)
    return (y_prompt, y_sample, jnp.stack(ks, axis=1), jnp.stack(vs, axis=1),
            jnp.stack(ssd_states, axis=1), jnp.stack(rwkv_states, axis=1))
```

```python
import functools
import math

import numpy as np
import jax
import jax.numpy as jnp
from jax import lax
from jax.experimental import pallas as pl
from jax.experimental.pallas import tpu as pltpu

F32 = jnp.float32
BF16 = jnp.bfloat16
HI = lax.Precision.HIGHEST

D_MODEL = 1024
GRID_W = 64
HEAD_DIM = 64
ATTN_HEADS = 6
ATTN_KV_HEADS = 2
ROPE_THETA = 10000.0
SSD_HEADS = 6
SSD_GROUPS = 2
SSD_STATE = 64
SSD_CONV = 5
SSD_CHUNK = 128
RWKV_HEADS = 4
RWKV_CHUNK = 64
N_EXPERTS = 32
TOP_K = 4
D_EXPERT = 1024
SWIGLU_LIMIT = 7.0
SWIGLU_ALPHA = 1.702
RMS_EPS = 1e-6
GN_EPS = 64e-5

ATTN_WIDTH = ATTN_HEADS * HEAD_DIM
KV_WIDTH = ATTN_KV_HEADS * HEAD_DIM
SSD_WIDTH = SSD_HEADS * HEAD_DIM
XBC_WIDTH = SSD_WIDTH + 2 * SSD_GROUPS * SSD_STATE
RWKV_WIDTH = RWKV_HEADS * HEAD_DIM
LANES = 128

COL_Q, COL_Z, COL_R, COL_KR = 0, 384, 768, 1024
COL_K, COL_V, COL_DT, COL_WALO, COL_GLO = 1280, 1408, 1536, 1664, 1792
COL_XBC, COL_VR = 1920, 2560
PROJ_WIDTH = 2816

TOKEN_TILE = 256
MOE_ROWS = 512
MOE_TOKEN_TILE = 256
RUN_CHUNK = 8
ATTN_Q_TILE = 512
ATTN_K_TILE_MAX = 1536
RWKV_SEQS = 2
VMEM_LIMIT = 56 * 1024 * 1024


def _cparams(*sem):
    return pltpu.CompilerParams(dimension_semantics=sem, vmem_limit_bytes=VMEM_LIMIT)


def _nt(a, b, precision=None):
    return lax.dot_general(a, b, (((1,), (1,)), ((), ())), precision=precision,
                           preferred_element_type=F32)


def _tn(a, b, precision=None):
    return lax.dot_general(a, b, (((0,), (0,)), ((), ())), precision=precision,
                           preferred_element_type=F32)


def _mm(a, b, precision=None):
    return jnp.dot(a, b, precision=precision, preferred_element_type=F32)


def _split2(x):
    hi = x.astype(BF16)
    return hi, (x - hi.astype(F32)).astype(BF16)


def _split3(x):
    hi = x.astype(BF16)
    rest = x - hi.astype(F32)
    mid = rest.astype(BF16)
    return hi, mid, (rest - mid.astype(F32)).astype(BF16)


def _mm_3pass(a, b_hi, b_lo):
    a_hi, a_lo = _split2(a)
    return _mm(a_hi, b_hi) + (_mm(a_lo, b_hi) + _mm(a_hi, b_lo))


def _mm_exact_lhs(m_bf16, x):
    hi, mid, lo = _split3(x)
    return _mm(m_bf16, hi) + (_mm(m_bf16, mid) + _mm(m_bf16, lo))


def _mm_exact_rhs(x, m_bf16):
    hi, mid, lo = _split3(x)
    return _mm(hi, m_bf16) + (_mm(mid, m_bf16) + _mm(lo, m_bf16))


def _sigmoid(x):
    return 1.0 / (1.0 + jnp.exp(-x))


def _rms(x, w):
    ms = jnp.mean(x * x, axis=-1, keepdims=True)
    return x * lax.rsqrt(ms + RMS_EPS) * w


def _ada_body(c_ref, w_ref, b_ref, o_ref):
    x = c_ref[...]
    x = x * _sigmoid(x)
    o_ref[0] = _mm(x, w_ref[0], HI) + b_ref[0]


def _ada(cond, ada_w, ada_b):
    depth, d, n = ada_w.shape
    rows = cond.shape[0]
    tn = 512
    return pl.pallas_call(
        _ada_body,
        grid=(depth, n // tn),
        in_specs=[pl.BlockSpec((rows, d), lambda l, j: (0, 0)),
                  pl.BlockSpec((1, d, tn), lambda l, j: (l, 0, j)),
                  pl.BlockSpec((1, 1, tn), lambda l, j: (l, 0, j))],
        out_specs=pl.BlockSpec((1, rows, tn), lambda l, j: (l, 0, j)),
        out_shape=jax.ShapeDtypeStruct((depth, rows, n), F32),
        compiler_params=_cparams("parallel", "parallel"),
        name="ada_mod",
    )(cond, ada_w, ada_b.reshape(depth, 1, n))


def _in_proj_body(tmod_ref, h_ref, mod_ref, nw_ref, w_ref, o_ref):
    del tmod_ref
    mod = mod_ref[0]
    u = _rms(h_ref[...], nw_ref[...]) * (1.0 + mod[1:2]) + mod[0:1]
    o_ref[...] = _mm(u.astype(BF16), w_ref[...])


def _in_proj_first_body(tmod_ref, hc_ref, hl_ref, mod_ref, nw_ref, w_ref, o_ref, h_ref, *,
                        ctx_tiles):
    del tmod_ref
    x = jnp.where(pl.program_id(0) < ctx_tiles, hc_ref[...], hl_ref[...])
    h_ref[...] = x
    mod = mod_ref[0]
    u = _rms(x, nw_ref[...]) * (1.0 + mod[1:2]) + mod[0:1]
    o_ref[...] = _mm(u.astype(BF16), w_ref[...])


def _in_proj(tile_mod, h, mod, norm_w, w):
    tm = TOKEN_TILE
    first = isinstance(h, tuple)
    if first:
        h_ctx, h_lat = h
        d = h_ctx.shape[1]
        ctx_tiles = h_ctx.shape[0] // tm
        n_tok = h_ctx.shape[0] + h_lat.shape[0]
        h_specs = [pl.BlockSpec((tm, d), lambda i, t: (jnp.minimum(i, ctx_tiles - 1), 0)),
                   pl.BlockSpec((tm, d), lambda i, t: (jnp.maximum(i - ctx_tiles, 0), 0))]
        h_args = [h_ctx, h_lat]
        body = functools.partial(_in_proj_first_body, ctx_tiles=ctx_tiles)
    else:
        n_tok, d = h.shape
        h_specs = [pl.BlockSpec((tm, d), lambda i, t: (i, 0))]
        h_args = [h]
        body = _in_proj_body
    out_specs = [pl.BlockSpec((tm, PROJ_WIDTH), lambda i, t: (i, 0))]
    out_shape = [jax.ShapeDtypeStruct((n_tok, PROJ_WIDTH), F32)]
    if first:
        out_specs.append(pl.BlockSpec((tm, d), lambda i, t: (i, 0)))
        out_shape.append(jax.ShapeDtypeStruct((n_tok, d), F32))
    grid_spec = pltpu.PrefetchScalarGridSpec(
        num_scalar_prefetch=1,
        grid=(n_tok // tm,),
        in_specs=h_specs + [pl.BlockSpec((1, 6, d), lambda i, t: (t[i], 0, 0)),
                            pl.BlockSpec((1, d), lambda i, t: (0, 0)),
                            pl.BlockSpec((d, PROJ_WIDTH), lambda i, t: (0, 0))],
        out_specs=out_specs)
    out = pl.pallas_call(
        body, grid_spec=grid_spec, out_shape=out_shape,
        compiler_params=_cparams("parallel"),
        name="in_proj_first" if first else "in_proj",
    )(tile_mod, *h_args, mod, norm_w.reshape(1, d), w)
    return (out[0], out[1]) if first else (out[0], h)


def _rope_apply(x, cos, sin_signed):
    width = x.shape[-1]
    quarter = HEAD_DIM // 4
    up = pltpu.roll(x, width - quarter, 1)
    down = pltpu.roll(x, quarter, 1)
    lane = lax.broadcasted_iota(jnp.int32, x.shape, 1)
    first = (lane % (2 * quarter)) < quarter
    return x * cos + jnp.where(first, up, down) * sin_signed


def _prep_body(*refs, rope):
    if rope:
        (q_ref, k_ref, v_ref, qnw_ref, knw_ref, bd_ref, cq_ref, sq_ref, ck_ref, sk_ref,
         qo_ref, ko_ref, vo_ref) = refs
    else:
        q_ref, k_ref, v_ref, qnw_ref, knw_ref, bd_ref, qo_ref, ko_ref, vo_ref, kn_ref = refs
    bd = bd_ref[...]
    q = q_ref[...]
    k = k_ref[...]
    qn = q * lax.rsqrt(_mm(q * q, bd, HI) + RMS_EPS) * qnw_ref[...]
    kn = k * lax.rsqrt(_mm(k * k, bd[:KV_WIDTH, :KV_WIDTH], HI) + RMS_EPS) * knw_ref[...]
    if rope:
        qn = _rope_apply(qn, cq_ref[...], sq_ref[...])
        kr = _rope_apply(kn, ck_ref[...], sk_ref[...])
    else:
        kn_ref[...] = kn
        kr = kn
    qo_ref[...] = (qn * (HEAD_DIM ** -0.5 * math.log2(math.e))).T.astype(BF16)
    ko_ref[...] = kr.astype(BF16)
    vo_ref[...] = v_ref[...].T.astype(BF16)


def _attn_prep(proj, tile_off, n_tok, q_norm_w, k_norm_w, rope_tabs, tiles_per_seq):
    tm = TOKEN_TILE
    rope = rope_tabs is not None
    head_of = np.arange(ATTN_WIDTH) // HEAD_DIM
    bd = jnp.asarray((head_of[:, None] == head_of[None, :]).astype(np.float32) / HEAD_DIM)
    qnw = jnp.tile(q_norm_w, ATTN_HEADS).reshape(1, ATTN_WIDTH)
    knw = jnp.tile(k_norm_w, ATTN_KV_HEADS).reshape(1, KV_WIDTH)
    in_specs = [pl.BlockSpec((tm, ATTN_WIDTH), lambda i: (i + tile_off, COL_Q // ATTN_WIDTH)),
                pl.BlockSpec((tm, KV_WIDTH), lambda i: (i + tile_off, COL_K // KV_WIDTH)),
                pl.BlockSpec((tm, KV_WIDTH), lambda i: (i + tile_off, COL_V // KV_WIDTH)),
                pl.BlockSpec((1, ATTN_WIDTH), lambda i: (0, 0)),
                pl.BlockSpec((1, KV_WIDTH), lambda i: (0, 0)),
                pl.BlockSpec((ATTN_WIDTH, ATTN_WIDTH), lambda i: (0, 0))]
    args = [proj, proj, proj, qnw, knw, bd]
    out_specs = [pl.BlockSpec((ATTN_WIDTH, tm), lambda i: (0, i)),
                 pl.BlockSpec((tm, KV_WIDTH), lambda i: (i, 0)),
                 pl.BlockSpec((KV_WIDTH, tm), lambda i: (0, i))]
    out_shape = [jax.ShapeDtypeStruct((ATTN_WIDTH, n_tok), BF16),
                 jax.ShapeDtypeStruct((n_tok, KV_WIDTH), BF16),
                 jax.ShapeDtypeStruct((KV_WIDTH, n_tok), BF16)]
    if rope:
        cq, sq, ck, sk = rope_tabs
        in_specs += [pl.BlockSpec((tm, ATTN_WIDTH), lambda i: (i % tiles_per_seq, 0)),
                     pl.BlockSpec((tm, ATTN_WIDTH), lambda i: (i % tiles_per_seq, 0)),
                     pl.BlockSpec((tm, KV_WIDTH), lambda i: (i % tiles_per_seq, 0)),
                     pl.BlockSpec((tm, KV_WIDTH), lambda i: (i % tiles_per_seq, 0))]
        args += [cq, sq, ck, sk]
    else:
        out_specs.append(pl.BlockSpec((tm, KV_WIDTH), lambda i: (i, 0)))
        out_shape.append(jax.ShapeDtypeStruct((n_tok, KV_WIDTH), F32))
    return pl.pallas_call(
        functools.partial(_prep_body, rope=rope),
        grid=(n_tok // tm,), in_specs=in_specs, out_specs=out_specs, out_shape=out_shape,
        compiler_params=_cparams("parallel"),
        name="attn_prep_rope" if rope else "attn_prep",
    )(*args)


def _rope_tables(seq_len):
    quarter = HEAD_DIM // 4
    half = HEAD_DIM // 2
    t = np.arange(seq_len)
    pos = np.stack([t // GRID_W, t % GRID_W], axis=1).astype(np.float32)
    inv_freq = jnp.power(ROPE_THETA, -jnp.arange(0, half, 2, dtype=F32) / half)
    d = np.arange(HEAD_DIM)
    which = d // half
    freq = inv_freq[d % quarter]
    ang = jnp.asarray(pos)[:, which] * freq[None, :]
    sign = np.where((d % half) < quarter, -1.0, 1.0).astype(np.float32)
    cos, sin = jnp.cos(ang), jnp.sin(ang) * sign[None, :]
    return (jnp.tile(cos, (1, ATTN_HEADS)), jnp.tile(sin, (1, ATTN_HEADS)),
            jnp.tile(cos, (1, ATTN_KV_HEADS)), jnp.tile(sin, (1, ATTN_KV_HEADS)))


def _attn_body(qt_ref, k_ref, vt_ref, o_ref, m_ref, l_ref, acc_ref, *, nk):
    ki = pl.program_id(2)

    @pl.when(ki == 0)
    def _():
        m_ref[...] = jnp.full(m_ref.shape, -jnp.inf, F32)
        l_ref[...] = jnp.zeros(l_ref.shape, F32)
        acc_ref[...] = jnp.zeros(acc_ref.shape, F32)

    groups = ATTN_HEADS // ATTN_KV_HEADS
    m_all = m_ref[...]
    l_all = l_ref[...]
    k_all = k_ref[0]
    vt_all = vt_ref[0]
    scores, probs, new_m, new_l, alphas, pvs = {}, {}, {}, {}, {}, {}

    def stage_scores(h):
        kv = h // groups
        qt = qt_ref[h * HEAD_DIM:(h + 1) * HEAD_DIM, :]
        scores[h] = _mm(k_all[:, kv * HEAD_DIM:(kv + 1) * HEAD_DIM], qt)

    def stage_softmax(h):
        st = scores.pop(h)
        m_prev = m_all[h]
        m_new = jnp.maximum(m_prev, jnp.max(st, axis=0, keepdims=True))
        alphas[h] = jnp.exp2(m_prev - m_new)
        pt = jnp.exp2(st - m_new)
        new_l[h] = alphas[h] * l_all[h] + jnp.sum(pt, axis=0, keepdims=True)
        new_m[h] = m_new
        probs[h] = pt.astype(BF16)

    def stage_values(h):
        kv = h // groups
        pvs[h] = _mm(vt_all[kv * HEAD_DIM:(kv + 1) * HEAD_DIM, :], probs.pop(h))

    for step in range(ATTN_HEADS + 2):
        if step < ATTN_HEADS:
            stage_scores(step)
        if 0 <= step - 1 < ATTN_HEADS:
            stage_softmax(step - 1)
        if 0 <= step - 2 < ATTN_HEADS:
            stage_values(step - 2)
    for h in range(ATTN_HEADS):
        acc_ref[h] = alphas[h] * acc_ref[h] + pvs[h]
        m_ref[h] = new_m[h]
        l_ref[h] = new_l[h]

    @pl.when(ki == nk - 1)
    def _():
        out = [acc_ref[h] / l_ref[h] for h in range(ATTN_HEADS)]
        o_ref[...] = jnp.concatenate(out, axis=0).astype(o_ref.dtype)


def _key_tile(n_keys):
    tk = min(n_keys, ATTN_K_TILE_MAX)
    while n_keys % tk:
        tk -= LANES
    return tk


def _attention(qt, k, vt, seq_len):
    b, s, _ = k.shape
    tq = min(ATTN_Q_TILE, seq_len)
    tk = _key_tile(s)
    nq = seq_len // tq
    nk = s // tk
    return pl.pallas_call(
        functools.partial(_attn_body, nk=nk),
        grid=(b, nq, nk),
        in_specs=[pl.BlockSpec((ATTN_WIDTH, tq), lambda bi, qi, ki: (0, bi * nq + qi)),
                  pl.BlockSpec((1, tk, KV_WIDTH), lambda bi, qi, ki: (bi, ki, 0)),
                  pl.BlockSpec((1, KV_WIDTH, tk), lambda bi, qi, ki: (bi, 0, ki))],
        out_specs=pl.BlockSpec((ATTN_WIDTH, tq), lambda bi, qi, ki: (0, bi * nq + qi)),
        out_shape=jax.ShapeDtypeStruct((ATTN_WIDTH, b * seq_len), BF16),
        scratch_shapes=[pltpu.VMEM((ATTN_HEADS, 1, tq), F32),
                        pltpu.VMEM((ATTN_HEADS, 1, tq), F32),
                        pltpu.VMEM((ATTN_HEADS, HEAD_DIM, tq), F32)],
        compiler_params=_cparams("parallel", "parallel", "arbitrary"),
        name="attention",
    )(qt, k, vt)


def _conv_body(first_ref, last_ref, cur_ref, prev_ref, next_ref, w_ref, b_ref, o_ref):
    i = pl.program_id(0)
    x = cur_ref[...]
    tm = x.shape[0]
    prev = jnp.where(first_ref[i] == 1, 0.0, prev_ref[...])
    nxt = jnp.where(last_ref[i] == 1, 0.0, next_ref[...])
    row8 = lax.broadcasted_iota(jnp.int32, prev.shape, 0)
    w = w_ref[...]
    half = SSD_CONV // 2
    acc = b_ref[...] + w[half:half + 1] * x
    for s in range(1, half + 1):
        rolled = pltpu.roll(x, s, 0)
        top = jnp.where(row8 < s, pltpu.roll(prev, s, 0), rolled[:8])
        acc = acc + w[half - s:half - s + 1] * jnp.concatenate([top, rolled[8:]], axis=0)
        rolled = pltpu.roll(x, tm - s, 0)
        bot = jnp.where(row8 >= 8 - s, pltpu.roll(nxt, 8 - s, 0), rolled[tm - 8:])
        acc = acc + w[half + s:half + s + 1] * jnp.concatenate([rolled[:tm - 8], bot], axis=0)
    o_ref[...] = acc * _sigmoid(acc)


def _ssd_conv(first, last, proj, conv_w, conv_b):
    n_tok = proj.shape[0]
    tm = TOKEN_TILE
    sub = tm // 8
    col = COL_XBC // XBC_WIDTH
    w8 = jnp.concatenate([conv_w, jnp.zeros((8 - SSD_CONV, XBC_WIDTH), F32)], axis=0)
    grid_spec = pltpu.PrefetchScalarGridSpec(
        num_scalar_prefetch=2,
        grid=(n_tok // tm,),
        in_specs=[pl.BlockSpec((tm, XBC_WIDTH), lambda i, f, l: (i, col)),
                  pl.BlockSpec((8, XBC_WIDTH), lambda i, f, l: (jnp.maximum(i * sub - 1, 0), col)),
                  pl.BlockSpec((8, XBC_WIDTH),
                               lambda i, f, l: (jnp.minimum((i + 1) * sub, n_tok // 8 - 1), col)),
                  pl.BlockSpec((8, XBC_WIDTH), lambda i, f, l: (0, 0)),
                  pl.BlockSpec((1, XBC_WIDTH), lambda i, f, l: (0, 0))],
        out_specs=pl.BlockSpec((tm, XBC_WIDTH), lambda i, f, l: (i, 0)))
    return pl.pallas_call(
        _conv_body, grid_spec=grid_spec,
        out_shape=jax.ShapeDtypeStruct((n_tok, XBC_WIDTH), F32),
        compiler_params=_cparams("parallel"),
        name="ssd_conv",
    )(first, last, proj, proj, proj, w8, conv_b.reshape(1, XBC_WIDTH))


def _softplus(x):
    return jnp.maximum(x, 0.0) + jnp.log1p(jnp.exp(-jnp.abs(x)))


def _ssd_body(*refs, nc, has_init):
    if has_init:
        (xs_ref, bm_ref, cm_ref, dt_ref, z_ref, dtb_ref, an_ref, drow_ref, nw_ref, h0_ref,
         y_ref, fin_ref, y_scr, h_scr) = refs
    else:
        (xs_ref, bm_ref, cm_ref, dt_ref, z_ref, dtb_ref, an_ref, drow_ref, nw_ref,
         y_ref, fin_ref, y_scr, h_scr) = refs
    sweep_id = pl.program_id(1)
    c = pl.program_id(2)
    chunk_len = SSD_CHUNK
    rep = SSD_HEADS // SSD_GROUPS

    @pl.when(c == 0)
    def _():
        if has_init:
            h_scr[...] = h0_ref[0, 0]
        else:
            h_scr[...] = jnp.zeros(h_scr.shape, F32)

    def sweep(direction):
        chunk = c if direction == 0 else nc - 1 - c
        row0 = pl.multiple_of(chunk * chunk_len, chunk_len)
        xs = xs_ref[...]
        bm = bm_ref[...]
        cm = cm_ref[...]
        dt_all = _softplus(dt_ref[...] + dtb_ref[...])
        la_all = dt_all * an_ref[...]
        ti = lax.broadcasted_iota(jnp.int32, (chunk_len, chunk_len), 0)
        si = lax.broadcasted_iota(jnp.int32, (chunk_len, chunk_len), 1)
        before_eq = (si <= ti) if direction == 0 else (si >= ti)
        cs_all = _mm_exact_lhs(jnp.where(before_eq, 1.0, 0.0).astype(BF16), la_all)
        cs_t = cs_all.T
        last = chunk_len - 1 if direction == 0 else 0
        b_g = [bm[:, g * SSD_STATE:(g + 1) * SSD_STATE].astype(BF16) for g in range(SSD_GROUPS)]
        c_g = [cm[:, g * SSD_STATE:(g + 1) * SSD_STATE].astype(BF16) for g in range(SSD_GROUPS)]
        cb = [_nt(c_g[g], b_g[g]) for g in range(SSD_GROUPS)]
        states = [h_scr[h] for h in range(SSD_HEADS)]
        y_off = [_nt(c_g[h // rep], states[h].astype(BF16)) for h in range(SSD_HEADS)]
        cs_c, xd, scores = [], [], []
        for h in range(SSD_HEADS):
            j = direction * SSD_HEADS + h
            cs_c.append(cs_all[:, j:j + 1])
            decay = jnp.exp(jnp.where(before_eq, cs_c[h] - cs_t[j:j + 1, :], -jnp.inf))
            xd.append(xs[:, h * HEAD_DIM:(h + 1) * HEAD_DIM] * dt_all[:, j:j + 1])
            scores.append((cb[h // rep] * decay).astype(BF16))
        y_diag = [_mm(scores[h], xd[h].astype(BF16)) for h in range(SSD_HEADS)]
        heads, tots, to_state = [], [], []
        for h in range(SSD_HEADS):
            j = direction * SSD_HEADS + h
            tots.append(cs_all[last:last + 1, j:j + 1])
            to_end = jnp.exp(tots[h] - cs_c[h])
            to_state.append(_tn((xd[h] * to_end).astype(BF16), b_g[h // rep]))
            heads.append(y_diag[h] + y_off[h] * jnp.exp(cs_c[h]))
        for h in range(SSD_HEADS):
            h_scr[h] = jnp.exp(tots[h]) * states[h] + to_state[h]
        y_dir = jnp.concatenate(heads, axis=-1)
        if direction == 0:
            y_scr[pl.ds(row0, chunk_len), :] = y_dir + drow_ref[...] * xs
        else:
            y = y_scr[pl.ds(row0, chunk_len), :] + y_dir
            z = z_ref[...]
            y = y * (z * _sigmoid(z))
            y_ref[...] = _rms(y, nw_ref[...]).astype(y_ref.dtype)

    @pl.when(sweep_id == 0)
    def _():
        sweep(0)

    @pl.when(sweep_id == 1)
    def _():
        sweep(1)

    @pl.when(c == nc - 1)
    def _():
        fin_ref[0, 0] = h_scr[...]


def _ssd(proj, xbc, tile_off, batch, seq_len, dt_bias, a_log, d_skip, norm_w, init):
    chunk_len = SSD_CHUNK
    nc = seq_len // chunk_len
    base = tile_off * (TOKEN_TILE // chunk_len)
    has_init = init is not None

    def row(b, s, c):
        return base + b * nc + c + s * (nc - 1 - 2 * c)

    def z_row(b, s, c):
        return base + b * nc + s * (nc - 1 - c) + (1 - s) * (nc - 1)

    def y_row(b, s, c):
        return b * nc + s * (nc - 1 - c) + (1 - s) * (nc - 1)

    dtb = jnp.zeros((1, LANES), F32).at[0, :2 * SSD_HEADS].set(dt_bias.reshape(-1))
    a_neg = jnp.zeros((1, LANES), F32).at[0, :2 * SSD_HEADS].set(-jnp.exp(a_log.reshape(-1)))
    d_row = jnp.repeat(d_skip, HEAD_DIM).reshape(1, SSD_WIDTH)
    in_specs = [
        pl.BlockSpec((chunk_len, SSD_WIDTH), lambda b, s, c: (row(b, s, c), 0)),
        pl.BlockSpec((chunk_len, LANES), lambda b, s, c: (row(b, s, c), SSD_WIDTH // LANES)),
        pl.BlockSpec((chunk_len, LANES), lambda b, s, c: (row(b, s, c), SSD_WIDTH // LANES + 1)),
        pl.BlockSpec((chunk_len, LANES), lambda b, s, c: (row(b, s, c), COL_DT // LANES)),
        pl.BlockSpec((chunk_len, SSD_WIDTH), lambda b, s, c: (z_row(b, s, c), COL_Z // SSD_WIDTH)),
        pl.BlockSpec((1, LANES), lambda b, s, c: (0, 0)),
        pl.BlockSpec((1, LANES), lambda b, s, c: (0, 0)),
        pl.BlockSpec((1, SSD_WIDTH), lambda b, s, c: (0, 0)),
        pl.BlockSpec((1, SSD_WIDTH), lambda b, s, c: (0, 0)),
    ]
    args = [xbc, xbc, xbc, proj, proj, dtb, a_neg, d_row, norm_w.reshape(1, SSD_WIDTH)]
    state_block = (1, 1, SSD_HEADS, HEAD_DIM, SSD_STATE)
    if has_init:
        in_specs.append(pl.BlockSpec(state_block, lambda b, s, c: (b, s, 0, 0, 0)))
        args.append(init)
    return pl.pallas_call(
        functools.partial(_ssd_body, nc=nc, has_init=has_init),
        grid=(batch, 2, nc),
        in_specs=in_specs,
        out_specs=[pl.BlockSpec((chunk_len, SSD_WIDTH), lambda b, s, c: (y_row(b, s, c), 0)),
                   pl.BlockSpec(state_block, lambda b, s, c: (b, s, 0, 0, 0))],
        out_shape=[jax.ShapeDtypeStruct((batch * seq_len, SSD_WIDTH), BF16),
                   jax.ShapeDtypeStruct((batch, 2, SSD_HEADS, HEAD_DIM, SSD_STATE), F32)],
        scratch_shapes=[pltpu.VMEM((seq_len, SSD_WIDTH), F32),
                        pltpu.VMEM((SSD_HEADS, HEAD_DIM, SSD_STATE), F32)],
        compiler_params=_cparams("parallel", "arbitrary", "arbitrary"),
        name="ssd",
    )(*args)


def _rwkv_body(*refs, nc, seq_len, has_init):
    n_io = 10 * RWKV_SEQS
    io_refs = [refs[5 * i:5 * i + 5] for i in range(2 * RWKV_SEQS)]
    (w0_ref, wuph_ref, wupl_ref, a0_ref, auph_ref, aupl_ref, gup_ref, kk_ref, ka_ref, rk_ref,
     lnw_ref, lnb_ref, heads_ref) = refs[n_io:n_io + 13]
    if has_init:
        s0_ref, y_ref, fin_ref, y_scr, s_scr = refs[n_io + 13:]
    else:
        y_ref, fin_ref, y_scr, s_scr = refs[n_io + 13:]

    def head_sum(x):
        return _mm_exact_rhs(x, heads_ref[...])
    c = pl.program_id(1)
    cl = RWKV_CHUNK
    n = HEAD_DIM
    lora = LANES // 2

    @pl.when(c == 0)
    def _():
        if has_init:
            s_scr[...] = s0_ref[...]
        else:
            s_scr[...] = jnp.zeros(s_scr.shape, F32)

    ti = lax.broadcasted_iota(jnp.int32, (cl, cl), 0)
    si = lax.broadcasted_iota(jnp.int32, (cl, cl), 1)
    lane2 = lax.broadcasted_iota(jnp.int32, (cl, 2 * cl), 1)
    left = lane2 < cl
    sign2 = jnp.where(left, 1.0, -1.0)
    t2 = lax.broadcasted_iota(jnp.int32, (cl, 2 * cl), 0)
    s2 = jnp.where(left, lane2, lane2 - cl)

    chains = []
    for idx, (r_ref, k_ref, v_ref, walo_ref, _) in enumerate(io_refs):
        seq, direction = divmod(idx, 2)
        r = r_ref[...]
        k = k_ref[...]
        v = v_ref[...]
        walo = walo_ref[...]
        zw = w0_ref[direction:direction + 1] + _mm_3pass(
            jnp.tanh(walo[:, :lora]), wuph_ref[direction], wupl_ref[direction])
        lw = -math.exp(-0.5) * _sigmoid(zw)
        a = _sigmoid(a0_ref[direction:direction + 1] + _mm_3pass(
            walo[:, lora:], auph_ref[direction], aupl_ref[direction]))
        kd = k * (1.0 + (a - 1.0) * ka_ref[...])
        kk_raw = k * kk_ref[...]
        kk_all = kk_raw * lax.rsqrt(head_sum(kk_raw * kk_raw) + 1e-12)
        akk_all = a * kk_all
        before_eq = (si <= ti) if direction == 0 else (si >= ti)
        before = (si < ti) if direction == 0 else (si > ti)
        before_eq2 = (s2 <= t2) if direction == 0 else (s2 >= t2)
        lp_all = _mm_exact_lhs(jnp.where(before_eq, 1.0, 0.0).astype(BF16), lw)
        last = cl - 1 if direction == 0 else 0
        lp_last = lp_all[last:last + 1]
        dec_in = jnp.exp(lp_all)
        dec_out = jnp.exp(-lp_all)
        dec_before = jnp.exp(lp_all - lw)
        to_end = jnp.exp(lp_last - lp_all)
        end_decay = jnp.exp(lp_last)
        kap_all = (kk_all * dec_before).astype(BF16)
        rt_all = (r * dec_in).astype(BF16)
        kt_all = (kd * dec_out).astype(BF16)
        bt_all = (akk_all * dec_out).astype(BF16)
        kend_all = (kd * to_end).astype(BF16)
        bend_all = (akk_all * to_end).astype(BF16)
        for h in range(RWKV_HEADS):
            sl = slice(h * n, (h + 1) * n)
            chains.append(dict(
                where=(seq, direction, h), v=v[:, sl],
                kap_rt=jnp.concatenate([kap_all[:, sl], rt_all[:, sl]], axis=0),
                kt_bt=jnp.concatenate([kt_all[:, sl], bt_all[:, sl]], axis=0),
                state=s_scr[seq, direction, h], before=before, before_eq2=before_eq2,
                end_decay=end_decay[:, sl],
                kend=jnp.concatenate([kend_all[:, sl], bend_all[:, sl]], axis=0)))

    for ch in chains:
        ch['g'] = _nt(ch['kap_rt'], ch['kt_bt'])
    for ch in chains:
        ch['fs'] = _nt(ch['kap_rt'], ch['state'].astype(BF16))
    for ch in chains:
        g = ch.pop('g')
        ch['n_mat'] = jnp.where(ch['before'], g[:cl, :cl], 0.0).astype(BF16)
        ch['l_mat'] = jnp.where(ch['before'], g[:cl, cl:], 0.0)
        ch['rkb'] = (jnp.where(ch['before_eq2'], g[cl:], 0.0) * sign2).astype(BF16)
    for ch in chains:
        x0 = ch['fs'][:cl] + _mm(ch.pop('n_mat'), ch['v'].astype(BF16))
        ch['z'] = jnp.concatenate([-ch.pop('l_mat'), x0], axis=1)

    for _ in range(int(math.log2(cl))):
        for ch in chains:
            z = ch['z']
            pz = _mm(z[:, :cl].astype(BF16), z.astype(BF16))
            ch['z'] = jnp.where(left, pz, z + pz)

    for ch in chains:
        u = ch['z'][:, cl:]
        ch['y'] = ch['fs'][cl:] + _mm(ch['rkb'],
                                      jnp.concatenate([ch['v'], u], axis=0).astype(BF16))
    for ch in chains:
        seq, direction, h = ch['where']
        v_nu = jnp.concatenate([ch['v'], -ch['z'][:, cl:]], axis=0).astype(BF16)
        s_scr[seq, direction, h] = ch['state'] * ch['end_decay'] + _tn(v_nu, ch['kend'])

    def finish(y, refs_):
        r_ref, k_ref, v_ref, _, glo_ref = refs_
        r = r_ref[...]
        v = v_ref[...]
        gate = _mm(_sigmoid(glo_ref[...]).astype(BF16), gup_ref[...])
        bonus = head_sum(r * k_ref[...] * rk_ref[...]) * v
        centred = y - head_sum(y) * (1.0 / n)
        var = head_sum(centred * centred) * (1.0 / n)
        y_n = centred * lax.rsqrt(var + GN_EPS)
        return ((y_n * lnw_ref[...] + lnb_ref[...] + bonus) * gate).astype(y_ref.dtype)

    ys, rows = [], []
    for idx in range(2 * RWKV_SEQS):
        seq, direction = divmod(idx, 2)
        ys.append(jnp.concatenate(
            [ch['y'] for ch in chains[idx * RWKV_HEADS:(idx + 1) * RWKV_HEADS]], axis=-1))
        chunk = c if direction == 0 else nc - 1 - c
        rows.append(pl.multiple_of(seq * seq_len + chunk * cl, cl))

    @pl.when(c < nc // 2)
    def _():
        for idx in range(2 * RWKV_SEQS):
            y_scr[pl.ds(rows[idx], cl), :] = ys[idx]

    @pl.when(c >= nc // 2)
    def _():
        for idx in range(2 * RWKV_SEQS):
            y_ref[pl.ds(rows[idx], cl), :] = finish(y_scr[pl.ds(rows[idx], cl), :] + ys[idx],
                                                    io_refs[idx])

    @pl.when(c == nc - 1)
    def _():
        fin_ref[...] = s_scr[...]


def _rwkv(proj, tile_off, batch, seq_len, lp, init):
    cl = RWKV_CHUNK
    nc = seq_len // cl
    assert nc % 2 == 0 and batch % RWKV_SEQS == 0
    base = tile_off * (TOKEN_TILE // cl)
    has_init = init is not None
    w = RWKV_WIDTH

    def const(shape):
        return pl.BlockSpec(shape, lambda b, c: (0,) * len(shape))

    def chunk_specs(seq, direction):
        def row(b, c):
            first = base + (b * RWKV_SEQS + seq) * nc
            return first + (c if direction == 0 else nc - 1 - c)
        return [pl.BlockSpec((cl, w), lambda b, c: (row(b, c), COL_R // w)),
                pl.BlockSpec((cl, w), lambda b, c: (row(b, c), COL_KR // w)),
                pl.BlockSpec((cl, w), lambda b, c: (row(b, c), COL_VR // w)),
                pl.BlockSpec((cl, LANES), lambda b, c: (row(b, c), COL_WALO // LANES)),
                pl.BlockSpec((cl, LANES), lambda b, c: (row(b, c), COL_GLO // LANES))]

    in_specs = []
    for seq in range(RWKV_SEQS):
        for direction in range(2):
            in_specs += chunk_specs(seq, direction)
    lora_shape = (2, LANES // 2, w)
    in_specs += [const((2, w)), const(lora_shape), const(lora_shape), const((2, w)),
                 const(lora_shape), const(lora_shape), const((LANES, w)),
                 const((1, w)), const((1, w)), const((1, w)), const((1, w)), const((1, w)),
                 const((w, w))]
    wup_hi, wup_lo = _split2(lp['rwkv_w_up'])
    aup_hi, aup_lo = _split2(lp['rwkv_a_up'])
    head_of = np.arange(w) // HEAD_DIM
    same_head = jnp.asarray((head_of[:, None] == head_of[None, :]).astype(np.float32), BF16)
    args = [proj] * (10 * RWKV_SEQS) + [
        lp['rwkv_w0'], wup_hi, wup_lo, lp['rwkv_a0'], aup_hi, aup_lo,
        lp['rwkv_g_up'].astype(BF16), lp['rwkv_k_k'].reshape(1, w), lp['rwkv_k_a'].reshape(1, w),
        lp['rwkv_r_k'].reshape(1, w), lp['rwkv_ln_w'].reshape(1, w),
        lp['rwkv_ln_b'].reshape(1, w), same_head]
    state_block = (RWKV_SEQS, 2, RWKV_HEADS, HEAD_DIM, HEAD_DIM)
    if has_init:
        in_specs.append(pl.BlockSpec(state_block, lambda b, c: (b, 0, 0, 0, 0)))
        args.append(init)
    return pl.pallas_call(
        functools.partial(_rwkv_body, nc=nc, seq_len=seq_len, has_init=has_init),
        grid=(batch // RWKV_SEQS, nc),
        in_specs=in_specs,
        out_specs=[pl.BlockSpec((RWKV_SEQS * seq_len, w), lambda b, c: (b, 0)),
                   pl.BlockSpec(state_block, lambda b, c: (b, 0, 0, 0, 0))],
        out_shape=[jax.ShapeDtypeStruct((batch * seq_len, w), BF16),
                   jax.ShapeDtypeStruct((batch, 2, RWKV_HEADS, HEAD_DIM, HEAD_DIM), F32)],
        scratch_shapes=[pltpu.VMEM((RWKV_SEQS * seq_len, w), F32),
                        pltpu.VMEM(state_block, F32)],
        compiler_params=_cparams("parallel", "arbitrary"),
        name="rwkv",
    )(*args)


def _out_proj_body(tmod_ref, attn_ref, ssd_ref, rwkv_ref, h_ref, mod_ref, nw_ref, w_ref, rwh_ref,
                   rwl_ref, rb_ref, h_out, u_out, lg_out):
    del tmod_ref
    mixed = (_tn(attn_ref[...], w_ref[:ATTN_WIDTH])
             + _mm(ssd_ref[...], w_ref[ATTN_WIDTH:ATTN_WIDTH + SSD_WIDTH])
             + _mm(rwkv_ref[...], w_ref[ATTN_WIDTH + SSD_WIDTH:]))
    mod = mod_ref[0]
    h = h_ref[...] + mod[2:3] * mixed
    h_out[...] = h
    u = _rms(h, nw_ref[...]) * (1.0 + mod[4:5]) + mod[3:4]
    u_out[...] = u
    lg_out[...] = _mm_3pass(u, rwh_ref[...], rwl_ref[...]) + rb_ref[...]


def _out_proj(tile_mod, attn_t, ssd, rwkv, h, mod, norm_w, w, router_w, router_b):
    n_tok, d = h.shape
    tm = TOKEN_TILE
    rw = jnp.zeros((d, LANES), F32).at[:, :N_EXPERTS].set(router_w)
    rw_hi, rw_lo = _split2(rw)
    rb = jnp.zeros((1, LANES), F32).at[0, :N_EXPERTS].set(router_b)
    grid_spec = pltpu.PrefetchScalarGridSpec(
        num_scalar_prefetch=1,
        grid=(n_tok // tm,),
        in_specs=[pl.BlockSpec((ATTN_WIDTH, tm), lambda i, t: (0, i)),
                  pl.BlockSpec((tm, SSD_WIDTH), lambda i, t: (i, 0)),
                  pl.BlockSpec((tm, RWKV_WIDTH), lambda i, t: (i, 0)),
                  pl.BlockSpec((tm, d), lambda i, t: (i, 0)),
                  pl.BlockSpec((1, 6, d), lambda i, t: (t[i], 0, 0)),
                  pl.BlockSpec((1, d), lambda i, t: (0, 0)),
                  pl.BlockSpec((d, d), lambda i, t: (0, 0)),
                  pl.BlockSpec((d, LANES), lambda i, t: (0, 0)),
                  pl.BlockSpec((d, LANES), lambda i, t: (0, 0)),
                  pl.BlockSpec((1, LANES), lambda i, t: (0, 0))],
        out_specs=[pl.BlockSpec((tm, d), lambda i, t: (i, 0)),
                   pl.BlockSpec((tm, d), lambda i, t: (i, 0)),
                   pl.BlockSpec((tm, LANES), lambda i, t: (i, 0))])
    return pl.pallas_call(
        _out_proj_body, grid_spec=grid_spec,
        out_shape=[jax.ShapeDtypeStruct((n_tok, d), F32),
                   jax.ShapeDtypeStruct((n_tok, d), F32),
                   jax.ShapeDtypeStruct((n_tok, LANES), F32)],
        compiler_params=_cparams("parallel"),
        name="out_proj",
    )(tile_mod, attn_t, ssd, rwkv, h, mod, norm_w.reshape(1, d), w, rw_hi, rw_lo, rb)


def _run_copies(i, start_ref, nch_ref, dst_ref, tot_ref, make_copy):
    def per_expert(e, carry):
        idx = i * N_EXPERTS + e
        start = start_ref[idx]
        dst = dst_ref[idx]

        def per_chunk(j, c2):
            make_copy(pl.multiple_of(start + j * RUN_CHUNK, RUN_CHUNK),
                      pl.multiple_of(dst + j * RUN_CHUNK, RUN_CHUNK)).start()
            return c2

        lax.fori_loop(0, nch_ref[idx], per_chunk, 0)
        return carry

    lax.fori_loop(0, N_EXPERTS, per_expert, 0)

    def wait_one(j, carry):
        make_copy(0, 0).wait()
        return carry

    lax.fori_loop(0, tot_ref[i], wait_one, 0)


def _zero_unused_rows(tail_start_ref, tail_nch_ref, nv_ref, o_hbm, zeros, sem):
    zeros[...] = jnp.zeros(zeros.shape, zeros.dtype)
    n_blocks = o_hbm.shape[0] // MOE_ROWS

    def tail_copy(row):
        return pltpu.make_async_copy(zeros.at[pl.ds(0, RUN_CHUNK)],
                                     o_hbm.at[pl.ds(row, RUN_CHUNK)], sem)

    def block_copy(row):
        return pltpu.make_async_copy(zeros, o_hbm.at[pl.ds(row, MOE_ROWS)], sem)

    def per_expert(e, carry):
        def per_chunk(j, c2):
            tail_copy(pl.multiple_of(tail_start_ref[e] + j * RUN_CHUNK, RUN_CHUNK)).start()
            return c2
        lax.fori_loop(0, tail_nch_ref[e], per_chunk, 0)
        return carry

    lax.fori_loop(0, N_EXPERTS, per_expert, 0)

    def per_block(b, carry):
        block_copy(pl.multiple_of(b * MOE_ROWS, MOE_ROWS)).start()
        return carry

    lax.fori_loop(nv_ref[0], n_blocks, per_block, 0)

    def wait_tails(e, carry):
        def wait_chunk(j, c2):
            tail_copy(0).wait()
            return c2
        lax.fori_loop(0, tail_nch_ref[e], wait_chunk, 0)
        return carry

    lax.fori_loop(0, N_EXPERTS, wait_tails, 0)

    def wait_block(b, carry):
        block_copy(0).wait()
        return carry

    lax.fori_loop(nv_ref[0], n_blocks, wait_block, 0)


def _dispatch_body(start_ref, nch_ref, dst_ref, tot_ref, tail_start_ref, tail_nch_ref, nv_ref,
                   colt_ref, u_ref, o_hbm, buf, zeros, sem, zsem):
    i = pl.program_id(0)

    @pl.when(i == 0)
    def _():
        _zero_unused_rows(tail_start_ref, tail_nch_ref, nv_ref, o_hbm, zeros, zsem)

    colt = colt_ref[0]
    stage_row = lax.broadcasted_iota(jnp.int32, (buf.shape[0], colt.shape[1]), 0)
    sel = jnp.where(stage_row == colt[0:1], 1.0, 0.0)
    for kk in range(1, TOP_K):
        sel = sel + jnp.where(stage_row == colt[kk:kk + 1], 1.0, 0.0)
    buf[...] = _mm(sel.astype(BF16), u_ref[...].astype(BF16))

    def make_copy(hbm_row, stage):
        return pltpu.make_async_copy(buf.at[pl.ds(stage, RUN_CHUNK)],
                                     o_hbm.at[pl.ds(hbm_row, RUN_CHUNK)], sem)

    _run_copies(i, start_ref, nch_ref, dst_ref, tot_ref, make_copy)


def _stage_rows(tm):
    return TOP_K * tm + N_EXPERTS * RUN_CHUNK


def _moe_dispatch(u, tables, tails, n_valid, n_rows):
    n_tok, d = u.shape
    tm = MOE_TOKEN_TILE
    start, nch, dst, tot, col = tables
    tail_start, tail_nch = tails
    col_t = col.reshape(n_tok // tm, tm, TOP_K).transpose(0, 2, 1)
    grid_spec = pltpu.PrefetchScalarGridSpec(
        num_scalar_prefetch=7,
        grid=(n_tok // tm,),
        in_specs=[pl.BlockSpec((1, TOP_K, tm), lambda i, *_: (i, 0, 0)),
                  pl.BlockSpec((tm, d), lambda i, *_: (i, 0))],
        out_specs=pl.BlockSpec(memory_space=pl.ANY),
        scratch_shapes=[pltpu.VMEM((_stage_rows(tm), d), F32), pltpu.VMEM((MOE_ROWS, d), F32),
                        pltpu.SemaphoreType.DMA, pltpu.SemaphoreType.DMA])
    return pl.pallas_call(
        _dispatch_body, grid_spec=grid_spec,
        out_shape=jax.ShapeDtypeStruct((n_rows, d), u.dtype),
        compiler_params=_cparams("arbitrary"),
        name="moe_dispatch",
    )(start, nch, dst, tot, tail_start, tail_nch, n_valid, col_t, u)


def _moe_body(be_ref, nv_ref, new_ref, x_ref, wgu_ref, bgu_ref, wdn_ref, bdn_ref, o_ref,
              wgu_bf, wdn_bf):
    del be_ref
    i = pl.program_id(0)

    @pl.when(new_ref[i] == 1)
    def _():
        wgu_bf[...] = wgu_ref[0, 0].astype(BF16)
        wdn_bf[...] = wdn_ref[0, 0].astype(BF16)

    @pl.when(i < nv_ref[0])
    def _():
        gu = _mm(x_ref[...].astype(BF16), wgu_bf[...]) + bgu_ref[0, 0]
        gate = jnp.minimum(gu[:, :D_EXPERT], SWIGLU_LIMIT)
        up = jnp.clip(gu[:, D_EXPERT:], -SWIGLU_LIMIT, SWIGLU_LIMIT)
        act = (up + 1.0) * gate * _sigmoid(SWIGLU_ALPHA * gate)
        o_ref[...] = _mm(act.astype(BF16), wdn_bf[...]) + bdn_ref[0, 0]

    @pl.when(i >= nv_ref[0])
    def _():
        o_ref[...] = jnp.zeros(o_ref.shape, o_ref.dtype)


def _moe_experts(block_expert, n_valid, x_rows, layer, w_gu, b_gu, w_dn, b_dn):
    n_rows, d = x_rows.shape
    rows = MOE_ROWS
    depth = w_gu.shape[0]
    new_expert = jnp.concatenate([jnp.ones((1,), jnp.int32),
                                  (block_expert[1:] != block_expert[:-1]).astype(jnp.int32)])
    grid_spec = pltpu.PrefetchScalarGridSpec(
        num_scalar_prefetch=3,
        grid=(n_rows // rows,),
        in_specs=[pl.BlockSpec((rows, d), lambda i, be, nv, ne: (i, 0)),
                  pl.BlockSpec((1, 1, d, 2 * D_EXPERT), lambda i, be, nv, ne: (layer, be[i], 0, 0)),
                  pl.BlockSpec((1, 1, 1, 2 * D_EXPERT), lambda i, be, nv, ne: (layer, be[i], 0, 0)),
                  pl.BlockSpec((1, 1, D_EXPERT, d), lambda i, be, nv, ne: (layer, be[i], 0, 0)),
                  pl.BlockSpec((1, 1, 1, d), lambda i, be, nv, ne: (layer, be[i], 0, 0))],
        out_specs=pl.BlockSpec((rows, d), lambda i, be, nv, ne: (i, 0)),
        scratch_shapes=[pltpu.VMEM((d, 2 * D_EXPERT), BF16), pltpu.VMEM((D_EXPERT, d), BF16)])
    return pl.pallas_call(
        _moe_body, grid_spec=grid_spec,
        out_shape=jax.ShapeDtypeStruct((n_rows, d), F32),
        compiler_params=_cparams("arbitrary"),
        name="moe_experts",
    )(block_expert, n_valid, new_expert, x_rows, w_gu, b_gu.reshape(depth, N_EXPERTS, 1, -1),
      w_dn, b_dn.reshape(depth, N_EXPERTS, 1, -1))


def _combine_body(tmod_ref, start_ref, nch_ref, dst_ref, tot_ref, col_ref, gates_ref, h_ref,
                  mod_ref, fw_ref, y_hbm, o_ref, *rest, final_ctx_tiles):
    del tmod_ref
    o2_ref, (buf, sem) = rest[:-2], rest[-2:]
    i = pl.program_id(0)

    @pl.when(i == 0)
    def _():
        buf[...] = jnp.zeros(buf.shape, buf.dtype)

    def make_copy(hbm_row, stage):
        return pltpu.make_async_copy(y_hbm.at[pl.ds(hbm_row, RUN_CHUNK)],
                                     buf.at[pl.ds(stage, RUN_CHUNK)], sem)

    _run_copies(i, start_ref, nch_ref, dst_ref, tot_ref, make_copy)
    col = col_ref[...]
    gates = gates_ref[...]
    stage_row = lax.broadcasted_iota(jnp.int32, (col.shape[0], buf.shape[0]), 1)
    sel = jnp.where(stage_row == col[:, 0:1], gates[:, 0:1], 0.0)
    for kk in range(1, TOP_K):
        sel = sel + jnp.where(stage_row == col[:, kk:kk + 1], gates[:, kk:kk + 1], 0.0)
    sel_hi, sel_lo = _split2(sel)
    rows_bf = buf[...].astype(BF16)
    y = _mm(sel_hi, rows_bf) + _mm(sel_lo, rows_bf)
    h = h_ref[...] + mod_ref[0][5:6] * y
    if final_ctx_tiles is None:
        o_ref[...] = h
    else:
        oc_ref, ol_ref = o_ref, o2_ref[0]
        out = _rms(h, fw_ref[...])

        @pl.when(i < final_ctx_tiles)
        def _():
            oc_ref[...] = out

        @pl.when(i >= final_ctx_tiles)
        def _():
            ol_ref[...] = out


def _moe_combine(tile_mod, tables, gates, h, mod, final_w, y_rows, final_ctx_tiles):
    n_tok, d = h.shape
    tm = MOE_TOKEN_TILE
    ratio = TOKEN_TILE // tm
    start, nch, dst, tot, col = tables
    if final_ctx_tiles is None:
        out_specs = pl.BlockSpec((tm, d), lambda i, *_: (i, 0))
        out_shape = jax.ShapeDtypeStruct((n_tok, d), F32)
    else:
        c = final_ctx_tiles
        out_specs = [pl.BlockSpec((tm, d), lambda i, *_: (jnp.minimum(i, c - 1), 0)),
                     pl.BlockSpec((tm, d), lambda i, *_: (jnp.maximum(i - c, 0), 0))]
        out_shape = [jax.ShapeDtypeStruct((c * tm, d), F32),
                     jax.ShapeDtypeStruct((n_tok - c * tm, d), F32)]
    grid_spec = pltpu.PrefetchScalarGridSpec(
        num_scalar_prefetch=5,
        grid=(n_tok // tm,),
        in_specs=[pl.BlockSpec((tm, TOP_K), lambda i, *_: (i, 0)),
                  pl.BlockSpec((tm, TOP_K), lambda i, *_: (i, 0)),
                  pl.BlockSpec((tm, d), lambda i, *_: (i, 0)),
                  pl.BlockSpec((1, 6, d), lambda i, t, *_: (t[i // ratio], 0, 0)),
                  pl.BlockSpec((1, d), lambda i, *_: (0, 0)),
                  pl.BlockSpec(memory_space=pl.ANY)],
        out_specs=out_specs,
        scratch_shapes=[pltpu.VMEM((_stage_rows(tm), d), F32), pltpu.SemaphoreType.DMA])
    return pl.pallas_call(
        functools.partial(_combine_body, final_ctx_tiles=final_ctx_tiles), grid_spec=grid_spec,
        out_shape=out_shape,
        compiler_params=_cparams("arbitrary"),
        name="moe_combine",
    )(tile_mod, start, nch, dst, tot, col, gates, h, mod, final_w.reshape(1, d), y_rows)


def _route(logits):
    n_tok = logits.shape[0]
    rows = MOE_ROWS
    tm = MOE_TOKEN_TILE
    tiles = n_tok // tm
    slack = tiles * N_EXPERTS * (RUN_CHUNK - 1)
    n_blocks = n_tok * TOP_K // rows + N_EXPERTS + (slack + rows - 1) // rows
    top_logit, top_idx = lax.top_k(logits, TOP_K)
    gates = jax.nn.softmax(top_logit, axis=-1)
    hit = top_idx[:, :, None] == jnp.arange(N_EXPERTS, dtype=top_idx.dtype)[None, None, :]
    per_tok = jnp.sum(hit, axis=1, dtype=jnp.int32).reshape(tiles, tm, N_EXPERTS)
    earlier = jnp.tril(jnp.ones((tm, tm), F32), k=-1)
    in_tile_rank = jnp.einsum('ts,nse->nte', earlier, per_tok.astype(F32),
                              precision=HI).astype(jnp.int32)
    run_len = per_tok.sum(axis=1)
    n_chunks = (run_len + RUN_CHUNK - 1) // RUN_CHUNK
    run_rows = n_chunks * RUN_CHUNK
    rows_before = jnp.cumsum(run_rows, axis=0) - run_rows
    counts = rows_before[-1] + run_rows[-1]
    padded = (counts + rows - 1) // rows * rows
    padded_end = jnp.cumsum(padded)
    padded_start = padded_end - padded
    block_start = jnp.arange(n_blocks, dtype=jnp.int32) * rows
    block_expert = jnp.minimum(
        jnp.sum(block_start[:, None] >= padded_end[None, :], axis=1, dtype=jnp.int32),
        N_EXPERTS - 1)
    n_valid = (padded_end[-1] // rows).astype(jnp.int32).reshape(1)
    run_start = padded_start[None, :] + rows_before
    stage_off = (jnp.cumsum(n_chunks, axis=1) - n_chunks) * RUN_CHUNK
    total_chunks = jnp.sum(n_chunks, axis=1)
    stage_row = (stage_off[:, None, :] + in_tile_rank).reshape(n_tok, 1, N_EXPERTS)
    col = jnp.sum(jnp.where(hit, stage_row, 0), axis=-1)
    tables = (run_start.reshape(-1).astype(jnp.int32), n_chunks.reshape(-1).astype(jnp.int32),
              stage_off.reshape(-1).astype(jnp.int32), total_chunks.astype(jnp.int32),
              col.astype(jnp.int32))
    tails = ((padded_start + counts).astype(jnp.int32),
             ((padded - counts) // RUN_CHUNK).astype(jnp.int32))
    return tables, tails, gates, block_expert, n_valid, n_blocks * rows


def _permute_w_in(w_in):
    def cols(a, b):
        return w_in[:, a:b]
    q, k, v, z = cols(0, 384), cols(384, 512), cols(512, 640), cols(640, 1024)
    xbc, dt = cols(1024, 1664), cols(1664, 1676)
    r, kr, vr = cols(1676, 1932), cols(1932, 2188), cols(2188, 2444)
    walo, glo = cols(2444, 2572), cols(2572, 2700)
    dt = jnp.pad(dt, ((0, 0), (0, LANES - dt.shape[1])))
    return jnp.concatenate([q, z, r, kr, k, v, dt, walo, glo, xbc, vr], axis=1).astype(BF16)


def kernel(x_prompt, x_sample, cache_attn_k, cache_attn_v, state_ssd, state_rwkv, c, c_ctx,
           norm1_w, norm2_w, ada_w, ada_b, w_in, q_norm_w, k_norm_w,
           ssd_conv_w, ssd_conv_b, ssd_dt_bias, ssd_a_log, ssd_d, ssd_norm_w,
           rwkv_w0, rwkv_w_up, rwkv_a0, rwkv_a_up, rwkv_g_up, rwkv_k_k, rwkv_k_a, rwkv_r_k,
           rwkv_ln_w, rwkv_ln_b, w_out, router_w, router_b,
           moe_w_gate_up, moe_b_gate_up, moe_w_down, moe_b_down, final_norm_w):
    batch, seq, d = x_prompt.shape
    dec_batch, dec_seq, _ = x_sample.shape
    depth = norm1_w.shape[0]
    past = cache_attn_k.shape[2]
    n_ctx, n_lat = batch * seq, dec_batch * dec_seq
    tm = TOKEN_TILE
    ctx_tiles, lat_tiles = n_ctx // tm, n_lat // tm

    tile_mod = np.concatenate([np.zeros(ctx_tiles, np.int32),
                               1 + np.arange(lat_tiles, dtype=np.int32) // (dec_seq // tm)])
    seq_tiles = np.concatenate([np.full(ctx_tiles, seq // tm), np.full(lat_tiles, dec_seq // tm)])
    tile_in_seq = np.concatenate([np.arange(ctx_tiles) % (seq // tm),
                                  np.arange(lat_tiles) % (dec_seq // tm)])
    first = jnp.asarray((tile_in_seq == 0).astype(np.int32))
    last = jnp.asarray((tile_in_seq == seq_tiles - 1).astype(np.int32))
    tile_mod = jnp.asarray(tile_mod)

    cond = jnp.concatenate([c_ctx[None, :], c], axis=0)
    pad_rows = -cond.shape[0] % 8
    cond = jnp.pad(cond, ((0, pad_rows), (0, 0)))
    mod_all = _ada(cond, ada_w, ada_b).reshape(depth, cond.shape[0], 6, d)

    rope_tabs = _rope_tables(dec_seq)
    h = (x_prompt.reshape(n_ctx, d), x_sample.reshape(n_lat, d))
    ks, vs, ssd_states, rwkv_states = [], [], [], []
    for l in range(depth):
        lp = dict(rwkv_w0=rwkv_w0[l], rwkv_w_up=rwkv_w_up[l], rwkv_a0=rwkv_a0[l],
                  rwkv_a_up=rwkv_a_up[l], rwkv_g_up=rwkv_g_up[l], rwkv_k_k=rwkv_k_k[l],
                  rwkv_k_a=rwkv_k_a[l], rwkv_r_k=rwkv_r_k[l], rwkv_ln_w=rwkv_ln_w[l],
                  rwkv_ln_b=rwkv_ln_b[l])
        mod = mod_all[l]
        proj, h = _in_proj(tile_mod, h, mod, norm1_w[l], _permute_w_in(w_in[l]))

        q_c, k_c, v_c, k_norm = _attn_prep(proj, 0, n_ctx, q_norm_w[l], k_norm_w[l], None, 1)
        q_l, k_l, v_l = _attn_prep(proj, ctx_tiles, n_lat, q_norm_w[l], k_norm_w[l], rope_tabs,
                                   dec_seq // tm)
        v_c = v_c.reshape(KV_WIDTH, batch, seq).transpose(1, 0, 2)
        attn_c = _attention(q_c, k_c.reshape(batch, seq, -1), v_c, seq)
        k_full = jnp.concatenate([cache_attn_k[:, l].reshape(dec_batch, past, -1).astype(BF16),
                                  k_l.reshape(dec_batch, dec_seq, -1)], axis=1)
        v_past = cache_attn_v[:, l].reshape(dec_batch, past, -1).astype(BF16).transpose(0, 2, 1)
        v_full = jnp.concatenate(
            [v_past, v_l.reshape(KV_WIDTH, dec_batch, dec_seq).transpose(1, 0, 2)], axis=2)
        attn_l = _attention(q_l, k_full, v_full, dec_seq)
        attn = jnp.concatenate([attn_c, attn_l], axis=1)
        ks.append(k_norm.reshape(batch, seq, ATTN_KV_HEADS, HEAD_DIM))
        vs.append(proj[:n_ctx, COL_V:COL_V + KV_WIDTH].reshape(batch, seq, ATTN_KV_HEADS, HEAD_DIM))

        xbc = _ssd_conv(first, last, proj, ssd_conv_w[l], ssd_conv_b[l])
        ssd_c, ssd_fin = _ssd(proj, xbc, 0, batch, seq, ssd_dt_bias[l], ssd_a_log[l], ssd_d[l],
                              ssd_norm_w[l], None)
        ssd_l, _ = _ssd(proj, xbc, ctx_tiles, dec_batch, dec_seq, ssd_dt_bias[l], ssd_a_log[l],
                        ssd_d[l], ssd_norm_w[l], state_ssd[:, l])
        ssd = jnp.concatenate([ssd_c, ssd_l], axis=0)
        ssd_states.append(ssd_fin)

        rwkv_c, rwkv_fin = _rwkv(proj, 0, batch, seq, lp, None)
        rwkv_l, _ = _rwkv(proj, ctx_tiles, dec_batch, dec_seq, lp, state_rwkv[:, l])
        rwkv = jnp.concatenate([rwkv_c, rwkv_l], axis=0)
        rwkv_states.append(rwkv_fin)

        h, u, logits = _out_proj(tile_mod, attn, ssd, rwkv, h, mod, norm2_w[l],
                                 w_out[l].astype(BF16), router_w[l], router_b[l])

        tables, tails, gates, block_expert, n_valid, n_rows = _route(logits[:, :N_EXPERTS])
        x_rows = _moe_dispatch(u, tables, tails, n_valid, n_rows)
        y_rows = _moe_experts(block_expert, n_valid, x_rows, l, moe_w_gate_up, moe_b_gate_up,
                              moe_w_down, moe_b_down)
        h = _moe_combine(tile_mod, tables, gates, h, mod, final_norm_w, y_rows,
                         n_ctx // MOE_TOKEN_TILE if l == depth - 1 else None)

    y_prompt = h[0].reshape(batch, seq, d)
    y_sample = h[1].reshape(dec_batch, dec_seq, d)
    return (y_prompt, y_sample, jnp.stack(ks, axis=1), jnp.stack(vs, axis=1),
            jnp.stack(ssd_states, axis=1), jnp.stack(rwkv_states, axis=1))
```
